```python
import math
import jax, jax.numpy as jnp
from jax import lax
import numpy as np

D_MODEL = 1024
BATCH = 8
SEQ = 4096
DEPTH = 2
DEC_BATCH = 128
DEC_SEQ = 4
PAST_LEN = 16384
PAGE_SIZE = 128

N_EVEN = (DEPTH + 1) // 2
N_ODD = DEPTH // 2
MLA_HEADS = 8
MLA_NOPE = 64
MLA_ROPE = 32
MLA_QK = MLA_NOPE + MLA_ROPE
MLA_V = 64
MLA_Q_LORA = 768
MLA_KV_LORA = 256
ROPE_THETA = 10000.0
Q_BLOCK = 128
S5_WIDTH = 512
S5_GROUP = 16
S5_GROUPS = S5_WIDTH // S5_GROUP
S5_STATE = 64
EVEN_IN = MLA_Q_LORA + MLA_KV_LORA + MLA_ROPE + S5_WIDTH
EVEN_OUT = MLA_HEADS * MLA_V + S5_WIDTH
HGRN_WIDTH = D_MODEL
HGRN_HEADS = 8
HGRN_DK = HGRN_WIDTH // HGRN_HEADS
HGRN_DV = HGRN_DK
HGRN_CHUNK = 64
MEM_LEN = 256
MEM_HEADS = 4
MEM_HEAD_DIM = 128
MEM_WIDTH = MEM_HEADS * MEM_HEAD_DIM
D_FF = 4 * D_MODEL
EPS = 1e-6

kernel_name = 'hybrid_mla_s5_hgrn2_decoder_step'


def rms_norm(x, g):
    xf = x.astype(jnp.float32)
    y = xf * lax.rsqrt(jnp.mean(xf * xf, axis=-1, keepdims=True) + EPS)
    return (y * g.astype(jnp.float32)).astype(x.dtype)


def rope(x, pos):
    half = x.shape[-1] // 2
    inv = ROPE_THETA ** (-jnp.arange(half, dtype=jnp.float32) / half)
    ang = pos.astype(jnp.float32)[:, None] * inv[None, :]
    cos = jnp.cos(ang)[None, :, None, :]
    sin = jnp.sin(ang)[None, :, None, :]
    xf = x.astype(jnp.float32)
    x1, x2 = xf[..., :half], xf[..., half:]
    return jnp.concatenate([x1 * cos - x2 * sin, x2 * cos + x1 * sin], axis=-1).astype(x.dtype)


def mla_keys(c_kv, k_rope, w_ukv, k_gain):
    b, t, _ = c_kv.shape
    kv = (c_kv @ w_ukv).reshape(b, t, MLA_HEADS, MLA_NOPE + MLA_V)
    kr = jnp.broadcast_to(k_rope[:, :, None, :], (b, t, MLA_HEADS, MLA_ROPE)).astype(kv.dtype)
    k = jnp.concatenate([kv[..., :MLA_NOPE], kr], axis=-1)
    return rms_norm(k, k_gain), kv[..., MLA_NOPE:]


def causal_block_attention(q, k, v):
    b, s, h, dq = q.shape
    qb = min(Q_BLOCK, s)
    nb = s // qb
    q_blocks = q.reshape(b, nb, qb, h, dq).swapaxes(0, 1)
    kpos = jnp.arange(s)

    def block(args):
        q_i, start = args
        qpos = start + jnp.arange(qb)
        sc = jnp.einsum('bqhd,bkhd->bhqk', q_i, k).astype(jnp.float32) * (MLA_QK ** -0.5)
        sc = jnp.where(qpos[:, None] >= kpos[None, :], sc, -jnp.inf)
        p = jax.nn.softmax(sc, axis=-1).astype(v.dtype)
        return jnp.einsum('bhqk,bkhd->bqhd', p, v)

    o = lax.map(block, (q_blocks, jnp.arange(nb) * qb))
    return o.swapaxes(0, 1).reshape(b, s, h, v.shape[-1])


def attn_stats(q, k, v, mask):
    sc = jnp.einsum('bqhd,bkhd->bhqk', q, k).astype(jnp.float32) * (MLA_QK ** -0.5)
    if mask is not None:
        sc = jnp.where(mask, sc, -jnp.inf)
    m = jnp.max(sc, axis=-1)
    p = jnp.exp(sc - m[..., None])
    return m, jnp.sum(p, axis=-1), jnp.einsum('bhqk,bkhd->bhqd', p, v.astype(jnp.float32))


def paged_mla_attention(q, ckv_new, kr_new, cache_lat, cache_kr, page_table, e, w_ukv, k_gain):
    def page_stats(phys):
        k, v = mla_keys(cache_lat[phys, e], cache_kr[phys, e], w_ukv, k_gain)
        return attn_stats(q, k, v, None)

    m_pg, l_pg, a_pg = lax.map(page_stats, page_table.T)
    k_n, v_n = mla_keys(ckv_new, kr_new, w_ukv, k_gain)
    sq = q.shape[1]
    m_n, l_n, a_n = attn_stats(q, k_n, v_n, jnp.tril(jnp.ones((sq, sq), bool)))
    m_all = jnp.concatenate([m_pg, m_n[None]], axis=0)
    l_all = jnp.concatenate([l_pg, l_n[None]], axis=0)
    a_all = jnp.concatenate([a_pg, a_n[None]], axis=0)
    m_max = jnp.max(m_all, axis=0)
    w = jnp.exp(m_all - m_max)
    o = jnp.sum(a_all * w[..., None], axis=0) / jnp.sum(l_all * w, axis=0)[..., None]
    return o.transpose(0, 2, 1, 3).astype(q.dtype)


def _cplx_affine_combine(e1, e2):
    a1r, a1i, b1r, b1i = e1
    a2r, a2i, b2r, b2i = e2
    return (a2r * a1r - a2i * a1i, a2r * a1i + a2i * a1r,
            a2r * b1r - a2i * b1i + b2r, a2r * b1i + a2i * b1r + b2i)


def s5_ssm(u, h0_re, h0_im, p):
    f32 = jnp.float32
    bsz, L, _ = u.shape
    lr = jnp.minimum(p['lam_re'].astype(f32), -1e-4)
    li = p['lam_im'].astype(f32)
    dt = jnp.exp(p['log_step'].astype(f32))[:, None]
    mag = jnp.exp(lr * dt)
    ab_re, ab_im = mag * jnp.cos(li * dt), mag * jnp.sin(li * dt)
    den = lr * lr + li * li
    co_re = ((ab_re - 1.0) * lr + ab_im * li) / den
    co_im = (ab_im * lr - (ab_re - 1.0) * li) / den
    br, bi = p['b_re'].astype(f32), p['b_im'].astype(f32)
    bb_re = co_re[..., None] * br - co_im[..., None] * bi
    bb_im = co_re[..., None] * bi + co_im[..., None] * br
    ug = u.astype(f32).reshape(bsz, L, S5_GROUPS, S5_GROUP)
    x_re = jnp.einsum('blgc,gnc->blgn', ug, bb_re)
    x_im = jnp.einsum('blgc,gnc->blgn', ug, bb_im)
    h0r, h0i = h0_re.astype(f32), h0_im.astype(f32)
    x_re = x_re.at[:, 0].add(ab_re * h0r - ab_im * h0i)
    x_im = x_im.at[:, 0].add(ab_re * h0i + ab_im * h0r)
    a_re = jnp.broadcast_to(ab_re, x_re.shape)
    a_im = jnp.broadcast_to(ab_im, x_im.shape)
    _, _, h_re, h_im = lax.associative_scan(_cplx_affine_combine, (a_re, a_im, x_re, x_im), axis=1)
    y = (jnp.einsum('blgn,gcn->blgc', h_re, p['c_re'].astype(f32))
         - jnp.einsum('blgn,gcn->blgc', h_im, p['c_im'].astype(f32)))
    y = y.reshape(bsz, L, S5_WIDTH) + p['d'].astype(f32) * u.astype(f32)
    z = jax.nn.gelu(y)
    out = z * jax.nn.sigmoid(z @ p['w_glu'].astype(f32) + p['b_glu'].astype(f32))
    return out.astype(u.dtype), h_re[:, -1], h_im[:, -1]


def even_mixer(hn, pos, h0_re, h0_im, attend, p):
    b, L, _ = hn.shape
    proj = hn @ p['w_in']
    o1 = MLA_Q_LORA
    o2 = o1 + MLA_KV_LORA
    o3 = o2 + MLA_ROPE
    c_q, c_kv, k_r, u = proj[..., :o1], proj[..., o1:o2], proj[..., o2:o3], proj[..., o3:]
    cq = rms_norm(c_q, p['cq_norm'])
    qf = rms_norm((cq @ p['w_uq']).reshape(b, L, MLA_HEADS, MLA_QK), p['q_gain'])
    q = jnp.concatenate([qf[..., :MLA_NOPE], rope(qf[..., MLA_NOPE:], pos)], axis=-1)
    ckv = rms_norm(c_kv, p['ckv_norm'])
    kr = rope(k_r[:, :, None, :], pos)[:, :, 0]
    o_att = attend(q, ckv, kr).reshape(b, L, MLA_HEADS * MLA_V)
    o_s5, hr, hi = s5_ssm(u, h0_re, h0_im, p)
    out = jnp.concatenate([o_att, o_s5.astype(o_att.dtype)], axis=-1) @ p['w_out']
    return out, ckv, kr, hr, hi


def hgrn_recurrence(q, k, v, logf, s0):
    bsz, L, H, _ = q.shape
    C = min(HGRN_CHUNK, L)
    nc = -(-L // C)
    pad = nc * C - L

    def prep(t):
        t = jnp.pad(t, ((0, 0), (0, pad), (0, 0), (0, 0)))
        return t.reshape(bsz, nc, C, H, t.shape[-1]).swapaxes(0, 1)

    qs, ks, vs, gs = prep(q), prep(k), prep(v), prep(logf)
    causal = jnp.tril(jnp.ones((C, C), bool))[None, :, :, None, None]

    def step(S, xs):
        qc, kc, vc, gc = xs
        bcum = jnp.cumsum(gc, axis=1)
        decay = jnp.exp(jnp.where(causal, bcum[:, :, None] - bcum[:, None, :], -jnp.inf))
        att = jnp.einsum('bthd,bshd,btshd->bhts', qc, kc, decay)
        o = (jnp.einsum('bhts,bshe->bthe', att, vc)
             + jnp.einsum('bthd,bhde->bthe', qc * jnp.exp(bcum), S))
        b_last = bcum[:, -1]
        S = (jnp.exp(b_last)[..., None] * S
             + jnp.einsum('bshd,bshe->bhde', kc * jnp.exp(b_last[:, None] - bcum), vc))
        return S, o

    S, o = lax.scan(step, s0, (qs, ks, vs, gs))
    o = o.swapaxes(0, 1).reshape(bsz, nc * C, H, v.shape[-1])[:, :L]
    return o, S


def odd_mixer(hn, s0, lb, p):
    b, L, _ = hn.shape
    q, f, i, g = jnp.split(hn @ p['w_in'], 4, axis=-1)
    forget = lb + (1.0 - lb) * jax.nn.sigmoid(f.astype(jnp.float32))
    logf = jnp.log(forget)
    k = 1.0 - forget

    def heads(t):
        return t.reshape(b, L, HGRN_HEADS, HGRN_DK)

    o, S = hgrn_recurrence(heads(jax.nn.silu(q.astype(jnp.float32))), heads(k),
                           heads(i.astype(jnp.float32)), heads(logf), s0.astype(jnp.float32))
    o = rms_norm(o.reshape(b, L, HGRN_WIDTH), p['out_norm']) * jax.nn.silu(g.astype(jnp.float32))
    return o.astype(hn.dtype) @ p['w_out'], S


def mem_kv(mem, g_src, w_k, w_v, k_gain):
    b, m, _ = mem.shape
    mn = rms_norm(mem, g_src)
    k = rms_norm((mn @ w_k).reshape(b, m, MEM_HEADS, MEM_HEAD_DIM), k_gain)
    v = (mn @ w_v).reshape(b, m, MEM_HEADS, MEM_HEAD_DIM)
    return k, v


def mem_attend(hn, k, v, w_q, q_gain, w_o):
    b, L, _ = hn.shape
    q = rms_norm((hn @ w_q).reshape(b, L, MEM_HEADS, MEM_HEAD_DIM), q_gain)
    sc = jnp.einsum('bqhd,bkhd->bhqk', q, k).astype(jnp.float32) * (MEM_HEAD_DIM ** -0.5)
    pr = jax.nn.softmax(sc, axis=-1).astype(v.dtype)
    o = jnp.einsum('bhqk,bkhd->bqhd', pr, v).reshape(b, L, MEM_WIDTH)
    return o @ w_o


def sq_relu_mlp(hn, w_up, w_down):
    return jnp.square(jax.nn.relu(hn @ w_up)) @ w_down


def setup_inputs(seed: int = 0) -> dict:
    key = jax.random.key(seed)
    keys = iter(jax.random.split(key, 64))

    def nrm(shape, scale=1.0):
        return jax.random.normal(next(keys), shape, jnp.float32) * scale

    def gain(shape):
        return 1.0 + 0.02 * nrm(shape)

    n_pages = PAST_LEN // PAGE_SIZE
    n_used = DEC_BATCH * n_pages
    n_phys = (5 * n_used + 3) // 4
    page_table = jax.random.permutation(next(keys), n_phys)[:n_used].reshape(DEC_BATCH, n_pages).astype(jnp.int32)
    return {
        'x_prompt': nrm((BATCH, SEQ, D_MODEL)),
        'x_sample': nrm((DEC_BATCH, DEC_SEQ, D_MODEL)),
        'cache_mla_latent': nrm((n_phys, N_EVEN, PAGE_SIZE, MLA_KV_LORA)),
        'cache_mla_krope': nrm((n_phys, N_EVEN, PAGE_SIZE, MLA_ROPE)),
        'state_s5_re': nrm((N_EVEN, DEC_BATCH, S5_GROUPS, S5_STATE), 0.5),
        'state_s5_im': nrm((N_EVEN, DEC_BATCH, S5_GROUPS, S5_STATE), 0.5),
        'state_hgrn': nrm((N_ODD, DEC_BATCH, HGRN_HEADS, HGRN_DK, HGRN_DV), 0.5),
        'cache_mem_k': nrm((DEPTH, DEC_BATCH, MEM_LEN, MEM_HEADS, MEM_HEAD_DIM)),
        'cache_mem_v': nrm((DEPTH, DEC_BATCH, MEM_LEN, MEM_HEADS, MEM_HEAD_DIM)),
        'page_table': page_table,
        'mem_prompt': nrm((BATCH, MEM_LEN, D_MODEL)),
        'norm_mix': gain((DEPTH, D_MODEL)),
        'norm_mem': gain((DEPTH, D_MODEL)),
        'norm_memsrc': gain((DEPTH, D_MODEL)),
        'norm_mlp': gain((DEPTH, D_MODEL)),
        'w_mem_q': nrm((DEPTH, D_MODEL, MEM_WIDTH), D_MODEL ** -0.5),
        'w_mem_k': nrm((DEPTH, D_MODEL, MEM_WIDTH), D_MODEL ** -0.5),
        'w_mem_v': nrm((DEPTH, D_MODEL, MEM_WIDTH), D_MODEL ** -0.5),
        'w_mem_o': nrm((DEPTH, MEM_WIDTH, D_MODEL), MEM_WIDTH ** -0.5),
        'mem_q_gain': gain((DEPTH, MEM_HEAD_DIM)),
        'mem_k_gain': gain((DEPTH, MEM_HEAD_DIM)),
        'w_mlp_up': nrm((DEPTH, D_MODEL, D_FF), D_MODEL ** -0.5),
        'w_mlp_down': nrm((DEPTH, D_FF, D_MODEL), D_FF ** -0.5),
        'w_in_even': nrm((N_EVEN, D_MODEL, EVEN_IN), D_MODEL ** -0.5),
        'mla_cq_norm': gain((N_EVEN, MLA_Q_LORA)),
        'mla_ckv_norm': gain((N_EVEN, MLA_KV_LORA)),
        'w_mla_uq': nrm((N_EVEN, MLA_Q_LORA, MLA_HEADS * MLA_QK), MLA_Q_LORA ** -0.5),
        'w_mla_ukv': nrm((N_EVEN, MLA_KV_LORA, MLA_HEADS * (MLA_NOPE + MLA_V)), MLA_KV_LORA ** -0.5),
        'mla_qn_nope': gain((N_EVEN, MLA_NOPE)),
        'mla_qn_rope': gain((N_EVEN, MLA_ROPE // 2)),
        'mla_kn_nope': gain((N_EVEN, MLA_NOPE)),
        'mla_kn_rope': gain((N_EVEN, MLA_ROPE // 2)),
        's5_lambda_re': -0.5 + 0.01 * nrm((N_EVEN, S5_GROUPS, S5_STATE)),
        's5_lambda_im': math.pi * jnp.arange(S5_STATE, dtype=jnp.float32) + 0.01 * nrm((N_EVEN, S5_GROUPS, S5_STATE)),
        's5_log_step': jax.random.uniform(next(keys), (N_EVEN, S5_GROUPS), jnp.float32, math.log(1e-3), math.log(1e-1)),
        's5_b_re': nrm((N_EVEN, S5_GROUPS, S5_STATE, S5_GROUP), (2 * S5_GROUP) ** -0.5),
        's5_b_im': nrm((N_EVEN, S5_GROUPS, S5_STATE, S5_GROUP), (2 * S5_GROUP) ** -0.5),
        's5_c_re': nrm((N_EVEN, S5_GROUPS, S5_GROUP, S5_STATE), S5_STATE ** -0.5),
        's5_c_im': nrm((N_EVEN, S5_GROUPS, S5_GROUP, S5_STATE), S5_STATE ** -0.5),
        's5_d': nrm((N_EVEN, S5_WIDTH)),
        's5_w_glu': nrm((N_EVEN, S5_WIDTH, S5_WIDTH), S5_WIDTH ** -0.5),
        's5_b_glu': nrm((N_EVEN, S5_WIDTH), 0.01),
        'w_out_even': nrm((N_EVEN, EVEN_OUT, D_MODEL), EVEN_OUT ** -0.5),
        'w_in_odd': nrm((N_ODD, D_MODEL, 4 * HGRN_WIDTH), D_MODEL ** -0.5),
        'hgrn_lower_bounds': nrm((DEPTH, HGRN_WIDTH), 0.1),
        'hgrn_out_norm': gain((N_ODD, HGRN_WIDTH)),
        'w_out_odd': nrm((N_ODD, HGRN_WIDTH, D_MODEL), HGRN_WIDTH ** -0.5),
    }


def reference(x_prompt, x_sample, cache_mla_latent, cache_mla_krope, state_s5_re, state_s5_im, state_hgrn,
              cache_mem_k, cache_mem_v, page_table, mem_prompt,
              norm_mix, norm_mem, norm_memsrc, norm_mlp, w_mem_q, w_mem_k, w_mem_v, w_mem_o, mem_q_gain, mem_k_gain,
              w_mlp_up, w_mlp_down, w_in_even, mla_cq_norm, mla_ckv_norm, w_mla_uq, w_mla_ukv,
              mla_qn_nope, mla_qn_rope, mla_kn_nope, mla_kn_rope,
              s5_lambda_re, s5_lambda_im, s5_log_step, s5_b_re, s5_b_im, s5_c_re, s5_c_im, s5_d, s5_w_glu, s5_b_glu,
              w_out_even, w_in_odd, hgrn_lower_bounds, hgrn_out_norm, w_out_odd):
    pos_p = jnp.arange(SEQ, dtype=jnp.int32)
    pos_s = PAST_LEN + jnp.arange(DEC_SEQ, dtype=jnp.int32)
    lb_sm = jax.nn.softmax(hgrn_lower_bounds.astype(jnp.float32), axis=0)
    lb_all = jnp.cumsum(lb_sm, axis=0) - lb_sm[0]

    hp, hs = x_prompt, x_sample
    lat_p, kr_p, s5r_p, s5i_p, hg_p, mk_p, mv_p = [], [], [], [], [], [], []
    lat_s, kr_s, s5r_s, s5i_s, hg_s = [], [], [], [], []
    for l in range(DEPTH):
        if l % 2 == 0:
            e = l // 2
            pe = {
                'w_in': w_in_even[e], 'cq_norm': mla_cq_norm[e], 'ckv_norm': mla_ckv_norm[e],
                'w_uq': w_mla_uq[e],
                'q_gain': jnp.concatenate([mla_qn_nope[e], mla_qn_rope[e], mla_qn_rope[e]]),
                'lam_re': s5_lambda_re[e], 'lam_im': s5_lambda_im[e], 'log_step': s5_log_step[e],
                'b_re': s5_b_re[e], 'b_im': s5_b_im[e], 'c_re': s5_c_re[e], 'c_im': s5_c_im[e],
                'd': s5_d[e], 'w_glu': s5_w_glu[e], 'b_glu': s5_b_glu[e], 'w_out': w_out_even[e],
            }
            w_ukv_e = w_mla_ukv[e]
            k_gain_e = jnp.concatenate([mla_kn_nope[e], mla_kn_rope[e], mla_kn_rope[e]])

            def attend_prompt(q, ckv, kr, w=w_ukv_e, g=k_gain_e):
                k, v = mla_keys(ckv, kr, w, g)
                return causal_block_attention(q, k, v)

            def attend_sample(q, ckv, kr, w=w_ukv_e, g=k_gain_e, e=e):
                return paged_mla_attention(q, ckv, kr, cache_mla_latent, cache_mla_krope, page_table, e, w, g)

            z0 = jnp.zeros((BATCH, S5_GROUPS, S5_STATE), jnp.float32)
            out, ckv, kr, hr, hi = even_mixer(rms_norm(hp, norm_mix[l]), pos_p, z0, z0, attend_prompt, pe)
            hp = hp + out
            lat_p.append(ckv); kr_p.append(kr); s5r_p.append(hr); s5i_p.append(hi)
            out, ckv, kr, hr, hi = even_mixer(rms_norm(hs, norm_mix[l]), pos_s, state_s5_re[e], state_s5_im[e],
                                              attend_sample, pe)
            hs = hs + out
            lat_s.append(ckv); kr_s.append(kr); s5r_s.append(hr); s5i_s.append(hi)
        else:
            o = l // 2
            po = {'w_in': w_in_odd[o], 'out_norm': hgrn_out_norm[o], 'w_out': w_out_odd[o]}
            s_zero = jnp.zeros((BATCH, HGRN_HEADS, HGRN_DK, HGRN_DV), jnp.float32)
            out, S = odd_mixer(rms_norm(hp, norm_mix[l]), s_zero, lb_all[l], po)
            hp = hp + out
            hg_p.append(S)
            out, S = odd_mixer(rms_norm(hs, norm_mix[l]), state_hgrn[o], lb_all[l], po)
            hs = hs + out
            hg_s.append(S)
        mk, mv = mem_kv(mem_prompt, norm_memsrc[l], w_mem_k[l], w_mem_v[l], mem_k_gain[l])
        mk_p.append(mk); mv_p.append(mv)
        hp = hp + mem_attend(rms_norm(hp, norm_mem[l]), mk, mv, w_mem_q[l], mem_q_gain[l], w_mem_o[l])
        hs = hs + mem_attend(rms_norm(hs, norm_mem[l]), cache_mem_k[l], cache_mem_v[l],
                             w_mem_q[l], mem_q_gain[l], w_mem_o[l])
        hp = hp + sq_relu_mlp(rms_norm(hp, norm_mlp[l]), w_mlp_up[l], w_mlp_down[l])
        hs = hs + sq_relu_mlp(rms_norm(hs, norm_mlp[l]), w_mlp_up[l], w_mlp_down[l])

    return (hp, hs,
            jnp.stack(lat_p, axis=1), jnp.stack(kr_p, axis=1),
            jnp.stack(s5r_p), jnp.stack(s5i_p), jnp.stack(hg_p),
            jnp.stack(mk_p), jnp.stack(mv_p),
            jnp.stack(lat_s, axis=1), jnp.stack(kr_s, axis=1),
            jnp.stack(s5r_s), jnp.stack(s5i_s), jnp.stack(hg_s))
```

```python
import functools
import math

import jax
import jax.numpy as jnp
from jax import lax
from jax.experimental import pallas as pl
from jax.experimental.pallas import tpu as pltpu

F32 = jnp.float32
BF16 = jnp.bfloat16

LANES = 128
VMEM_LIMIT_BYTES = 56 * 1024 * 1024

D_MODEL = 1024
MLA_HEADS = 8
MLA_NOPE = 64
MLA_ROPE = 32
MLA_QK = MLA_NOPE + MLA_ROPE
MLA_V = 64
MLA_Q_LORA = 768
MLA_KV_LORA = 256
ROPE_THETA = 10000.0
PAGE_SIZE = 128
S5_WIDTH = 512
S5_GROUP = 16
S5_GROUPS = S5_WIDTH // S5_GROUP
S5_STATE = 64
S5_NSTATE = S5_GROUPS * S5_STATE
HGRN_HEADS = 8
HGRN_DK = 128
HGRN_SUB = 16
HGRN_EXP_CLAMP = 80.0
MEM_HEADS = 4
MEM_HEAD_DIM = 128
MEM_WIDTH = MEM_HEADS * MEM_HEAD_DIM
D_FF = 4 * D_MODEL
EPS = 1e-6
HP = MLA_HEADS * LANES


def _cparams(sem):
    return pltpu.CompilerParams(dimension_semantics=sem, vmem_limit_bytes=VMEM_LIMIT_BYTES)


def _rms(x, g):
    return x * lax.rsqrt(jnp.mean(x * x, axis=-1, keepdims=True) + EPS) * g


def _sigmoid(x):
    return 1.0 / (1.0 + jnp.exp(-x))


def _dot(a, b):
    return jnp.dot(a, b, preferred_element_type=F32)


def _dot_nt(a, b):
    return lax.dot_general(a, b, (((1,), (1,)), ((), ())), preferred_element_type=F32)


def _dot_tn(a, b):
    return lax.dot_general(a, b, (((0,), (0,)), ((), ())), preferred_element_type=F32)


def _row_to_col(row, n):
    r = lax.broadcasted_iota(jnp.int32, (n, n), 0)
    c = lax.broadcasted_iota(jnp.int32, (n, n), 1)
    return jnp.sum(jnp.where(r == c, jnp.broadcast_to(row, (n, n)), 0.0), axis=1, keepdims=True)


def _div_pow2(x, d):
    return lax.shift_right_logical(x, int(math.log2(d)))


def _full_spec(shape):
    nd = len(shape)
    return pl.BlockSpec(shape, lambda *_: (0,) * nd)


def _even_proj_kernel(h_ref, g_ref, wp_ref, cqn_ref, wuq_ref, qg_ref, ckvn_ref, wkv_ref, kg_ref,
                      cq_ref, s1q_ref, s2q_ref, ck_ref, s1k_ref, s2k_ref,
                      q_out, k_out, v_out, ckv_out, kr_out, u_out, *maybe_qk_out):
    x = h_ref[...]
    hn = _rms(x, g_ref[...]).astype(BF16)
    proj = _dot(hn, wp_ref[...])
    o1 = MLA_Q_LORA
    o2 = o1 + MLA_KV_LORA
    o3 = o2 + LANES
    u_out[...] = proj[:, o3:]
    cq = _rms(proj[:, :o1], cqn_ref[...]).astype(BF16)
    qf = _dot(cq, wuq_ref[...])
    ckv = _rms(proj[:, o1:o2], ckvn_ref[...])
    ckv_out[...] = ckv
    kv = _dot(ckv.astype(BF16), wkv_ref[...])
    kr = proj[:, o2:o3]
    half = MLA_ROPE // 2
    krr = (kr * ck_ref[...] + pltpu.roll(kr, half, 1) * s1k_ref[...]
           + pltpu.roll(kr, LANES - half, 1) * s2k_ref[...])
    kr_out[...] = krr[:, :MLA_ROPE]
    kr_sh = pltpu.roll(krr, MLA_NOPE, 1)
    qg = qg_ref[...]
    kg = kg_ref[...]
    cq_t, s1q_t, s2q_t = cq_ref[...], s1q_ref[...], s2q_ref[...]
    inv_qk = 1.0 / MLA_QK
    for h in range(MLA_HEADS):
        sl = slice(LANES * h, LANES * (h + 1))
        qh = qf[:, sl]
        qh = qh * lax.rsqrt(jnp.sum(qh * qh, axis=-1, keepdims=True) * inv_qk + EPS) * qg
        qh = (qh * cq_t + pltpu.roll(qh, half, 1) * s1q_t + pltpu.roll(qh, LANES - half, 1) * s2q_t)
        q_out[:, sl] = qh.astype(BF16)
        if maybe_qk_out:
            maybe_qk_out[0][:, sl] = (qh * kg).astype(BF16)
        kh = kv[:, sl] + kr_sh
        kh = kh * lax.rsqrt(jnp.sum(kh * kh, axis=-1, keepdims=True) * inv_qk + EPS) * kg
        k_out[:, sl] = kh.astype(BF16)
    v_out[...] = kv[:, HP:].astype(BF16)


def _even_proj(h, g, wp, cqn, wuq, qg, ckvn, wkv, kg, tabs, *, tm, u_shape, u_index, emit_qk):
    m = h.shape[0]
    ltab = tabs[0].shape[0]
    ntab = ltab // tm
    row = lambda i: (i, 0)
    tab_spec = pl.BlockSpec((tm, LANES), lambda i: (i % ntab, 0))
    in_specs = [pl.BlockSpec((tm, D_MODEL), row), _full_spec(g.shape), _full_spec(wp.shape),
                _full_spec(cqn.shape), _full_spec(wuq.shape), _full_spec(qg.shape),
                _full_spec(ckvn.shape), _full_spec(wkv.shape), _full_spec(kg.shape)] + [tab_spec] * 6
    out_shape = [jax.ShapeDtypeStruct((m, HP), BF16), jax.ShapeDtypeStruct((m, HP), BF16),
                 jax.ShapeDtypeStruct((m, HP), BF16), jax.ShapeDtypeStruct((m, MLA_KV_LORA), F32),
                 jax.ShapeDtypeStruct((m, MLA_ROPE), F32), jax.ShapeDtypeStruct(u_shape, F32)]
    out_specs = [pl.BlockSpec((tm, HP), row), pl.BlockSpec((tm, HP), row), pl.BlockSpec((tm, HP), row),
                 pl.BlockSpec((tm, MLA_KV_LORA), row), pl.BlockSpec((tm, MLA_ROPE), row),
                 pl.BlockSpec((tm, S5_WIDTH), u_index)]
    if emit_qk:
        out_shape.append(jax.ShapeDtypeStruct((m, HP), BF16))
        out_specs.append(pl.BlockSpec((tm, HP), row))
    return pl.pallas_call(
        _even_proj_kernel, grid=(m // tm,), in_specs=in_specs, out_specs=out_specs, out_shape=out_shape,
        compiler_params=_cparams(("arbitrary",)), name="even_proj",
    )(h, g, wp, cqn, wuq, qg, ckvn, wkv, kg, *tabs)


def _flash_kernel(q_ref, k_ref, v_ref, o_ref, *, tq, tk):
    i = pl.program_id(1)
    nfull = (i * tq) // tk
    row = lax.broadcasted_iota(jnp.int32, (tq, tk), 0) + i * tq
    col = lax.broadcasted_iota(jnp.int32, (tq, tk), 1) + nfull * tk
    causal = row >= col

    for h in range(MLA_HEADS):
        sl = slice(LANES * h, LANES * (h + 1))
        q = q_ref[:, sl]

        def update(j, carry, masked):
            m, l, acc = carry
            off = pl.multiple_of(j * tk, tk)
            kb = k_ref[pl.ds(off, tk), sl]
            vb = v_ref[pl.ds(off, tk), sl]
            s = _dot_nt(q, kb)
            if masked:
                s = jnp.where(causal, s, -jnp.inf)
            m_new = jnp.maximum(m, jnp.max(s, axis=-1, keepdims=True))
            alpha = jnp.exp(m - m_new)
            p = jnp.exp(s - m_new)
            l = alpha * l + jnp.sum(p, axis=-1, keepdims=True)
            acc = alpha * acc + _dot(p.astype(BF16), vb)
            return m_new, l, acc

        carry = (jnp.full((tq, 1), -jnp.inf, F32), jnp.zeros((tq, 1), F32), jnp.zeros((tq, LANES), F32))
        carry = lax.fori_loop(0, nfull, lambda j, c: update(j, c, False), carry)
        _, l, acc = update(nfull, carry, True)
        o_ref[:, sl] = (acc / l).astype(BF16)


def _flash_attention(q, k, v, *, tq, tk):
    b, l, _ = q.shape
    return pl.pallas_call(
        functools.partial(_flash_kernel, tq=tq, tk=tk),
        grid=(b, l // tq),
        in_specs=[pl.BlockSpec((None, tq, HP), lambda bi, i: (bi, i, 0)),
                  pl.BlockSpec((None, l, HP), lambda bi, i: (bi, 0, 0)),
                  pl.BlockSpec((None, l, HP), lambda bi, i: (bi, 0, 0))],
        out_specs=pl.BlockSpec((None, tq, HP), lambda bi, i: (bi, i, 0)),
        out_shape=jax.ShapeDtypeStruct((b, l, HP), BF16),
        compiler_params=_cparams(("arbitrary", "arbitrary")), name="prompt_attention",
    )(q, k, v)


def _paged_kernel(pt_ref, *refs, npg, nsteps, nq):
    lat_refs = refs[:npg]
    kr_refs = refs[npg:2 * npg]
    (qn_ref, qr_ref, wuk_ref, e2_ref, onr_ref, cnew_ref, krnew_ref, wuv_ref,
     o_ref, cbuf, krbuf, m_scr, l_scr, a_scr) = refs[2 * npg:]
    del pt_ref
    s = pl.program_id(1)
    nslots = nsteps + 1
    ncol = LANES
    inv_qk = 1.0 / MLA_QK

    def stats(c_bf, kr_bf, mask):
        kn = _dot(c_bf, wuk_ref[...])
        ss = _dot((kn * kn).astype(BF16), e2_ref[...])
        krf = kr_bf.astype(F32)
        ss = ss + _dot((krf * krf).astype(BF16), onr_ref[...])
        r = lax.rsqrt(ss * inv_qk + EPS)
        sc = (_dot(kn.astype(BF16), qn_ref[...]) + _dot(kr_bf, qr_ref[...])) * r
        if mask is not None:
            sc = jnp.where(mask, sc, -jnp.inf)
        m = jnp.max(sc, axis=0, keepdims=True)
        p = jnp.exp(sc - m)
        l = jnp.sum(p, axis=0, keepdims=True)
        a = _dot_tn(p.astype(BF16), c_bf)
        return m, l, a

    krbuf[...] = jnp.zeros(krbuf.shape, BF16)
    for g in range(npg):
        rows = slice(PAGE_SIZE * g, PAGE_SIZE * (g + 1))
        cbuf[rows, :] = lat_refs[g][...].astype(BF16)
        krbuf[rows, :MLA_ROPE] = kr_refs[g][...].astype(BF16)
    m, l, a = stats(cbuf[...], krbuf[...], None)
    m_scr[pl.ds(s, 1), :] = m
    l_scr[pl.ds(s, 1), :] = l
    a_scr[s] = a

    @pl.when(s == nsteps - 1)
    def _():
        nnew = cnew_ref.shape[0]
        krn = jnp.concatenate([krnew_ref[...], jnp.zeros((nnew, LANES - MLA_ROPE), F32)], axis=1)
        key = lax.broadcasted_iota(jnp.int32, (nnew, ncol), 0)
        qry = _div_pow2(lax.broadcasted_iota(jnp.int32, (nnew, ncol), 1), MLA_HEADS)
        m2, l2, a2 = stats(cnew_ref[...].astype(BF16), krn.astype(BF16), key <= qry)
        m_scr[pl.ds(nsteps, 1), :] = m2
        l_scr[pl.ds(nsteps, 1), :] = l2
        a_scr[nsteps] = a2
        mall = m_scr[0:nslots, :]
        w = jnp.exp(mall - jnp.max(mall, axis=0, keepdims=True))
        den = jnp.sum(l_scr[0:nslots, :] * w, axis=0, keepdims=True)
        wn = w / den
        num = jnp.zeros((ncol, MLA_KV_LORA), F32)
        for t in range(nslots):
            num = num + a_scr[t] * _row_to_col(wn[t:t + 1, :], ncol)
        full = _dot(num.astype(BF16), wuv_ref[...])
        hrow = lax.broadcasted_iota(jnp.int32, (MLA_HEADS, MLA_HEADS * MLA_V), 0)
        hcol = _div_pow2(lax.broadcasted_iota(jnp.int32, (MLA_HEADS, MLA_HEADS * MLA_V), 1), MLA_V)
        rows = []
        for qi in range(nq):
            blk = full[MLA_HEADS * qi:MLA_HEADS * (qi + 1), :]
            rows.append(jnp.sum(jnp.where(hrow == hcol, blk, 0.0), axis=0, keepdims=True))
        o_ref[...] = jnp.concatenate(rows, axis=0)


def _paged_attention(page_table, cache_lat, cache_kr, e, qn, qr, wuk, e2, onr, cnew, krnew, wuv, *, npg, nq):
    nb, npages = page_table.shape
    nsteps = npages // npg
    t = npg * PAGE_SIZE
    nnew = cnew.shape[1]

    def page_spec(width, g):
        return pl.BlockSpec((None, None, PAGE_SIZE, width),
                            lambda b, s, pt: (pt[b, s * npg + g], e, 0, 0))

    per_b3 = lambda b, s, pt: (b, 0, 0)
    const2 = lambda b, s, pt: (0, 0)
    in_specs = ([page_spec(MLA_KV_LORA, g) for g in range(npg)]
                + [page_spec(MLA_ROPE, g) for g in range(npg)]
                + [pl.BlockSpec((None,) + qn.shape[1:], per_b3), pl.BlockSpec((None,) + qr.shape[1:], per_b3),
                   pl.BlockSpec(wuk.shape, const2), pl.BlockSpec(e2.shape, const2),
                   pl.BlockSpec(onr.shape, const2),
                   pl.BlockSpec((None, nnew, MLA_KV_LORA), per_b3), pl.BlockSpec((None, nnew, MLA_ROPE), per_b3),
                   pl.BlockSpec(wuv.shape, const2)])
    grid_spec = pltpu.PrefetchScalarGridSpec(
        num_scalar_prefetch=1, grid=(nb, nsteps), in_specs=in_specs,
        out_specs=pl.BlockSpec((None, nq, MLA_HEADS * MLA_V), per_b3),
        scratch_shapes=[pltpu.VMEM((t, MLA_KV_LORA), BF16), pltpu.VMEM((t, LANES), BF16),
                        pltpu.VMEM((16, LANES), F32), pltpu.VMEM((16, LANES), F32),
                        pltpu.VMEM((nsteps + 1, LANES, MLA_KV_LORA), F32)])
    return pl.pallas_call(
        functools.partial(_paged_kernel, npg=npg, nsteps=nsteps, nq=nq),
        grid_spec=grid_spec, out_shape=jax.ShapeDtypeStruct((nb, nq, MLA_HEADS * MLA_V), F32),
        compiler_params=_cparams(("arbitrary", "arbitrary")), name="paged_attention",
    )(page_table, *([cache_lat] * npg), *([cache_kr] * npg), qn, qr, wuk, e2, onr, cnew, krnew, wuv)


def _s5_kernel(u_ref, h0r_ref, h0i_ref, lamr_ref, lami_ref, lstep_ref, brm_ref, bim_ref, crm_ref, cim_ref,
               d_ref, wg_ref, bg_ref, o_ref, hr_out, hi_out, xr_scr, xi_scr, hcr, hci, disc, *, tt, nb, strip):
    c = pl.program_id(0)

    @pl.when(c == 0)
    def _():
        lr = jnp.minimum(lamr_ref[...], -1e-4)
        li = lami_ref[...]
        dt = jnp.exp(lstep_ref[...])
        mag = jnp.exp(lr * dt)
        abr = mag * jnp.cos(li * dt)
        abi = mag * jnp.sin(li * dt)
        den = lr * lr + li * li
        disc[0:1, :] = abr
        disc[1:2, :] = abi
        disc[2:3, :] = ((abr - 1.0) * lr + abi * li) / den
        disc[3:4, :] = (abi * lr - (abr - 1.0) * li) / den
        hcr[...] = h0r_ref[...]
        hci[...] = h0i_ref[...]

    u = u_ref[...]
    ub = u.astype(BF16)
    pr = _dot(ub, brm_ref[...])
    pi = _dot(ub, bim_ref[...])
    cor = disc[2:3, :]
    coi = disc[3:4, :]
    xr_scr[...] = cor * pr - coi * pi
    xi_scr[...] = cor * pi + coi * pr

    for s0 in range(0, S5_NSTATE, strip):
        lanes = slice(s0, s0 + strip)
        ar = jnp.broadcast_to(disc[0:1, lanes], (nb, strip))
        ai = jnp.broadcast_to(disc[1:2, lanes], (nb, strip))

        def step(t, carry, lanes=lanes, ar=ar, ai=ai):
            hr, hi = carry
            rows = pl.ds(pl.multiple_of(t * nb, nb), nb)
            nr = ar * hr - ai * hi + xr_scr[rows, lanes]
            ni = ar * hi + ai * hr + xi_scr[rows, lanes]
            xr_scr[rows, lanes] = nr
            xi_scr[rows, lanes] = ni
            return nr, ni

        hr, hi = lax.fori_loop(0, tt, step, (hcr[:, lanes], hci[:, lanes]))
        hcr[:, lanes] = hr
        hci[:, lanes] = hi

    y = _dot(xr_scr[...].astype(BF16), crm_ref[...]) - _dot(xi_scr[...].astype(BF16), cim_ref[...])
    y = y + d_ref[...] * u
    z = jax.nn.gelu(y)
    gate = _sigmoid(_dot(z.astype(BF16), wg_ref[...]) + bg_ref[...])
    o_ref[...] = (z * gate).astype(BF16)

    @pl.when(c == pl.num_programs(0) - 1)
    def _():
        hr_out[...] = hcr[...]
        hi_out[...] = hci[...]


def _s5(u, h0r, h0i, lamr, lami, lstep, brm, bim, crm, cim, d, wg, bg, *, tt, nb, strip):
    rows = u.shape[0]
    blk = tt * nb
    consts = (h0r, h0i, lamr, lami, lstep, brm, bim, crm, cim, d, wg, bg)
    return pl.pallas_call(
        functools.partial(_s5_kernel, tt=tt, nb=nb, strip=strip),
        grid=(rows // blk,),
        in_specs=[pl.BlockSpec((blk, S5_WIDTH), lambda c: (c, 0))] + [_full_spec(a.shape) for a in consts],
        out_specs=[pl.BlockSpec((blk, S5_WIDTH), lambda c: (c, 0)),
                   _full_spec((nb, S5_NSTATE)), _full_spec((nb, S5_NSTATE))],
        out_shape=[jax.ShapeDtypeStruct((rows, S5_WIDTH), BF16),
                   jax.ShapeDtypeStruct((nb, S5_NSTATE), F32), jax.ShapeDtypeStruct((nb, S5_NSTATE), F32)],
        scratch_shapes=[pltpu.VMEM((blk, S5_NSTATE), F32), pltpu.VMEM((blk, S5_NSTATE), F32),
                        pltpu.VMEM((nb, S5_NSTATE), F32), pltpu.VMEM((nb, S5_NSTATE), F32),
                        pltpu.VMEM((8, S5_NSTATE), F32)],
        compiler_params=_cparams(("arbitrary",)), name="s5",
    )(u, *consts)


def _mm_res_kernel(*refs, nop):
    res_ref = refs[2 * nop]
    o_ref = refs[2 * nop + 1]
    acc = res_ref[...]
    for t in range(nop):
        acc = acc + _dot(refs[t][...].astype(BF16), refs[nop + t][...])
    o_ref[...] = acc


def _mm_res(ops, ws, res, *, tm, op_index):
    m, n = res.shape
    row = lambda i: (i, 0)
    in_specs = ([pl.BlockSpec((tm, w.shape[0]), idx) for w, idx in zip(ws, op_index)]
                + [_full_spec(w.shape) for w in ws] + [pl.BlockSpec((tm, n), row)])
    return pl.pallas_call(
        functools.partial(_mm_res_kernel, nop=len(ops)), grid=(m // tm,), in_specs=in_specs,
        out_specs=pl.BlockSpec((tm, n), row), out_shape=jax.ShapeDtypeStruct((m, n), F32),
        compiler_params=_cparams(("arbitrary",)), name="matmul_residual",
    )(*ops, *ws, res)


def _mem_kv_kernel(x_ref, g_ref, w_ref, kg_ref, k_out, v_out):
    mn = _rms(x_ref[...], g_ref[...]).astype(BF16)
    kv = _dot(mn, w_ref[...])
    kg = kg_ref[...]
    for h in range(MEM_HEADS):
        sl = slice(LANES * h, LANES * (h + 1))
        k_out[:, sl] = _rms(kv[:, sl], kg)
    v_out[...] = kv[:, MEM_WIDTH:]


def _mem_kv(x, g, w, kg, *, tm):
    m = x.shape[0]
    row = lambda i: (i, 0)
    return pl.pallas_call(
        _mem_kv_kernel, grid=(m // tm,),
        in_specs=[pl.BlockSpec((tm, D_MODEL), row), _full_spec(g.shape), _full_spec(w.shape), _full_spec(kg.shape)],
        out_specs=[pl.BlockSpec((tm, MEM_WIDTH), row), pl.BlockSpec((tm, MEM_WIDTH), row)],
        out_shape=[jax.ShapeDtypeStruct((m, MEM_WIDTH), F32), jax.ShapeDtypeStruct((m, MEM_WIDTH), F32)],
        compiler_params=_cparams(("arbitrary",)), name="mem_kv",
    )(x, g, w, kg)


def _mem_attn_kernel(h_ref, g_ref, wq_ref, qg_ref, mk_ref, mv_ref, wo_ref, o_ref):
    x = h_ref[...]
    hn = _rms(x, g_ref[...]).astype(BF16)
    q = _dot(hn, wq_ref[...])
    mk = mk_ref[...].astype(BF16)
    mv = mv_ref[...].astype(BF16)
    qg = qg_ref[...] * (MEM_HEAD_DIM ** -0.5)
    outs = []
    for h in range(MEM_HEADS):
        sl = slice(LANES * h, LANES * (h + 1))
        qh = _rms(q[:, sl], qg).astype(BF16)
        s = _dot_nt(qh, mk[:, sl])
        p = jnp.exp(s - jnp.max(s, axis=-1, keepdims=True))
        oh = _dot(p.astype(BF16), mv[:, sl]) / jnp.sum(p, axis=-1, keepdims=True)
        outs.append(oh)
    o = jnp.concatenate(outs, axis=1).astype(BF16)
    o_ref[...] = x + _dot(o, wo_ref[...])


def _mem_attn(h, g, wq, qg, mk, mv, wo, *, tl):
    b, l, _ = h.shape
    ml = mk.shape[1]
    blk = lambda bi, i: (bi, i, 0)
    per_b = lambda bi, i: (bi, 0, 0)
    return pl.pallas_call(
        _mem_attn_kernel, grid=(b, l // tl),
        in_specs=[pl.BlockSpec((None, tl, D_MODEL), blk), _full_spec(g.shape), _full_spec(wq.shape),
                  _full_spec(qg.shape), pl.BlockSpec((None, ml, MEM_WIDTH), per_b),
                  pl.BlockSpec((None, ml, MEM_WIDTH), per_b), _full_spec(wo.shape)],
        out_specs=pl.BlockSpec((None, tl, D_MODEL), blk),
        out_shape=jax.ShapeDtypeStruct(h.shape, F32),
        compiler_params=_cparams(("arbitrary", "arbitrary")), name="mem_attention",
    )(h, g, wq, qg, mk, mv, wo)


def _mlp_kernel(h_ref, g_ref, wu_ref, wd_ref, o_ref, xn_scr, acc_scr):
    j = pl.program_id(1)

    @pl.when(j == 0)
    def _():
        xn_scr[...] = _rms(h_ref[...], g_ref[...]).astype(BF16)
        acc_scr[...] = jnp.zeros(acc_scr.shape, F32)

    a = _dot(xn_scr[...], wu_ref[...])
    a = jnp.square(jnp.maximum(a, 0.0)).astype(BF16)
    acc_scr[...] += _dot(a, wd_ref[...])

    @pl.when(j == pl.num_programs(1) - 1)
    def _():
        o_ref[...] = h_ref[...] + acc_scr[...]


def _mlp(h, g, wu, wd, *, tm, tf):
    m = h.shape[0]
    return pl.pallas_call(
        _mlp_kernel, grid=(m // tm, D_FF // tf),
        in_specs=[pl.BlockSpec((tm, D_MODEL), lambda i, j: (i, 0)), _full_spec(g.shape),
                  pl.BlockSpec((D_MODEL, tf), lambda i, j: (0, j)), pl.BlockSpec((tf, D_MODEL), lambda i, j: (j, 0))],
        out_specs=pl.BlockSpec((tm, D_MODEL), lambda i, j: (i, 0)),
        out_shape=jax.ShapeDtypeStruct((m, D_MODEL), F32),
        scratch_shapes=[pltpu.VMEM((tm, D_MODEL), BF16), pltpu.VMEM((tm, D_MODEL), F32)],
        compiler_params=_cparams(("arbitrary", "arbitrary")), name="mlp",
    )(h, g, wu, wd)


def _norm_mm_kernel(h_ref, g_ref, w_ref, o_ref, xn_scr):
    @pl.when(pl.program_id(1) == 0)
    def _():
        xn_scr[...] = _rms(h_ref[...], g_ref[...]).astype(BF16)

    o_ref[...] = _dot(xn_scr[...], w_ref[...])


def _norm_mm(h, g, w, *, tm, tn):
    m = h.shape[0]
    n = w.shape[1]
    return pl.pallas_call(
        _norm_mm_kernel, grid=(m // tm, n // tn),
        in_specs=[pl.BlockSpec((tm, D_MODEL), lambda i, j: (i, 0)), _full_spec(g.shape),
                  pl.BlockSpec((D_MODEL, tn), lambda i, j: (0, j))],
        out_specs=pl.BlockSpec((tm, tn), lambda i, j: (i, j)),
        out_shape=jax.ShapeDtypeStruct((m, n), F32),
        scratch_shapes=[pltpu.VMEM((tm, D_MODEL), BF16)],
        compiler_params=_cparams(("arbitrary", "arbitrary")), name="norm_matmul",
    )(h, g, w)


def _hgrn_kernel(q_ref, f_ref, i_ref, g_ref, lbp_ref, on_ref, s0_ref, o_ref, s_out, s_scr, *, chunk, layer, l_valid):
    c = pl.program_id(1)

    @pl.when(c == 0)
    def _():
        s_scr[...] = s0_ref[...]

    lbp = lbp_ref[...]
    e = jnp.exp(lbp - jnp.max(lbp, axis=0, keepdims=True))
    sm = e / jnp.sum(e, axis=0, keepdims=True)
    lb = jnp.sum(sm[0:layer + 1, :], axis=0, keepdims=True) - sm[0:1, :]

    q = q_ref[...]
    qa = q * _sigmoid(q)
    fg = lb + (1.0 - lb) * _sigmoid(f_ref[...])
    logf = jnp.log(fg)
    kk = 1.0 - fg
    v = i_ref[...]
    if l_valid is not None:
        valid = (lax.broadcasted_iota(jnp.int32, (chunk, 1), 0) + c * chunk) < l_valid
        logf = jnp.where(valid, logf, 0.0)
        kk = jnp.where(valid, kk, 0.0)
    vb = v.astype(BF16)

    tr = lax.broadcasted_iota(jnp.int32, (chunk, chunk), 0)
    tc = lax.broadcasted_iota(jnp.int32, (chunk, chunk), 1)
    tri = jnp.where(tr >= tc, 1.0, 0.0).astype(BF16)
    hi = logf.astype(BF16)
    lo = (logf - hi.astype(F32)).astype(BF16)
    bcum = _dot(tri, hi) + _dot(tri, lo)
    blast = bcum[chunk - 1:chunk, :]
    qhat = (qa * jnp.exp(bcum)).astype(BF16)
    khat = (kk * jnp.exp(blast - bcum)).astype(BF16)
    dec = jnp.exp(blast)

    nsub = chunk // HGRN_SUB
    qloc, kloc = [], []
    for i in range(nsub):
        r0 = i * HGRN_SUB
        r1 = r0 + HGRN_SUB
        base = bcum[r0 - 1:r0, :] if i > 0 else jnp.zeros((1, bcum.shape[1]), F32)
        qloc.append((qa[r0:r1, :] * jnp.exp(bcum[r0:r1, :] - base)).astype(BF16))
        kloc.append((kk[:r1, :] * jnp.exp(jnp.minimum(base - bcum[:r1, :], HGRN_EXP_CLAMP))).astype(BF16))

    o_heads = []
    for h in range(HGRN_HEADS):
        sl = slice(HGRN_DK * h, HGRN_DK * (h + 1))
        st = s_scr[h]
        o_h = _dot(qhat[:, sl], st.astype(BF16))
        parts = []
        for i in range(nsub):
            r0 = i * HGRN_SUB
            r1 = r0 + HGRN_SUB
            att = _dot_nt(qloc[i][:, sl], kloc[i][:, sl])
            ar = lax.broadcasted_iota(jnp.int32, (HGRN_SUB, r1), 0) + r0
            ac = lax.broadcasted_iota(jnp.int32, (HGRN_SUB, r1), 1)
            att = jnp.where(ar >= ac, att, 0.0).astype(BF16)
            parts.append(_dot(att, vb[:r1, sl]))
        o_h = o_h + (parts[0] if nsub == 1 else jnp.concatenate(parts, axis=0))
        s_scr[h] = _row_to_col(dec[:, sl], HGRN_DK) * st + _dot_tn(khat[:, sl], vb[:, sl])
        o_heads.append(o_h)

    o = jnp.concatenate(o_heads, axis=1)
    g = g_ref[...]
    o_ref[...] = (_rms(o, on_ref[...]) * (g * _sigmoid(g))).astype(BF16)

    @pl.when(c == pl.num_programs(1) - 1)
    def _():
        s_out[...] = s_scr[...]


def _hgrn(proj, lbp, on, s0, *, chunk, layer, l_valid):
    b, l, _ = proj.shape
    w = D_MODEL

    def col(k):
        return pl.BlockSpec((None, chunk, w), lambda bi, c: (bi, c, k))

    st_spec = pl.BlockSpec((None, HGRN_HEADS, HGRN_DK, HGRN_DK), lambda bi, c: (bi, 0, 0, 0))
    return pl.pallas_call(
        functools.partial(_hgrn_kernel, chunk=chunk, layer=layer, l_valid=l_valid),
        grid=(b, l // chunk),
        in_specs=[col(0), col(1), col(2), col(3), _full_spec(lbp.shape), _full_spec(on.shape), st_spec],
        out_specs=[pl.BlockSpec((None, chunk, w), lambda bi, c: (bi, c, 0)), st_spec],
        out_shape=[jax.ShapeDtypeStruct((b, l, w), BF16), jax.ShapeDtypeStruct(s0.shape, F32)],
        scratch_shapes=[pltpu.VMEM((HGRN_HEADS, HGRN_DK, HGRN_DK), F32)],
        compiler_params=_cparams(("arbitrary", "arbitrary")), name="hgrn",
    )(proj, proj, proj, proj, lbp, on, s0)


def _pad_last(x, n):
    return jnp.pad(x, [(0, 0)] * (x.ndim - 1) + [(0, n - x.shape[-1])])


def _head_pad(w, per):
    k = w.shape[0]
    return _pad_last(w.reshape(k, -1, per), LANES).reshape(k, -1)


def _rope_tables(pos):
    half = MLA_ROPE // 2
    inv = ROPE_THETA ** (-jnp.arange(half, dtype=F32) / half)
    ang = pos.astype(F32)[:, None] * inv[None, :]
    cos, sin = jnp.cos(ang), jnp.sin(ang)
    n = pos.shape[0]
    z = lambda w: jnp.zeros((n, w), F32)
    scale = MLA_QK ** -0.5
    cq = scale * jnp.concatenate([jnp.ones((n, MLA_NOPE), F32), cos, cos, z(LANES - MLA_QK)], axis=1)
    s1q = scale * jnp.concatenate([z(MLA_NOPE + half), sin, z(LANES - MLA_QK)], axis=1)
    s2q = scale * jnp.concatenate([z(MLA_NOPE), -sin, z(half + LANES - MLA_QK)], axis=1)
    ck = jnp.concatenate([cos, cos, z(LANES - MLA_ROPE)], axis=1)
    s1k = jnp.concatenate([z(half), sin, z(LANES - MLA_ROPE)], axis=1)
    s2k = jnp.concatenate([-sin, z(LANES - half)], axis=1)
    return (cq, s1q, s2q, ck, s1k, s2k)


def _block_diag(x):
    g, a, b = x.shape
    eye = jnp.eye(g, dtype=x.dtype)
    return (x[:, :, None, :] * eye[:, None, :, None]).reshape(g * a, g * b)


def kernel(x_prompt, x_sample, cache_mla_latent, cache_mla_krope, state_s5_re, state_s5_im, state_hgrn, cache_mem_k, cache_mem_v, page_table, mem_prompt, norm_mix, norm_mem, norm_memsrc, norm_mlp, w_mem_q, w_mem_k, w_mem_v, w_mem_o, mem_q_gain, mem_k_gain, w_mlp_up, w_mlp_down, w_in_even, mla_cq_norm, mla_ckv_norm, w_mla_uq, w_mla_ukv, mla_qn_nope, mla_qn_rope, mla_kn_nope, mla_kn_rope, s5_lambda_re, s5_lambda_im, s5_log_step, s5_b_re, s5_b_im, s5_c_re, s5_c_im, s5_d, s5_w_glu, s5_b_glu, w_out_even, w_in_odd, hgrn_lower_bounds, hgrn_out_norm, w_out_odd):
    bsz, seq, _ = x_prompt.shape
    dbs, dseq, _ = x_sample.shape
    depth = norm_mix.shape[0]
    past_len = page_table.shape[1] * PAGE_SIZE
    ns = 8
    mem_len = mem_prompt.shape[1]
    row2 = lambda a: a.reshape(1, -1).astype(F32)

    hp = x_prompt.reshape(bsz * seq, D_MODEL)
    hs = jnp.pad(x_sample, ((0, 0), (0, ns - dseq), (0, 0))).reshape(dbs * ns, D_MODEL)

    tabs_p = _rope_tables(jnp.arange(seq, dtype=jnp.int32))
    pos_s = past_len + jnp.arange(ns, dtype=jnp.int32)
    tabs_s = tuple(jnp.tile(t, (dbs, 1)) for t in _rope_tables(pos_s))

    outs_p = {k: [] for k in ("lat", "kr", "s5r", "s5i", "hg", "mk", "mv")}
    outs_s = {k: [] for k in ("lat", "kr", "s5r", "s5i", "hg")}

    tm_p = 512
    nl_p = seq // tm_p
    tm_r = 1024
    nl_r = seq // tm_r

    for l in range(depth):
        if l % 2 == 0:
            e = l // 2
            w_in = w_in_even[e]
            o1 = MLA_Q_LORA + MLA_KV_LORA
            wp = jnp.concatenate([w_in[:, :o1], _pad_last(w_in[:, o1:o1 + MLA_ROPE], LANES),
                                  w_in[:, o1 + MLA_ROPE:]], axis=1).astype(BF16)
            wuq = _head_pad(w_mla_uq[e], MLA_QK).astype(BF16)
            ukv = w_mla_ukv[e].reshape(MLA_KV_LORA, MLA_HEADS, MLA_NOPE + MLA_V)
            wuk_c = ukv[:, :, :MLA_NOPE].reshape(MLA_KV_LORA, -1)
            wuv_c = ukv[:, :, MLA_NOPE:].reshape(MLA_KV_LORA, -1)
            wkv = jnp.concatenate([_head_pad(wuk_c, MLA_NOPE), _head_pad(wuv_c, MLA_V)], axis=1).astype(BF16)
            qg = _pad_last(jnp.concatenate([mla_qn_nope[e], mla_qn_rope[e], mla_qn_rope[e]])[None, :], LANES)
            kg = _pad_last(jnp.concatenate([mla_kn_nope[e], mla_kn_rope[e], mla_kn_rope[e]])[None, :], LANES)
            cqn = row2(mla_cq_norm[e])
            ckvn = row2(mla_ckv_norm[e])
            g_mix = row2(norm_mix[l])

            brm = _block_diag(jnp.swapaxes(s5_b_re[e], 1, 2)).astype(BF16)
            bim = _block_diag(jnp.swapaxes(s5_b_im[e], 1, 2)).astype(BF16)
            crm = _block_diag(jnp.swapaxes(s5_c_re[e], 1, 2)).astype(BF16)
            cim = _block_diag(jnp.swapaxes(s5_c_im[e], 1, 2)).astype(BF16)
            lamr = row2(s5_lambda_re[e])
            lami = row2(s5_lambda_im[e])
            lstep = row2(jnp.repeat(s5_log_step[e], S5_STATE))
            s5_consts = (lamr, lami, lstep, brm, bim, crm, cim, row2(s5_d[e]), s5_w_glu[e].astype(BF16),
                         row2(s5_b_glu[e]))
            w_out = w_out_even[e]
            wo_att_c = w_out[:MLA_HEADS * MLA_V].astype(BF16)
            wo_att_p = _pad_last(w_out[:MLA_HEADS * MLA_V].reshape(MLA_HEADS, MLA_V, D_MODEL).swapaxes(1, 2),
                                 LANES).swapaxes(1, 2).reshape(HP, D_MODEL).astype(BF16)
            wo_s5 = w_out[MLA_HEADS * MLA_V:].astype(BF16)

            q, k, v, ckv, kr, u = _even_proj(
                hp, g_mix, wp, cqn, wuq, qg, ckvn, wkv, kg, tabs_p, tm=tm_p,
                u_shape=(seq, bsz * S5_WIDTH), u_index=lambda i: (i % nl_p, i // nl_p), emit_qk=False)
            o_att = _flash_attention(q.reshape(bsz, seq, HP), k.reshape(bsz, seq, HP), v.reshape(bsz, seq, HP),
                                     tq=256, tk=512)
            z0 = jnp.zeros((bsz, S5_NSTATE), F32)
            o_s5, hr, hi = _s5(u.reshape(seq * bsz, S5_WIDTH), z0, z0, *s5_consts, tt=64, nb=bsz, strip=512)
            hp = _mm_res([o_att.reshape(bsz * seq, HP), o_s5.reshape(seq, bsz * S5_WIDTH)], [wo_att_p, wo_s5], hp,
                         tm=tm_r, op_index=[lambda i: (i, 0), lambda i: (i % nl_r, i // nl_r)])
            outs_p["lat"].append(ckv.reshape(bsz, seq, MLA_KV_LORA))
            outs_p["kr"].append(kr.reshape(bsz, seq, MLA_ROPE))
            outs_p["s5r"].append(hr.reshape(bsz, S5_GROUPS, S5_STATE))
            outs_p["s5i"].append(hi.reshape(bsz, S5_GROUPS, S5_STATE))

            m_s = dbs * ns
            q, k, v, ckv, kr, u, qk = _even_proj(
                hs, g_mix, wp, cqn, wuq, qg, ckvn, wkv, kg, tabs_s, tm=512,
                u_shape=(m_s, S5_WIDTH), u_index=lambda i: (i, 0), emit_qk=True)
            del q, k, v
            ckv3 = ckv.reshape(dbs, ns, MLA_KV_LORA)
            kr3 = kr.reshape(dbs, ns, MLA_ROPE)
            qk4 = qk.reshape(dbs, ns, MLA_HEADS, LANES)
            eye_h = jnp.eye(MLA_HEADS, dtype=BF16)
            qn = (jnp.transpose(qk4[..., :MLA_NOPE], (0, 2, 3, 1))[..., None]
                  * eye_h[None, :, None, None, :]).reshape(dbs, MLA_HEADS * MLA_NOPE, ns * MLA_HEADS)
            qn = _pad_last(qn, LANES)
            qr = jnp.transpose(qk4[..., MLA_NOPE:MLA_QK], (0, 3, 1, 2)).reshape(dbs, MLA_ROPE, ns * MLA_HEADS)
            qr = jnp.pad(qr, ((0, 0), (0, LANES - MLA_ROPE), (0, LANES - ns * MLA_HEADS)))
            ncols = ns * MLA_HEADS
            colmask = (jnp.arange(LANES) < ncols)
            e2 = ((jnp.arange(MLA_HEADS * MLA_NOPE)[:, None] // MLA_NOPE == (jnp.arange(LANES)[None, :] % MLA_HEADS))
                  & colmask[None, :]).astype(BF16)
            onr = ((jnp.arange(LANES)[:, None] < MLA_ROPE) & colmask[None, :]).astype(BF16)
            nnew = 16
            cnew = jnp.pad(ckv3, ((0, 0), (0, nnew - ns), (0, 0)))
            krnew = jnp.pad(kr3, ((0, 0), (0, nnew - ns), (0, 0)))
            o_att_s = _paged_attention(page_table, cache_mla_latent, cache_mla_krope, e, qn, qr,
                                       wuk_c.astype(BF16), e2, onr, cnew, krnew, wuv_c.astype(BF16),
                                       npg=16, nq=ns)
            u_tb = jnp.transpose(u.reshape(dbs, ns, S5_WIDTH)[:, :dseq], (1, 0, 2)).reshape(dseq * dbs, S5_WIDTH)
            o_s5, hr, hi = _s5(u_tb, state_s5_re[e].reshape(dbs, S5_NSTATE), state_s5_im[e].reshape(dbs, S5_NSTATE),
                               *s5_consts, tt=dseq, nb=dbs, strip=512)
            o_s5 = jnp.transpose(o_s5.reshape(dseq, dbs, S5_WIDTH), (1, 0, 2))
            o_s5 = jnp.pad(o_s5, ((0, 0), (0, ns - dseq), (0, 0))).reshape(m_s, S5_WIDTH)
            hs = _mm_res([o_att_s.reshape(m_s, MLA_HEADS * MLA_V), o_s5], [wo_att_c, wo_s5], hs,
                         tm=m_s, op_index=[lambda i: (i, 0), lambda i: (i, 0)])
            outs_s["lat"].append(ckv3[:, :dseq])
            outs_s["kr"].append(kr3[:, :dseq])
            outs_s["s5r"].append(hr.reshape(dbs, S5_GROUPS, S5_STATE))
            outs_s["s5i"].append(hi.reshape(dbs, S5_GROUPS, S5_STATE))
        else:
            o = l // 2
            g_mix = row2(norm_mix[l])
            w_in = w_in_odd[o].astype(BF16)
            w_out = w_out_odd[o].astype(BF16)
            on = row2(hgrn_out_norm[o])
            lbp = hgrn_lower_bounds.astype(F32)

            proj = _norm_mm(hp, g_mix, w_in, tm=1024, tn=1024)
            s_zero = jnp.zeros((bsz, HGRN_HEADS, HGRN_DK, HGRN_DK), F32)
            og, st = _hgrn(proj.reshape(bsz, seq, 4 * D_MODEL), lbp, on, s_zero, chunk=64, layer=l, l_valid=None)
            hp = _mm_res([og.reshape(bsz * seq, D_MODEL)], [w_out], hp, tm=tm_r, op_index=[lambda i: (i, 0)])
            outs_p["hg"].append(st)

            m_s = dbs * ns
            proj = _norm_mm(hs, g_mix, w_in, tm=m_s, tn=1024)
            lpad = HGRN_SUB
            proj = jnp.pad(proj.reshape(dbs, ns, 4 * D_MODEL), ((0, 0), (0, lpad - ns), (0, 0)))
            og, st = _hgrn(proj, lbp, on, state_hgrn[o], chunk=lpad, layer=l, l_valid=dseq)
            hs = _mm_res([og[:, :ns].reshape(m_s, D_MODEL)], [w_out], hs, tm=m_s, op_index=[lambda i: (i, 0)])
            outs_s["hg"].append(st)

        g_mem = row2(norm_mem[l])
        wq = w_mem_q[l].astype(BF16)
        wo = w_mem_o[l].astype(BF16)
        mqg = row2(mem_q_gain[l])
        wkv_m = jnp.concatenate([w_mem_k[l], w_mem_v[l]], axis=1).astype(BF16)
        mk, mv = _mem_kv(mem_prompt.reshape(bsz * mem_len, D_MODEL), row2(norm_memsrc[l]), wkv_m,
                         row2(mem_k_gain[l]), tm=512)
        mk = mk.reshape(bsz, mem_len, MEM_WIDTH)
        mv = mv.reshape(bsz, mem_len, MEM_WIDTH)
        outs_p["mk"].append(mk.reshape(bsz, mem_len, MEM_HEADS, MEM_HEAD_DIM))
        outs_p["mv"].append(mv.reshape(bsz, mem_len, MEM_HEADS, MEM_HEAD_DIM))
        hp = _mem_attn(hp.reshape(bsz, seq, D_MODEL), g_mem, wq, mqg, mk, mv, wo, tl=512).reshape(bsz * seq, D_MODEL)
        hs = _mem_attn(hs.reshape(dbs, ns, D_MODEL), g_mem, wq, mqg,
                       cache_mem_k[l].reshape(dbs, mem_len, MEM_WIDTH), cache_mem_v[l].reshape(dbs, mem_len, MEM_WIDTH),
                       wo, tl=ns).reshape(dbs * ns, D_MODEL)

        g_mlp = row2(norm_mlp[l])
        wu = w_mlp_up[l].astype(BF16)
        wd = w_mlp_down[l].astype(BF16)
        hp = _mlp(hp, g_mlp, wu, wd, tm=1024, tf=1024)
        hs = _mlp(hs, g_mlp, wu, wd, tm=dbs * ns, tf=1024)

    y_p = hp.reshape(bsz, seq, D_MODEL)
    y_s = hs.reshape(dbs, ns, D_MODEL)[:, :dseq]
    return (y_p, y_s,
            jnp.stack(outs_p["lat"], axis=1), jnp.stack(outs_p["kr"], axis=1),
            jnp.stack(outs_p["s5r"]), jnp.stack(outs_p["s5i"]), jnp.stack(outs_p["hg"]),
            jnp.stack(outs_p["mk"]), jnp.stack(outs_p["mv"]),
            jnp.stack(outs_s["lat"], axis=1), jnp.stack(outs_s["kr"], axis=1),
            jnp.stack(outs_s["s5r"]), jnp.stack(outs_s["s5i"]), jnp.stack(outs_s["hg"]))
```

```python
import functools
import math

import jax
import jax.numpy as jnp
from jax import lax
from jax.experimental import pallas as pl
from jax.experimental.pallas import tpu as pltpu

F32 = jnp.float32
BF16 = jnp.bfloat16

LANES = 128
VMEM_LIMIT_BYTES = 56 * 1024 * 1024

D_MODEL = 1024
MLA_HEADS = 8
MLA_NOPE = 64
MLA_ROPE = 32
MLA_QK = MLA_NOPE + MLA_ROPE
MLA_V = 64
MLA_Q_LORA = 768
MLA_KV_LORA = 256
ROPE_THETA = 10000.0
PAGE_SIZE = 128
S5_WIDTH = 512
S5_GROUP = 16
S5_GROUPS = S5_WIDTH // S5_GROUP
S5_STATE = 64
S5_NSTATE = S5_GROUPS * S5_STATE
HGRN_HEADS = 8
HGRN_DK = 128
HGRN_SUB = 16
HGRN_EXP_CLAMP = 80.0
MEM_HEADS = 4
MEM_HEAD_DIM = 128
MEM_WIDTH = MEM_HEADS * MEM_HEAD_DIM
D_FF = 4 * D_MODEL
EPS = 1e-6
HP = MLA_HEADS * LANES


def _cparams(sem):
    return pltpu.CompilerParams(dimension_semantics=sem, vmem_limit_bytes=VMEM_LIMIT_BYTES)


def _rms(x, g):
    return x * lax.rsqrt(jnp.mean(x * x, axis=-1, keepdims=True) + EPS) * g


def _sigmoid(x):
    return 1.0 / (1.0 + jnp.exp(-x))


def _dot(a, b):
    return jnp.dot(a, b, preferred_element_type=F32)


def _dot_nt(a, b):
    return lax.dot_general(a, b, (((1,), (1,)), ((), ())), preferred_element_type=F32)


def _dot_tn(a, b):
    return lax.dot_general(a, b, (((0,), (0,)), ((), ())), preferred_element_type=F32)


def _row_to_col(row, n):
    r = lax.broadcasted_iota(jnp.int32, (n, n), 0)
    c = lax.broadcasted_iota(jnp.int32, (n, n), 1)
    return jnp.sum(jnp.where(r == c, jnp.broadcast_to(row, (n, n)), 0.0), axis=1, keepdims=True)


def _div_pow2(x, d):
    return lax.shift_right_logical(x, int(math.log2(d)))


def _full_spec(shape):
    nd = len(shape)
    return pl.BlockSpec(shape, lambda *_: (0,) * nd)


def _even_proj_kernel(h_ref, g_ref, wp_ref, cqn_ref, wuq_ref, qg_ref, ckvn_ref, wkv_ref, kg_ref,
                      cq_ref, s1q_ref, s2q_ref, ck_ref, s1k_ref, s2k_ref,
                      q_out, k_out, v_out, ckv_out, kr_out, u_out, *maybe_qk_out):
    x = h_ref[...]
    hn = _rms(x, g_ref[...]).astype(BF16)
    proj = _dot(hn, wp_ref[...])
    o1 = MLA_Q_LORA
    o2 = o1 + MLA_KV_LORA
    o3 = o2 + LANES
    u_out[...] = proj[:, o3:]
    cq = _rms(proj[:, :o1], cqn_ref[...]).astype(BF16)
    qf = _dot(cq, wuq_ref[...])
    ckv = _rms(proj[:, o1:o2], ckvn_ref[...])
    ckv_out[...] = ckv
    kv = _dot(ckv.astype(BF16), wkv_ref[...])
    kr = proj[:, o2:o3]
    half = MLA_ROPE // 2
    krr = (kr * ck_ref[...] + pltpu.roll(kr, half, 1) * s1k_ref[...]
           + pltpu.roll(kr, LANES - half, 1) * s2k_ref[...])
    kr_out[...] = krr[:, :MLA_ROPE]
    kr_sh = pltpu.roll(krr, MLA_NOPE, 1)
    qg = qg_ref[...]
    kg = kg_ref[...]
    cq_t, s1q_t, s2q_t = cq_ref[...], s1q_ref[...], s2q_ref[...]
    inv_qk = 1.0 / MLA_QK
    for h in range(MLA_HEADS):
        sl = slice(LANES * h, LANES * (h + 1))
        qh = qf[:, sl]
        qh = qh * lax.rsqrt(jnp.sum(qh * qh, axis=-1, keepdims=True) * inv_qk + EPS) * qg
        qh = (qh * cq_t + pltpu.roll(qh, half, 1) * s1q_t + pltpu.roll(qh, LANES - half, 1) * s2q_t)
        q_out[:, sl] = qh.astype(BF16)
        if maybe_qk_out:
            maybe_qk_out[0][:, sl] = (qh * kg).astype(BF16)
        kh = kv[:, sl] + kr_sh
        kh = kh * lax.rsqrt(jnp.sum(kh * kh, axis=-1, keepdims=True) * inv_qk + EPS) * kg
        k_out[:, sl] = kh.astype(BF16)
    v_out[...] = kv[:, HP:].astype(BF16)


def _even_proj(h, g, wp, cqn, wuq, qg, ckvn, wkv, kg, tabs, *, tm, u_shape, u_index, emit_qk):
    m = h.shape[0]
    ltab = tabs[0].shape[0]
    ntab = ltab // tm
    row = lambda i: (i, 0)
    tab_spec = pl.BlockSpec((tm, LANES), lambda i: (i % ntab, 0))
    in_specs = [pl.BlockSpec((tm, D_MODEL), row), _full_spec(g.shape), _full_spec(wp.shape),
                _full_spec(cqn.shape), _full_spec(wuq.shape), _full_spec(qg.shape),
                _full_spec(ckvn.shape), _full_spec(wkv.shape), _full_spec(kg.shape)] + [tab_spec] * 6
    out_shape = [jax.ShapeDtypeStruct((m, HP), BF16), jax.ShapeDtypeStruct((m, HP), BF16),
                 jax.ShapeDtypeStruct((m, HP), BF16), jax.ShapeDtypeStruct((m, MLA_KV_LORA), F32),
                 jax.ShapeDtypeStruct((m, MLA_ROPE), F32), jax.ShapeDtypeStruct(u_shape, F32)]
    out_specs = [pl.BlockSpec((tm, HP), row), pl.BlockSpec((tm, HP), row), pl.BlockSpec((tm, HP), row),
                 pl.BlockSpec((tm, MLA_KV_LORA), row), pl.BlockSpec((tm, MLA_ROPE), row),
                 pl.BlockSpec((tm, S5_WIDTH), u_index)]
    if emit_qk:
        out_shape.append(jax.ShapeDtypeStruct((m, HP), BF16))
        out_specs.append(pl.BlockSpec((tm, HP), row))
    return pl.pallas_call(
        _even_proj_kernel, grid=(m // tm,), in_specs=in_specs, out_specs=out_specs, out_shape=out_shape,
        compiler_params=_cparams(("arbitrary",)), name="even_proj",
    )(h, g, wp, cqn, wuq, qg, ckvn, wkv, kg, *tabs)


def _flash_kernel(q_ref, k_ref, v_ref, o_ref, *, tq, tk, hg):
    i = pl.program_id(1)
    nfull = (i * tq) // tk
    row = lax.broadcasted_iota(jnp.int32, (tq, tk), 0) + i * tq
    col = lax.broadcasted_iota(jnp.int32, (tq, tk), 1) + nfull * tk
    causal = row >= col

    heads = [slice(LANES * h, LANES * (h + 1)) for h in range(MLA_HEADS)]

    def update(j, carry, masked):
        off = pl.multiple_of(j * tk, tk)
        out = []
        for h0 in range(0, MLA_HEADS, hg):
            grp = range(h0, h0 + hg)
            scores = [_dot_nt(q_ref[:, heads[h]], k_ref[pl.ds(off, tk), heads[h]]) for h in grp]
            probs, stats = [], []
            for h, s in zip(grp, scores):
                m, l = carry[3 * h], carry[3 * h + 1]
                if masked:
                    s = jnp.where(causal, s, -jnp.inf)
                m_new = jnp.maximum(m, jnp.max(s, axis=-1, keepdims=True))
                alpha = jnp.exp(m - m_new)
                p = jnp.exp(s - m_new)
                stats.append((m_new, alpha * l + jnp.sum(p, axis=-1, keepdims=True), alpha))
                probs.append(p.astype(BF16))
            pv = [_dot(p, v_ref[pl.ds(off, tk), heads[h]]) for p, h in zip(probs, grp)]
            for h, (m_new, l_new, alpha), o in zip(grp, stats, pv):
                out += [m_new, l_new, alpha * carry[3 * h + 2] + o]
        return tuple(out)

    init = (jnp.full((tq, 1), -jnp.inf, F32), jnp.zeros((tq, 1), F32), jnp.zeros((tq, LANES), F32))
    carry = lax.fori_loop(0, nfull, lambda j, c: update(j, c, False), init * MLA_HEADS)
    carry = update(nfull, carry, True)
    for h, sl in enumerate(heads):
        o_ref[:, sl] = (carry[3 * h + 2] / carry[3 * h + 1]).astype(BF16)


def _flash_attention(q, k, v, *, tq, tk, hg=2):
    b, l, _ = q.shape
    return pl.pallas_call(
        functools.partial(_flash_kernel, tq=tq, tk=tk, hg=hg),
        grid=(b, l // tq),
        in_specs=[pl.BlockSpec((None, tq, HP), lambda bi, i: (bi, i, 0)),
                  pl.BlockSpec((None, l, HP), lambda bi, i: (bi, 0, 0)),
                  pl.BlockSpec((None, l, HP), lambda bi, i: (bi, 0, 0))],
        out_specs=pl.BlockSpec((None, tq, HP), lambda bi, i: (bi, i, 0)),
        out_shape=jax.ShapeDtypeStruct((b, l, HP), BF16),
        compiler_params=_cparams(("arbitrary", "arbitrary")), name="prompt_attention",
    )(q, k, v)


def _paged_kernel(pt_ref, *refs, npg, ngrp, nsteps, nq):
    lat_refs = refs[:npg]
    krt_refs = refs[npg:2 * npg]
    (qn_ref, rhs2_ref, wuk_ref, cnew_ref, krnew_ref, wuv_ref,
     o_ref, wabs, m_scr, l_scr, a_scr) = refs[2 * npg:]
    del pt_ref
    s = pl.program_id(1)
    nslots = nsteps * ngrp + 1
    ncol = LANES
    inv_qk = 1.0 / MLA_QK

    @pl.when(s == 0)
    def _():
        wabs[...] = _dot(wuk_ref[...], qn_ref[...]).astype(BF16)

    def stats(blocks, mask):
        kn = [_dot(c, wuk_ref[...]) for c, _ in blocks]
        sq = [k * k for k in kn]
        psum = [q[:, 0:LANES] + q[:, LANES:2 * LANES] + q[:, 2 * LANES:3 * LANES] + q[:, 3 * LANES:] for q in sq]
        r2 = [_dot(jnp.concatenate([p.astype(BF16), x], axis=1), rhs2_ref[...]) for p, (_, x) in zip(psum, blocks)]
        scn = [_dot(c, wabs[...]) for c, _ in blocks]
        probs, out = [], []
        for t in range(len(blocks)):
            sc = (scn[t] + r2[t][:, LANES:]) * lax.rsqrt(r2[t][:, :LANES] * inv_qk + EPS)
            if mask is not None:
                sc = jnp.where(mask, sc, -jnp.inf)
            m = jnp.max(sc, axis=0, keepdims=True)
            p = jnp.exp(sc - m)
            out.append((m, jnp.sum(p, axis=0, keepdims=True)))
            probs.append(p.astype(BF16))
        acc = [_dot_tn(p, c) for p, (c, _) in zip(probs, blocks)]
        return [(m, l, a) for (m, l), a in zip(out, acc)]

    zpad = jnp.zeros((LANES - 2 * MLA_ROPE, PAGE_SIZE), F32)

    def rope_block(g):
        krt = krt_refs[g][...]
        return jnp.concatenate([krt, krt * krt, zpad], axis=0).T.astype(BF16)

    pg = npg // ngrp
    groups = [(jnp.concatenate([lat_refs[g][...].astype(BF16) for g in range(pg * t, pg * (t + 1))], axis=0),
               jnp.concatenate([rope_block(g) for g in range(pg * t, pg * (t + 1))], axis=0))
              for t in range(ngrp)]
    for t, (m, l, a) in enumerate(stats(groups, None)):
        slot = s * ngrp + t
        m_scr[pl.ds(slot, 1), :] = m
        l_scr[pl.ds(slot, 1), :] = l
        a_scr[slot] = a

    @pl.when(s == nsteps - 1)
    def _():
        nnew = cnew_ref.shape[0]
        krn = jnp.concatenate([krnew_ref[...], jnp.zeros((nnew, LANES - MLA_ROPE), F32)], axis=1)
        krn = krn + pltpu.roll(krn * krn, MLA_ROPE, 1)
        key = lax.broadcasted_iota(jnp.int32, (nnew, ncol), 0)
        qry = _div_pow2(lax.broadcasted_iota(jnp.int32, (nnew, ncol), 1), MLA_HEADS)
        (m2, l2, a2), = stats([(cnew_ref[...].astype(BF16), krn.astype(BF16))], key <= qry)
        m_scr[nslots - 1:nslots, :] = m2
        l_scr[nslots - 1:nslots, :] = l2
        a_scr[nslots - 1] = a2
        mall = m_scr[0:nslots, :]
        w = jnp.exp(mall - jnp.max(mall, axis=0, keepdims=True))
        den = jnp.sum(l_scr[0:nslots, :] * w, axis=0, keepdims=True)
        wn = w / den
        num = jnp.zeros((ncol, MLA_KV_LORA), F32)
        for t in range(nslots):
            num = num + a_scr[t] * _row_to_col(wn[t:t + 1, :], ncol)
        full = _dot(num.astype(BF16), wuv_ref[...])
        hrow = lax.broadcasted_iota(jnp.int32, (MLA_HEADS, MLA_HEADS * MLA_V), 0)
        hcol = _div_pow2(lax.broadcasted_iota(jnp.int32, (MLA_HEADS, MLA_HEADS * MLA_V), 1), MLA_V)
        rows = []
        for qi in range(nq):
            blk = full[MLA_HEADS * qi:MLA_HEADS * (qi + 1), :]
            rows.append(jnp.sum(jnp.where(hrow == hcol, blk, 0.0), axis=0, keepdims=True))
        o_ref[...] = jnp.concatenate(rows, axis=0)


def _paged_attention(page_table, cache_lat, cache_krt, e, qn, rhs2, wuk, cnew, krnew, wuv, *, npg, ngrp, nq):
    nb, npages = page_table.shape
    nsteps = npages // npg
    t = npg * PAGE_SIZE
    nnew = cnew.shape[1]
    nslots = nsteps * ngrp + 1

    def page_spec(shape, g):
        return pl.BlockSpec((None, None) + shape, lambda b, s, pt: (pt[b, s * npg + g], e, 0, 0))

    per_b3 = lambda b, s, pt: (b, 0, 0)
    const2 = lambda b, s, pt: (0, 0)
    in_specs = ([page_spec((PAGE_SIZE, MLA_KV_LORA), g) for g in range(npg)]
                + [page_spec((MLA_ROPE, PAGE_SIZE), g) for g in range(npg)]
                + [pl.BlockSpec((None,) + qn.shape[1:], per_b3), pl.BlockSpec((None,) + rhs2.shape[1:], per_b3),
                   pl.BlockSpec(wuk.shape, const2),
                   pl.BlockSpec((None, nnew, MLA_KV_LORA), per_b3), pl.BlockSpec((None, nnew, MLA_ROPE), per_b3),
                   pl.BlockSpec(wuv.shape, const2)])
    grid_spec = pltpu.PrefetchScalarGridSpec(
        num_scalar_prefetch=1, grid=(nb, nsteps), in_specs=in_specs,
        out_specs=pl.BlockSpec((None, nq, MLA_HEADS * MLA_V), per_b3),
        scratch_shapes=[pltpu.VMEM((MLA_KV_LORA, LANES), BF16),
                        pltpu.VMEM((nslots, LANES), F32), pltpu.VMEM((nslots, LANES), F32),
                        pltpu.VMEM((nslots, LANES, MLA_KV_LORA), F32)])
    return pl.pallas_call(
        functools.partial(_paged_kernel, npg=npg, ngrp=ngrp, nsteps=nsteps, nq=nq),
        grid_spec=grid_spec, out_shape=jax.ShapeDtypeStruct((nb, nq, MLA_HEADS * MLA_V), F32),
        compiler_params=_cparams(("arbitrary", "arbitrary")), name="paged_attention",
    )(page_table, *([cache_lat] * npg), *([cache_krt] * npg), qn, rhs2, wuk, cnew, krnew, wuv)


def _s5_kernel(u_ref, h0r_ref, h0i_ref, lamr_ref, lami_ref, lstep_ref, brm_ref, bim_ref, crm_ref, cim_ref,
               d_ref, wg_ref, bg_ref, o_ref, hr_out, hi_out, xr_scr, xi_scr, hcr, hci, disc, *, tt, nb, strip):
    c = pl.program_id(0)

    @pl.when(c == 0)
    def _():
        lr = jnp.minimum(lamr_ref[...], -1e-4)
        li = lami_ref[...]
        dt = jnp.exp(lstep_ref[...])
        mag = jnp.exp(lr * dt)
        abr = mag * jnp.cos(li * dt)
        abi = mag * jnp.sin(li * dt)
        den = lr * lr + li * li
        disc[0:1, :] = abr
        disc[1:2, :] = abi
        disc[2:3, :] = ((abr - 1.0) * lr + abi * li) / den
        disc[3:4, :] = (abi * lr - (abr - 1.0) * li) / den
        hcr[...] = h0r_ref[...]
        hci[...] = h0i_ref[...]

    u = u_ref[...]
    ub = u.astype(BF16)
    pr = _dot(ub, brm_ref[...])
    pi = _dot(ub, bim_ref[...])
    cor = disc[2:3, :]
    coi = disc[3:4, :]
    xr_scr[...] = cor * pr - coi * pi
    xi_scr[...] = cor * pi + coi * pr

    for s0 in range(0, S5_NSTATE, strip):
        lanes = slice(s0, s0 + strip)
        ar = jnp.broadcast_to(disc[0:1, lanes], (nb, strip))
        ai = jnp.broadcast_to(disc[1:2, lanes], (nb, strip))

        def step(t, carry, lanes=lanes, ar=ar, ai=ai):
            hr, hi = carry
            rows = pl.ds(pl.multiple_of(t * nb, nb), nb)
            nr = ar * hr - ai * hi + xr_scr[rows, lanes]
            ni = ar * hi + ai * hr + xi_scr[rows, lanes]
            xr_scr[rows, lanes] = nr
            xi_scr[rows, lanes] = ni
            return nr, ni

        hr, hi = lax.fori_loop(0, tt, step, (hcr[:, lanes], hci[:, lanes]))
        hcr[:, lanes] = hr
        hci[:, lanes] = hi

    y = _dot(xr_scr[...].astype(BF16), crm_ref[...]) - _dot(xi_scr[...].astype(BF16), cim_ref[...])
    y = y + d_ref[...] * u
    z = jax.nn.gelu(y)
    gate = _sigmoid(_dot(z.astype(BF16), wg_ref[...]) + bg_ref[...])
    o_ref[...] = (z * gate).astype(BF16)

    @pl.when(c == pl.num_programs(0) - 1)
    def _():
        hr_out[...] = hcr[...]
        hi_out[...] = hci[...]


def _s5(u, h0r, h0i, lamr, lami, lstep, brm, bim, crm, cim, d, wg, bg, *, tt, nb, strip):
    rows = u.shape[0]
    blk = tt * nb
    consts = (h0r, h0i, lamr, lami, lstep, brm, bim, crm, cim, d, wg, bg)
    return pl.pallas_call(
        functools.partial(_s5_kernel, tt=tt, nb=nb, strip=strip),
        grid=(rows // blk,),
        in_specs=[pl.BlockSpec((blk, S5_WIDTH), lambda c: (c, 0))] + [_full_spec(a.shape) for a in consts],
        out_specs=[pl.BlockSpec((blk, S5_WIDTH), lambda c: (c, 0)),
                   _full_spec((nb, S5_NSTATE)), _full_spec((nb, S5_NSTATE))],
        out_shape=[jax.ShapeDtypeStruct((rows, S5_WIDTH), BF16),
                   jax.ShapeDtypeStruct((nb, S5_NSTATE), F32), jax.ShapeDtypeStruct((nb, S5_NSTATE), F32)],
        scratch_shapes=[pltpu.VMEM((blk, S5_NSTATE), F32), pltpu.VMEM((blk, S5_NSTATE), F32),
                        pltpu.VMEM((nb, S5_NSTATE), F32), pltpu.VMEM((nb, S5_NSTATE), F32),
                        pltpu.VMEM((8, S5_NSTATE), F32)],
        compiler_params=_cparams(("arbitrary",)), name="s5",
    )(u, *consts)


def _mm_res_kernel(*refs, nop):
    res_ref = refs[2 * nop]
    o_ref = refs[2 * nop + 1]
    acc = res_ref[...]
    for t in range(nop):
        acc = acc + _dot(refs[t][...].astype(BF16), refs[nop + t][...])
    o_ref[...] = acc


def _mm_res(ops, ws, res, *, tm, op_index):
    m, n = res.shape
    row = lambda i: (i, 0)
    in_specs = ([pl.BlockSpec((tm, w.shape[0]), idx) for w, idx in zip(ws, op_index)]
                + [_full_spec(w.shape) for w in ws] + [pl.BlockSpec((tm, n), row)])
    return pl.pallas_call(
        functools.partial(_mm_res_kernel, nop=len(ops)), grid=(m // tm,), in_specs=in_specs,
        out_specs=pl.BlockSpec((tm, n), row), out_shape=jax.ShapeDtypeStruct((m, n), F32),
        compiler_params=_cparams(("arbitrary",)), name="matmul_residual",
    )(*ops, *ws, res)


def _mem_kv_kernel(x_ref, g_ref, w_ref, kg_ref, k_out, v_out):
    mn = _rms(x_ref[...], g_ref[...]).astype(BF16)
    kv = _dot(mn, w_ref[...])
    kg = kg_ref[...]
    for h in range(MEM_HEADS):
        sl = slice(LANES * h, LANES * (h + 1))
        k_out[:, sl] = _rms(kv[:, sl], kg)
    v_out[...] = kv[:, MEM_WIDTH:]


def _mem_kv(x, g, w, kg, *, tm):
    m = x.shape[0]
    row = lambda i: (i, 0)
    return pl.pallas_call(
        _mem_kv_kernel, grid=(m // tm,),
        in_specs=[pl.BlockSpec((tm, D_MODEL), row), _full_spec(g.shape), _full_spec(w.shape), _full_spec(kg.shape)],
        out_specs=[pl.BlockSpec((tm, MEM_WIDTH), row), pl.BlockSpec((tm, MEM_WIDTH), row)],
        out_shape=[jax.ShapeDtypeStruct((m, MEM_WIDTH), F32), jax.ShapeDtypeStruct((m, MEM_WIDTH), F32)],
        compiler_params=_cparams(("arbitrary",)), name="mem_kv",
    )(x, g, w, kg)


def _mem_attn_kernel(h_ref, g_ref, wq_ref, qg_ref, mk_ref, mv_ref, wo_ref, o_ref, *, nb, tl):
    x = h_ref[...].reshape(nb * tl, D_MODEL)
    hn = _rms(x, g_ref[...]).astype(BF16)
    q = _dot(hn, wq_ref[...])
    qg = qg_ref[...] * (MEM_HEAD_DIM ** -0.5)
    qn = [_rms(q[:, LANES * h:LANES * (h + 1)], qg).astype(BF16) for h in range(MEM_HEADS)]
    rows = []
    for b in range(nb):
        r = slice(tl * b, tl * (b + 1))
        mk = mk_ref[b].astype(BF16)
        mv = mv_ref[b].astype(BF16)
        outs = []
        for h in range(MEM_HEADS):
            sl = slice(LANES * h, LANES * (h + 1))
            s = _dot_nt(qn[h][r, :], mk[:, sl])
            p = jnp.exp(s - jnp.max(s, axis=-1, keepdims=True))
            outs.append(_dot(p.astype(BF16), mv[:, sl]) / jnp.sum(p, axis=-1, keepdims=True))
        rows.append(jnp.concatenate(outs, axis=1))
    o = (rows[0] if nb == 1 else jnp.concatenate(rows, axis=0)).astype(BF16)
    o_ref[...] = (x + _dot(o, wo_ref[...])).reshape(nb, tl, D_MODEL)


def _mem_attn(h, g, wq, qg, mk, mv, wo, *, nb, tl):
    b, l, _ = h.shape
    ml = mk.shape[1]
    blk = lambda bi, i: (bi, i, 0)
    per_b = lambda bi, i: (bi, 0, 0)
    return pl.pallas_call(
        functools.partial(_mem_attn_kernel, nb=nb, tl=tl), grid=(b // nb, l // tl),
        in_specs=[pl.BlockSpec((nb, tl, D_MODEL), blk), _full_spec(g.shape), _full_spec(wq.shape),
                  _full_spec(qg.shape), pl.BlockSpec((nb, ml, MEM_WIDTH), per_b),
                  pl.BlockSpec((nb, ml, MEM_WIDTH), per_b), _full_spec(wo.shape)],
        out_specs=pl.BlockSpec((nb, tl, D_MODEL), blk),
        out_shape=jax.ShapeDtypeStruct(h.shape, F32),
        compiler_params=_cparams(("arbitrary", "arbitrary")), name="mem_attention",
    )(h, g, wq, qg, mk, mv, wo)


def _mlp_kernel(h_ref, g_ref, wu_ref, wd_ref, o_ref, xn_scr, acc_scr):
    j = pl.program_id(1)

    @pl.when(j == 0)
    def _():
        xn_scr[...] = _rms(h_ref[...], g_ref[...]).astype(BF16)
        acc_scr[...] = jnp.zeros(acc_scr.shape, F32)

    a = _dot(xn_scr[...], wu_ref[...])
    a = jnp.square(jnp.maximum(a, 0.0)).astype(BF16)
    acc_scr[...] += _dot(a, wd_ref[...])

    @pl.when(j == pl.num_programs(1) - 1)
    def _():
        o_ref[...] = h_ref[...] + acc_scr[...]


def _mlp(h, g, wu, wd, *, tm, tf):
    m = h.shape[0]
    return pl.pallas_call(
        _mlp_kernel, grid=(m // tm, D_FF // tf),
        in_specs=[pl.BlockSpec((tm, D_MODEL), lambda i, j: (i, 0)), _full_spec(g.shape),
                  pl.BlockSpec((D_MODEL, tf), lambda i, j: (0, j)), pl.BlockSpec((tf, D_MODEL), lambda i, j: (j, 0))],
        out_specs=pl.BlockSpec((tm, D_MODEL), lambda i, j: (i, 0)),
        out_shape=jax.ShapeDtypeStruct((m, D_MODEL), F32),
        scratch_shapes=[pltpu.VMEM((tm, D_MODEL), BF16), pltpu.VMEM((tm, D_MODEL), F32)],
        compiler_params=_cparams(("arbitrary", "arbitrary")), name="mlp",
    )(h, g, wu, wd)


def _norm_mm_kernel(h_ref, g_ref, w_ref, o_ref, xn_scr):
    @pl.when(pl.program_id(1) == 0)
    def _():
        xn_scr[...] = _rms(h_ref[...], g_ref[...]).astype(BF16)

    o_ref[...] = _dot(xn_scr[...], w_ref[...])


def _norm_mm(h, g, w, *, tm, tn):
    m = h.shape[0]
    n = w.shape[1]
    return pl.pallas_call(
        _norm_mm_kernel, grid=(m // tm, n // tn),
        in_specs=[pl.BlockSpec((tm, D_MODEL), lambda i, j: (i, 0)), _full_spec(g.shape),
                  pl.BlockSpec((D_MODEL, tn), lambda i, j: (0, j))],
        out_specs=pl.BlockSpec((tm, tn), lambda i, j: (i, j)),
        out_shape=jax.ShapeDtypeStruct((m, n), F32),
        scratch_shapes=[pltpu.VMEM((tm, D_MODEL), BF16)],
        compiler_params=_cparams(("arbitrary", "arbitrary")), name="norm_matmul",
    )(h, g, w)


def _hgrn_kernel(q_ref, f_ref, i_ref, g_ref, lbp_ref, on_ref, s0_ref, o_ref, s_out, s_scr, *,
                 chunk, nchunk, layer, l_valid):
    c = pl.program_id(1)
    tb = chunk * nchunk

    @pl.when(c == 0)
    def _():
        s_scr[...] = s0_ref[...]

    lbp = lbp_ref[...]
    e = jnp.exp(lbp - jnp.max(lbp, axis=0, keepdims=True))
    sm = e / jnp.sum(e, axis=0, keepdims=True)
    lb = jnp.sum(sm[0:layer + 1, :], axis=0, keepdims=True) - sm[0:1, :]

    q = q_ref[...]
    qa = q * _sigmoid(q)
    fg = lb + (1.0 - lb) * _sigmoid(f_ref[...])
    logf = jnp.log(fg)
    kk = 1.0 - fg
    v = i_ref[...]
    if l_valid is not None:
        valid = (lax.broadcasted_iota(jnp.int32, (tb, 1), 0) + c * tb) < l_valid
        logf = jnp.where(valid, logf, 0.0)
        kk = jnp.where(valid, kk, 0.0)
    vb = v.astype(BF16)

    tr = lax.broadcasted_iota(jnp.int32, (tb, tb), 0)
    tc = lax.broadcasted_iota(jnp.int32, (tb, tb), 1)
    same_chunk = _div_pow2(tr, chunk) == _div_pow2(tc, chunk)
    tri = jnp.where(same_chunk, jnp.where(tr >= tc, 1.0, 0.0), 0.0).astype(BF16)
    hi = logf.astype(BF16)
    lo = (logf - hi.astype(F32)).astype(BF16)
    bcum = _dot(tri, hi) + _dot(tri, lo)
    qhat = (qa * jnp.exp(bcum)).astype(BF16)

    nsub = chunk // HGRN_SUB
    khat, dec, qloc, kloc, masks = [], [], [], [], []
    for ci in range(nchunk):
        c0 = ci * chunk
        blast = bcum[c0 + chunk - 1:c0 + chunk, :]
        khat.append((kk[c0:c0 + chunk, :] * jnp.exp(blast - bcum[c0:c0 + chunk, :])).astype(BF16))
        dec.append(jnp.exp(blast))
        for i in range(nsub):
            r0 = c0 + i * HGRN_SUB
            r1 = r0 + HGRN_SUB
            base = bcum[r0 - 1:r0, :] if i > 0 else jnp.zeros((1, bcum.shape[1]), F32)
            qloc.append((qa[r0:r1, :] * jnp.exp(bcum[r0:r1, :] - base)).astype(BF16))
            kloc.append((kk[c0:r1, :] * jnp.exp(jnp.minimum(base - bcum[c0:r1, :], HGRN_EXP_CLAMP))).astype(BF16))
    for i in range(nsub):
        ncols = (i + 1) * HGRN_SUB
        ar = lax.broadcasted_iota(jnp.int32, (HGRN_SUB, ncols), 0) + i * HGRN_SUB
        ac = lax.broadcasted_iota(jnp.int32, (HGRN_SUB, ncols), 1)
        masks.append(ar >= ac)

    hsl = [slice(HGRN_DK * h, HGRN_DK * (h + 1)) for h in range(HGRN_HEADS)]
    blocks = [(ci, i) for ci in range(nchunk) for i in range(nsub)]
    att = [[_dot_nt(qloc[ci * nsub + i][:, sl], kloc[ci * nsub + i][:, sl]) for ci, i in blocks] for sl in hsl]
    att = [[jnp.where(masks[i], a, 0.0).astype(BF16) for a, (ci, i) in zip(row, blocks)] for row in att]
    intra = [[_dot(a, vb[ci * chunk:ci * chunk + (i + 1) * HGRN_SUB, sl]) for a, (ci, i) in zip(row, blocks)]
             for row, sl in zip(att, hsl)]
    kv = [[_dot_tn(khat[ci][:, sl], vb[ci * chunk:(ci + 1) * chunk, sl]) for ci in range(nchunk)] for sl in hsl]
    dcol = [[_row_to_col(dec[ci][:, sl], HGRN_DK) for ci in range(nchunk)] for sl in hsl]
    st = [s_scr[h] for h in range(HGRN_HEADS)]
    inter = [[] for _ in hsl]
    for ci in range(nchunk):
        rows = slice(ci * chunk, (ci + 1) * chunk)
        for h, sl in enumerate(hsl):
            inter[h].append(_dot(qhat[rows, sl], st[h].astype(BF16)))
        for h in range(HGRN_HEADS):
            st[h] = dcol[h][ci] * st[h] + kv[h][ci]
    o_heads = []
    for h in range(HGRN_HEADS):
        s_scr[h] = st[h]
        parts = [inter[h][ci][i * HGRN_SUB:(i + 1) * HGRN_SUB, :] + intra[h][ci * nsub + i] for ci, i in blocks]
        o_heads.append(parts[0] if len(parts) == 1 else jnp.concatenate(parts, axis=0))

    o = jnp.concatenate(o_heads, axis=1)
    g = g_ref[...]
    o_ref[...] = (_rms(o, on_ref[...]) * (g * _sigmoid(g))).astype(BF16)

    @pl.when(c == pl.num_programs(1) - 1)
    def _():
        s_out[...] = s_scr[...]


def _hgrn(proj, lbp, on, s0, *, chunk, nchunk, layer, l_valid):
    b, l, _ = proj.shape
    w = D_MODEL
    tb = chunk * nchunk

    def col(k):
        return pl.BlockSpec((None, tb, w), lambda bi, c: (bi, c, k))

    st_spec = pl.BlockSpec((None, HGRN_HEADS, HGRN_DK, HGRN_DK), lambda bi, c: (bi, 0, 0, 0))
    return pl.pallas_call(
        functools.partial(_hgrn_kernel, chunk=chunk, nchunk=nchunk, layer=layer, l_valid=l_valid),
        grid=(b, l // tb),
        in_specs=[col(0), col(1), col(2), col(3), _full_spec(lbp.shape), _full_spec(on.shape), st_spec],
        out_specs=[pl.BlockSpec((None, tb, w), lambda bi, c: (bi, c, 0)), st_spec],
        out_shape=[jax.ShapeDtypeStruct((b, l, w), BF16), jax.ShapeDtypeStruct(s0.shape, F32)],
        scratch_shapes=[pltpu.VMEM((HGRN_HEADS, HGRN_DK, HGRN_DK), F32)],
        compiler_params=_cparams(("arbitrary", "arbitrary")), name="hgrn",
    )(proj, proj, proj, proj, lbp, on, s0)


def _pad_last(x, n):
    return jnp.pad(x, [(0, 0)] * (x.ndim - 1) + [(0, n - x.shape[-1])])


def _head_pad(w, per):
    k = w.shape[0]
    return _pad_last(w.reshape(k, -1, per), LANES).reshape(k, -1)


def _rope_tables(pos):
    half = MLA_ROPE // 2
    inv = ROPE_THETA ** (-jnp.arange(half, dtype=F32) / half)
    ang = pos.astype(F32)[:, None] * inv[None, :]
    cos, sin = jnp.cos(ang), jnp.sin(ang)
    n = pos.shape[0]
    z = lambda w: jnp.zeros((n, w), F32)
    scale = MLA_QK ** -0.5
    cq = scale * jnp.concatenate([jnp.ones((n, MLA_NOPE), F32), cos, cos, z(LANES - MLA_QK)], axis=1)
    s1q = scale * jnp.concatenate([z(MLA_NOPE + half), sin, z(LANES - MLA_QK)], axis=1)
    s2q = scale * jnp.concatenate([z(MLA_NOPE), -sin, z(half + LANES - MLA_QK)], axis=1)
    ck = jnp.concatenate([cos, cos, z(LANES - MLA_ROPE)], axis=1)
    s1k = jnp.concatenate([z(half), sin, z(LANES - MLA_ROPE)], axis=1)
    s2k = jnp.concatenate([-sin, z(LANES - half)], axis=1)
    return (cq, s1q, s2q, ck, s1k, s2k)


def _block_diag(x):
    g, a, b = x.shape
    eye = jnp.eye(g, dtype=x.dtype)
    return (x[:, :, None, :] * eye[:, None, :, None]).reshape(g * a, g * b)


def kernel(x_prompt, x_sample, cache_mla_latent, cache_mla_krope, state_s5_re, state_s5_im, state_hgrn, cache_mem_k, cache_mem_v, page_table, mem_prompt, norm_mix, norm_mem, norm_memsrc, norm_mlp, w_mem_q, w_mem_k, w_mem_v, w_mem_o, mem_q_gain, mem_k_gain, w_mlp_up, w_mlp_down, w_in_even, mla_cq_norm, mla_ckv_norm, w_mla_uq, w_mla_ukv, mla_qn_nope, mla_qn_rope, mla_kn_nope, mla_kn_rope, s5_lambda_re, s5_lambda_im, s5_log_step, s5_b_re, s5_b_im, s5_c_re, s5_c_im, s5_d, s5_w_glu, s5_b_glu, w_out_even, w_in_odd, hgrn_lower_bounds, hgrn_out_norm, w_out_odd):
    bsz, seq, _ = x_prompt.shape
    dbs, dseq, _ = x_sample.shape
    depth = norm_mix.shape[0]
    past_len = page_table.shape[1] * PAGE_SIZE
    ns = 8
    mem_len = mem_prompt.shape[1]
    row2 = lambda a: a.reshape(1, -1).astype(F32)

    hp = x_prompt.reshape(bsz * seq, D_MODEL)
    hs = jnp.pad(x_sample, ((0, 0), (0, ns - dseq), (0, 0))).reshape(dbs * ns, D_MODEL)

    tabs_p = _rope_tables(jnp.arange(seq, dtype=jnp.int32))
    pos_s = past_len + jnp.arange(ns, dtype=jnp.int32)
    tabs_s = tuple(jnp.tile(t, (dbs, 1)) for t in _rope_tables(pos_s))

    outs_p = {k: [] for k in ("lat", "kr", "s5r", "s5i", "hg", "mk", "mv")}
    outs_s = {k: [] for k in ("lat", "kr", "s5r", "s5i", "hg")}

    tm_p = 512
    nl_p = seq // tm_p
    tm_r = 1024
    nl_r = seq // tm_r

    for l in range(depth):
        if l % 2 == 0:
            e = l // 2
            w_in = w_in_even[e]
            o1 = MLA_Q_LORA + MLA_KV_LORA
            wp = jnp.concatenate([w_in[:, :o1], _pad_last(w_in[:, o1:o1 + MLA_ROPE], LANES),
                                  w_in[:, o1 + MLA_ROPE:]], axis=1).astype(BF16)
            wuq = _head_pad(w_mla_uq[e], MLA_QK).astype(BF16)
            ukv = w_mla_ukv[e].reshape(MLA_KV_LORA, MLA_HEADS, MLA_NOPE + MLA_V)
            wuk_c = ukv[:, :, :MLA_NOPE].reshape(MLA_KV_LORA, -1)
            wuv_c = ukv[:, :, MLA_NOPE:].reshape(MLA_KV_LORA, -1)
            wkv = jnp.concatenate([_head_pad(wuk_c, MLA_NOPE), _head_pad(wuv_c, MLA_V)], axis=1).astype(BF16)
            qg = _pad_last(jnp.concatenate([mla_qn_nope[e], mla_qn_rope[e], mla_qn_rope[e]])[None, :], LANES)
            kg = _pad_last(jnp.concatenate([mla_kn_nope[e], mla_kn_rope[e], mla_kn_rope[e]])[None, :], LANES)
            cqn = row2(mla_cq_norm[e])
            ckvn = row2(mla_ckv_norm[e])
            g_mix = row2(norm_mix[l])

            brm = _block_diag(jnp.swapaxes(s5_b_re[e], 1, 2)).astype(BF16)
            bim = _block_diag(jnp.swapaxes(s5_b_im[e], 1, 2)).astype(BF16)
            crm = _block_diag(jnp.swapaxes(s5_c_re[e], 1, 2)).astype(BF16)
            cim = _block_diag(jnp.swapaxes(s5_c_im[e], 1, 2)).astype(BF16)
            lamr = row2(s5_lambda_re[e])
            lami = row2(s5_lambda_im[e])
            lstep = row2(jnp.repeat(s5_log_step[e], S5_STATE))
            s5_consts = (lamr, lami, lstep, brm, bim, crm, cim, row2(s5_d[e]), s5_w_glu[e].astype(BF16),
                         row2(s5_b_glu[e]))
            w_out = w_out_even[e]
            wo_att_c = w_out[:MLA_HEADS * MLA_V].astype(BF16)
            wo_att_p = _pad_last(w_out[:MLA_HEADS * MLA_V].reshape(MLA_HEADS, MLA_V, D_MODEL).swapaxes(1, 2),
                                 LANES).swapaxes(1, 2).reshape(HP, D_MODEL).astype(BF16)
            wo_s5 = w_out[MLA_HEADS * MLA_V:].astype(BF16)

            q, k, v, ckv, kr, u = _even_proj(
                hp, g_mix, wp, cqn, wuq, qg, ckvn, wkv, kg, tabs_p, tm=tm_p,
                u_shape=(seq, bsz * S5_WIDTH), u_index=lambda i: (i % nl_p, i // nl_p), emit_qk=False)
            o_att = _flash_attention(q.reshape(bsz, seq, HP), k.reshape(bsz, seq, HP), v.reshape(bsz, seq, HP),
                                     tq=256, tk=512)
            z0 = jnp.zeros((bsz, S5_NSTATE), F32)
            o_s5, hr, hi = _s5(u.reshape(seq * bsz, S5_WIDTH), z0, z0, *s5_consts, tt=64, nb=bsz, strip=512)
            hp = _mm_res([o_att.reshape(bsz * seq, HP), o_s5.reshape(seq, bsz * S5_WIDTH)], [wo_att_p, wo_s5], hp,
                         tm=tm_r, op_index=[lambda i: (i, 0), lambda i: (i % nl_r, i // nl_r)])
            outs_p["lat"].append(ckv.reshape(bsz, seq, MLA_KV_LORA))
            outs_p["kr"].append(kr.reshape(bsz, seq, MLA_ROPE))
            outs_p["s5r"].append(hr.reshape(bsz, S5_GROUPS, S5_STATE))
            outs_p["s5i"].append(hi.reshape(bsz, S5_GROUPS, S5_STATE))

            m_s = dbs * ns
            q, k, v, ckv, kr, u, qk = _even_proj(
                hs, g_mix, wp, cqn, wuq, qg, ckvn, wkv, kg, tabs_s, tm=512,
                u_shape=(m_s, S5_WIDTH), u_index=lambda i: (i, 0), emit_qk=True)
            del q, k, v
            ckv3 = ckv.reshape(dbs, ns, MLA_KV_LORA)
            kr3 = kr.reshape(dbs, ns, MLA_ROPE)
            qk4 = qk.reshape(dbs, ns, MLA_HEADS, LANES)
            eye_h = jnp.eye(MLA_HEADS, dtype=BF16)
            ncols = ns * MLA_HEADS
            sub = LANES // MLA_HEADS
            nblk = MLA_NOPE // sub
            qn = (jnp.transpose(qk4[..., :MLA_NOPE], (0, 2, 3, 1))[..., None]
                  * eye_h[None, :, None, None, :])
            qn = qn.reshape(dbs, MLA_HEADS, nblk, sub, ncols).swapaxes(1, 2).reshape(dbs, MLA_HEADS * MLA_NOPE, ncols)
            qn = _pad_last(qn, LANES)
            wuk_p = (wuk_c.reshape(MLA_KV_LORA, MLA_HEADS, nblk, sub).swapaxes(1, 2)
                     .reshape(MLA_KV_LORA, MLA_HEADS * MLA_NOPE).astype(BF16))
            qr = jnp.transpose(qk4[..., MLA_NOPE:MLA_QK], (0, 3, 1, 2)).reshape(dbs, MLA_ROPE, ncols)
            qr = _pad_last(qr, LANES)
            colmask = (jnp.arange(LANES) < ncols)
            e16 = ((jnp.arange(LANES)[:, None] // sub == (jnp.arange(LANES)[None, :] % MLA_HEADS))
                   & colmask[None, :]).astype(BF16)
            onr = jnp.broadcast_to(colmask[None, :], (MLA_ROPE, LANES)).astype(BF16)
            zb = lambda r: jnp.zeros((dbs, r, LANES), BF16)
            bc = lambda x: jnp.broadcast_to(x[None], (dbs,) + x.shape)
            rhs2 = jnp.concatenate([
                jnp.concatenate([bc(e16), zb(LANES)], axis=2),
                jnp.concatenate([zb(MLA_ROPE), qr], axis=2),
                jnp.concatenate([bc(onr), zb(MLA_ROPE)], axis=2),
                jnp.zeros((dbs, LANES - 2 * MLA_ROPE, 2 * LANES), BF16)], axis=1)
            nnew = 16
            cnew = jnp.pad(ckv3, ((0, 0), (0, nnew - ns), (0, 0)))
            krnew = jnp.pad(kr3, ((0, 0), (0, nnew - ns), (0, 0)))
            o_att_s = _paged_attention(page_table, cache_mla_latent, jnp.swapaxes(cache_mla_krope, 2, 3), e,
                                       qn, rhs2, wuk_p, cnew, krnew, wuv_c.astype(BF16), npg=16, ngrp=4, nq=ns)
            u_tb = jnp.transpose(u.reshape(dbs, ns, S5_WIDTH)[:, :dseq], (1, 0, 2)).reshape(dseq * dbs, S5_WIDTH)
            o_s5, hr, hi = _s5(u_tb, state_s5_re[e].reshape(dbs, S5_NSTATE), state_s5_im[e].reshape(dbs, S5_NSTATE),
                               *s5_consts, tt=dseq, nb=dbs, strip=512)
            o_s5 = jnp.transpose(o_s5.reshape(dseq, dbs, S5_WIDTH), (1, 0, 2))
            o_s5 = jnp.pad(o_s5, ((0, 0), (0, ns - dseq), (0, 0))).reshape(m_s, S5_WIDTH)
            hs = _mm_res([o_att_s.reshape(m_s, MLA_HEADS * MLA_V), o_s5], [wo_att_c, wo_s5], hs,
                         tm=m_s, op_index=[lambda i: (i, 0), lambda i: (i, 0)])
            outs_s["lat"].append(ckv3[:, :dseq])
            outs_s["kr"].append(kr3[:, :dseq])
            outs_s["s5r"].append(hr.reshape(dbs, S5_GROUPS, S5_STATE))
            outs_s["s5i"].append(hi.reshape(dbs, S5_GROUPS, S5_STATE))
        else:
            o = l // 2
            g_mix = row2(norm_mix[l])
            w_in = w_in_odd[o].astype(BF16)
            w_out = w_out_odd[o].astype(BF16)
            on = row2(hgrn_out_norm[o])
            lbp = hgrn_lower_bounds.astype(F32)

            proj = _norm_mm(hp, g_mix, w_in, tm=1024, tn=1024)
            s_zero = jnp.zeros((bsz, HGRN_HEADS, HGRN_DK, HGRN_DK), F32)
            og, st = _hgrn(proj.reshape(bsz, seq, 4 * D_MODEL), lbp, on, s_zero, chunk=64, nchunk=4, layer=l,
                           l_valid=None)
            hp = _mm_res([og.reshape(bsz * seq, D_MODEL)], [w_out], hp, tm=tm_r, op_index=[lambda i: (i, 0)])
            outs_p["hg"].append(st)

            m_s = dbs * ns
            proj = _norm_mm(hs, g_mix, w_in, tm=m_s, tn=1024)
            lpad = HGRN_SUB
            proj = jnp.pad(proj.reshape(dbs, ns, 4 * D_MODEL), ((0, 0), (0, lpad - ns), (0, 0)))
            og, st = _hgrn(proj, lbp, on, state_hgrn[o], chunk=lpad, nchunk=1, layer=l, l_valid=dseq)
            hs = _mm_res([og[:, :ns].reshape(m_s, D_MODEL)], [w_out], hs, tm=m_s, op_index=[lambda i: (i, 0)])
            outs_s["hg"].append(st)

        g_mem = row2(norm_mem[l])
        wq = w_mem_q[l].astype(BF16)
        wo = w_mem_o[l].astype(BF16)
        mqg = row2(mem_q_gain[l])
        wkv_m = jnp.concatenate([w_mem_k[l], w_mem_v[l]], axis=1).astype(BF16)
        mk, mv = _mem_kv(mem_prompt.reshape(bsz * mem_len, D_MODEL), row2(norm_memsrc[l]), wkv_m,
                         row2(mem_k_gain[l]), tm=512)
        mk = mk.reshape(bsz, mem_len, MEM_WIDTH)
        mv = mv.reshape(bsz, mem_len, MEM_WIDTH)
        outs_p["mk"].append(mk.reshape(bsz, mem_len, MEM_HEADS, MEM_HEAD_DIM))
        outs_p["mv"].append(mv.reshape(bsz, mem_len, MEM_HEADS, MEM_HEAD_DIM))
        hp = _mem_attn(hp.reshape(bsz, seq, D_MODEL), g_mem, wq, mqg, mk, mv, wo,
                       nb=1, tl=512).reshape(bsz * seq, D_MODEL)
        hs = _mem_attn(hs.reshape(dbs, ns, D_MODEL), g_mem, wq, mqg,
                       cache_mem_k[l].reshape(dbs, mem_len, MEM_WIDTH), cache_mem_v[l].reshape(dbs, mem_len, MEM_WIDTH),
                       wo, nb=8, tl=ns).reshape(dbs * ns, D_MODEL)

        g_mlp = row2(norm_mlp[l])
        wu = w_mlp_up[l].astype(BF16)
        wd = w_mlp_down[l].astype(BF16)
        hp = _mlp(hp, g_mlp, wu, wd, tm=1024, tf=1024)
        hs = _mlp(hs, g_mlp, wu, wd, tm=dbs * ns, tf=1024)

    y_p = hp.reshape(bsz, seq, D_MODEL)
    y_s = hs.reshape(dbs, ns, D_MODEL)[:, :dseq]
    return (y_p, y_s,
            jnp.stack(outs_p["lat"], axis=1), jnp.stack(outs_p["kr"], axis=1),
            jnp.stack(outs_p["s5r"]), jnp.stack(outs_p["s5i"]), jnp.stack(outs_p["hg"]),
            jnp.stack(outs_p["mk"]), jnp.stack(outs_p["mv"]),
            jnp.stack(outs_s["lat"], axis=1), jnp.stack(outs_s["kr"], axis=1),
            jnp.stack(outs_s["s5r"]), jnp.stack(outs_s["s5i"]), jnp.stack(outs_s["hg"]))
```

```python
import functools
import math

import jax
import jax.numpy as jnp
from jax import lax
from jax.experimental import pallas as pl
from jax.experimental.pallas import tpu as pltpu

F32 = jnp.float32
BF16 = jnp.bfloat16

LANES = 128
VMEM_LIMIT_BYTES = 56 * 1024 * 1024

D_MODEL = 1024
MLA_HEADS = 8
MLA_NOPE = 64
MLA_ROPE = 32
MLA_QK = MLA_NOPE + MLA_ROPE
MLA_V = 64
MLA_Q_LORA = 768
MLA_KV_LORA = 256
ROPE_THETA = 10000.0
PAGE_SIZE = 128
S5_WIDTH = 512
S5_GROUP = 16
S5_GROUPS = S5_WIDTH // S5_GROUP
S5_STATE = 64
S5_NSTATE = S5_GROUPS * S5_STATE
HGRN_HEADS = 8
HGRN_DK = 128
HGRN_SUB = 16
HGRN_EXP_CLAMP = 80.0
MEM_HEADS = 4
MEM_HEAD_DIM = 128
MEM_WIDTH = MEM_HEADS * MEM_HEAD_DIM
D_FF = 4 * D_MODEL
EPS = 1e-6
HP = MLA_HEADS * LANES


def _cparams(sem):
    return pltpu.CompilerParams(dimension_semantics=sem, vmem_limit_bytes=VMEM_LIMIT_BYTES)


def _rms(x, g):
    return x * lax.rsqrt(jnp.mean(x * x, axis=-1, keepdims=True) + EPS) * g


def _sigmoid(x):
    return 1.0 / (1.0 + jnp.exp(-x))


def _dot(a, b):
    return jnp.dot(a, b, preferred_element_type=F32)


def _dot_nt(a, b):
    return lax.dot_general(a, b, (((1,), (1,)), ((), ())), preferred_element_type=F32)


def _dot_tn(a, b):
    return lax.dot_general(a, b, (((0,), (0,)), ((), ())), preferred_element_type=F32)


def _row_to_col(row, n):
    r = lax.broadcasted_iota(jnp.int32, (n, n), 0)
    c = lax.broadcasted_iota(jnp.int32, (n, n), 1)
    return jnp.sum(jnp.where(r == c, jnp.broadcast_to(row, (n, n)), 0.0), axis=1, keepdims=True)


def _div_pow2(x, d):
    return lax.shift_right_logical(x, int(math.log2(d)))


def _full_spec(shape):
    nd = len(shape)
    return pl.BlockSpec(shape, lambda *_: (0,) * nd)


def row_spec(tm, width):
    return pl.BlockSpec((tm, width), lambda i: (i, 0))


def _even_proj_kernel(h_ref, g_ref, wp_ref, cqn_ref, wuq_ref, qg_ref, ckvn_ref, wkv_ref, kg_ref,
                      cq_ref, s1q_ref, s2q_ref, ck_ref, s1k_ref, s2k_ref,
                      q_out, k_out, v_out, ckv_out, kr_out, u_out, *maybe_qk_out):
    x = h_ref[...]
    hn = _rms(x, g_ref[...]).astype(BF16)
    proj = _dot(hn, wp_ref[...])
    o1 = MLA_Q_LORA
    o2 = o1 + MLA_KV_LORA
    o3 = o2 + LANES
    u_out[...] = proj[:, o3:]
    cq = _rms(proj[:, :o1], cqn_ref[...]).astype(BF16)
    qf = _dot(cq, wuq_ref[...])
    ckv = _rms(proj[:, o1:o2], ckvn_ref[...])
    ckv_out[...] = ckv
    kv = _dot(ckv.astype(BF16), wkv_ref[...])
    kr = proj[:, o2:o3]
    half = MLA_ROPE // 2
    krr = (kr * ck_ref[...] + pltpu.roll(kr, half, 1) * s1k_ref[...]
           + pltpu.roll(kr, LANES - half, 1) * s2k_ref[...])
    kr_out[...] = krr[:, :MLA_ROPE]
    kr_sh = pltpu.roll(krr, MLA_NOPE, 1)
    qg = qg_ref[...]
    kg = kg_ref[...]
    cq_t, s1q_t, s2q_t = cq_ref[...], s1q_ref[...], s2q_ref[...]
    inv_qk = 1.0 / MLA_QK
    for h in range(MLA_HEADS):
        sl = slice(LANES * h, LANES * (h + 1))
        qh = qf[:, sl]
        qh = qh * lax.rsqrt(jnp.sum(qh * qh, axis=-1, keepdims=True) * inv_qk + EPS) * qg
        qh = (qh * cq_t + pltpu.roll(qh, half, 1) * s1q_t + pltpu.roll(qh, LANES - half, 1) * s2q_t)
        q_out[:, sl] = qh.astype(BF16)
        if maybe_qk_out:
            maybe_qk_out[0][:, sl] = (qh * kg).astype(BF16)
        kh = kv[:, sl] + kr_sh
        kh = kh * lax.rsqrt(jnp.sum(kh * kh, axis=-1, keepdims=True) * inv_qk + EPS) * kg
        k_out[:, sl] = kh.astype(BF16)
    v_out[...] = kv[:, HP:].T.astype(BF16)


def _even_proj(h, g, wp, cqn, wuq, qg, ckvn, wkv, kg, tabs, *, tm, u_shape, u_spec, emit_qk):
    m = h.shape[0]
    ltab = tabs[0].shape[0]
    ntab = ltab // tm
    row = lambda i: (i, 0)
    tab_spec = pl.BlockSpec((tm, LANES), lambda i: (i % ntab, 0))
    in_specs = [pl.BlockSpec((tm, D_MODEL), row), _full_spec(g.shape), _full_spec(wp.shape),
                _full_spec(cqn.shape), _full_spec(wuq.shape), _full_spec(qg.shape),
                _full_spec(ckvn.shape), _full_spec(wkv.shape), _full_spec(kg.shape)] + [tab_spec] * 6
    out_shape = [jax.ShapeDtypeStruct((m, HP), BF16), jax.ShapeDtypeStruct((m, HP), BF16),
                 jax.ShapeDtypeStruct((m // tm, HP, tm), BF16), jax.ShapeDtypeStruct((m, MLA_KV_LORA), F32),
                 jax.ShapeDtypeStruct((m, MLA_ROPE), F32), jax.ShapeDtypeStruct(u_shape, F32)]
    out_specs = [pl.BlockSpec((tm, HP), row), pl.BlockSpec((tm, HP), row),
                 pl.BlockSpec((None, HP, tm), lambda i: (i, 0, 0)),
                 pl.BlockSpec((tm, MLA_KV_LORA), row), pl.BlockSpec((tm, MLA_ROPE), row),
                 u_spec]
    if emit_qk:
        out_shape.append(jax.ShapeDtypeStruct((m, HP), BF16))
        out_specs.append(pl.BlockSpec((tm, HP), row))
    return pl.pallas_call(
        _even_proj_kernel, grid=(m // tm,), in_specs=in_specs, out_specs=out_specs, out_shape=out_shape,
        compiler_params=_cparams(("arbitrary",)), name="even_proj",
    )(h, g, wp, cqn, wuq, qg, ckvn, wkv, kg, *tabs)


def _flash_kernel(q_ref, k_ref, vt_ref, o_ref, *, tq, tk, hg):
    i = pl.program_id(1)
    nfull = (i * tq) // tk
    key = lax.broadcasted_iota(jnp.int32, (tk, tq), 0) + nfull * tk
    qry = lax.broadcasted_iota(jnp.int32, (tk, tq), 1) + i * tq
    causal = qry >= key

    heads = [slice(LANES * h, LANES * (h + 1)) for h in range(MLA_HEADS)]

    def update(j, carry, masked):
        off = pl.multiple_of(j * tk, tk)
        out = []
        for h0 in range(0, MLA_HEADS, hg):
            grp = range(h0, h0 + hg)
            scores = [_dot_nt(k_ref[pl.ds(off, tk), heads[h]], q_ref[:, heads[h]]) for h in grp]
            probs, stats = [], []
            for h, s in zip(grp, scores):
                m, l = carry[3 * h], carry[3 * h + 1]
                if masked:
                    s = jnp.where(causal, s, -jnp.inf)
                m_new = jnp.maximum(m, jnp.max(s, axis=0, keepdims=True))
                alpha = jnp.exp(m - m_new)
                p = jnp.exp(s - m_new)
                stats.append((m_new, alpha * l + jnp.sum(p, axis=0, keepdims=True), alpha))
                probs.append(p.astype(BF16))
            pv = [_dot(vt_ref[j, heads[h], :], p) for p, h in zip(probs, grp)]
            for h, (m_new, l_new, alpha), o in zip(grp, stats, pv):
                out += [m_new, l_new, alpha * carry[3 * h + 2] + o]
        return tuple(out)

    init = (jnp.full((1, tq), -jnp.inf, F32), jnp.zeros((1, tq), F32), jnp.zeros((LANES, tq), F32))
    carry = lax.fori_loop(0, nfull, lambda j, c: update(j, c, False), init * MLA_HEADS)
    carry = update(nfull, carry, True)
    for h, sl in enumerate(heads):
        o_ref[:, sl] = (carry[3 * h + 2] / carry[3 * h + 1]).T.astype(BF16)


def _flash_attention(q, k, vt, *, tq, hg=2):
    b, l, _ = q.shape
    tk = vt.shape[2]
    nkb = l // tk
    return pl.pallas_call(
        functools.partial(_flash_kernel, tq=tq, tk=tk, hg=hg),
        grid=(b, l // tq),
        in_specs=[pl.BlockSpec((None, tq, HP), lambda bi, i: (bi, i, 0)),
                  pl.BlockSpec((None, l, HP), lambda bi, i: (bi, 0, 0)),
                  pl.BlockSpec((nkb, HP, tk), lambda bi, i: (bi, 0, 0))],
        out_specs=pl.BlockSpec((None, tq, HP), lambda bi, i: (bi, i, 0)),
        out_shape=jax.ShapeDtypeStruct((b, l, HP), BF16),
        compiler_params=_cparams(("arbitrary", "arbitrary")), name="prompt_attention",
    )(q, k, vt)


def _paged_kernel(pt_ref, *refs, npg, ngrp, nsteps, nq):
    lat_refs = refs[:npg]
    krt_refs = refs[npg:2 * npg]
    (qn_ref, rhs2_ref, wuk_ref, cnew_ref, krnew_ref, wuv_ref,
     o_ref, wabs, m_scr, l_scr, a_scr) = refs[2 * npg:]
    del pt_ref
    s = pl.program_id(1)
    nslots = nsteps * ngrp + 1
    ncol = LANES
    inv_qk = 1.0 / MLA_QK

    @pl.when(s == 0)
    def _():
        wabs[...] = _dot(wuk_ref[...], qn_ref[...]).astype(BF16)

    def stats(blocks, mask):
        kn = [_dot(c, wuk_ref[...]) for c, _ in blocks]
        sq = [k * k for k in kn]
        psum = [q[:, 0:LANES] + q[:, LANES:2 * LANES] + q[:, 2 * LANES:3 * LANES] + q[:, 3 * LANES:] for q in sq]
        r2 = [_dot(jnp.concatenate([p.astype(BF16), x], axis=1), rhs2_ref[...]) for p, (_, x) in zip(psum, blocks)]
        scn = [_dot(c, wabs[...]) for c, _ in blocks]
        probs, out = [], []
        for t in range(len(blocks)):
            sc = (scn[t] + r2[t][:, LANES:]) * lax.rsqrt(r2[t][:, :LANES] * inv_qk + EPS)
            if mask is not None:
                sc = jnp.where(mask, sc, -jnp.inf)
            m = jnp.max(sc, axis=0, keepdims=True)
            p = jnp.exp(sc - m)
            out.append((m, jnp.sum(p, axis=0, keepdims=True)))
            probs.append(p.astype(BF16))
        acc = [_dot_tn(p, c) for p, (c, _) in zip(probs, blocks)]
        return [(m, l, a) for (m, l), a in zip(out, acc)]

    zpad = jnp.zeros((LANES - 2 * MLA_ROPE, PAGE_SIZE), F32)

    def rope_block(g):
        krt = krt_refs[g][...]
        return jnp.concatenate([krt, krt * krt, zpad], axis=0).T.astype(BF16)

    pg = npg // ngrp
    groups = [(jnp.concatenate([lat_refs[g][...].astype(BF16) for g in range(pg * t, pg * (t + 1))], axis=0),
               jnp.concatenate([rope_block(g) for g in range(pg * t, pg * (t + 1))], axis=0))
              for t in range(ngrp)]
    for t, (m, l, a) in enumerate(stats(groups, None)):
        slot = s * ngrp + t
        m_scr[pl.ds(slot, 1), :] = m
        l_scr[pl.ds(slot, 1), :] = l
        a_scr[slot] = a

    @pl.when(s == nsteps - 1)
    def _():
        nnew = cnew_ref.shape[0]
        krn = jnp.concatenate([krnew_ref[...], jnp.zeros((nnew, LANES - MLA_ROPE), F32)], axis=1)
        krn = krn + pltpu.roll(krn * krn, MLA_ROPE, 1)
        key = lax.broadcasted_iota(jnp.int32, (nnew, ncol), 0)
        qry = _div_pow2(lax.broadcasted_iota(jnp.int32, (nnew, ncol), 1), MLA_HEADS)
        (m2, l2, a2), = stats([(cnew_ref[...].astype(BF16), krn.astype(BF16))], key <= qry)
        m_scr[nslots - 1:nslots, :] = m2
        l_scr[nslots - 1:nslots, :] = l2
        a_scr[nslots - 1] = a2
        mall = m_scr[0:nslots, :]
        w = jnp.exp(mall - jnp.max(mall, axis=0, keepdims=True))
        den = jnp.sum(l_scr[0:nslots, :] * w, axis=0, keepdims=True)
        wn = w / den
        num = jnp.zeros((ncol, MLA_KV_LORA), F32)
        for t in range(nslots):
            num = num + a_scr[t] * _row_to_col(wn[t:t + 1, :], ncol)
        full = _dot(num.astype(BF16), wuv_ref[...])
        hrow = lax.broadcasted_iota(jnp.int32, (MLA_HEADS, MLA_HEADS * MLA_V), 0)
        hcol = _div_pow2(lax.broadcasted_iota(jnp.int32, (MLA_HEADS, MLA_HEADS * MLA_V), 1), MLA_V)
        rows = []
        for qi in range(nq):
            blk = full[MLA_HEADS * qi:MLA_HEADS * (qi + 1), :]
            rows.append(jnp.sum(jnp.where(hrow == hcol, blk, 0.0), axis=0, keepdims=True))
        o_ref[...] = jnp.concatenate(rows, axis=0)


def _paged_attention(page_table, cache_lat, cache_krt, e, qn, rhs2, wuk, cnew, krnew, wuv, *, npg, ngrp, nq):
    nb, npages = page_table.shape
    nsteps = npages // npg
    t = npg * PAGE_SIZE
    nnew = cnew.shape[1]
    nslots = nsteps * ngrp + 1

    def page_spec(shape, g):
        return pl.BlockSpec((None, None) + shape, lambda b, s, pt: (pt[b, s * npg + g], e, 0, 0))

    per_b3 = lambda b, s, pt: (b, 0, 0)
    const2 = lambda b, s, pt: (0, 0)
    in_specs = ([page_spec((PAGE_SIZE, MLA_KV_LORA), g) for g in range(npg)]
                + [page_spec((MLA_ROPE, PAGE_SIZE), g) for g in range(npg)]
                + [pl.BlockSpec((None,) + qn.shape[1:], per_b3), pl.BlockSpec((None,) + rhs2.shape[1:], per_b3),
                   pl.BlockSpec(wuk.shape, const2),
                   pl.BlockSpec((None, nnew, MLA_KV_LORA), per_b3), pl.BlockSpec((None, nnew, MLA_ROPE), per_b3),
                   pl.BlockSpec(wuv.shape, const2)])
    grid_spec = pltpu.PrefetchScalarGridSpec(
        num_scalar_prefetch=1, grid=(nb, nsteps), in_specs=in_specs,
        out_specs=pl.BlockSpec((None, nq, MLA_HEADS * MLA_V), per_b3),
        scratch_shapes=[pltpu.VMEM((MLA_KV_LORA, LANES), BF16),
                        pltpu.VMEM((nslots, LANES), F32), pltpu.VMEM((nslots, LANES), F32),
                        pltpu.VMEM((nslots, LANES, MLA_KV_LORA), F32)])
    return pl.pallas_call(
        functools.partial(_paged_kernel, npg=npg, ngrp=ngrp, nsteps=nsteps, nq=nq),
        grid_spec=grid_spec, out_shape=jax.ShapeDtypeStruct((nb, nq, MLA_HEADS * MLA_V), F32),
        compiler_params=_cparams(("arbitrary", "arbitrary")), name="paged_attention",
    )(page_table, *([cache_lat] * npg), *([cache_krt] * npg), qn, rhs2, wuk, cnew, krnew, wuv)


def _s5_kernel(u_ref, h0r_ref, h0i_ref, lamr_ref, lami_ref, lstep_ref, brm_ref, bim_ref, crm_ref, cim_ref,
               d_ref, wg_ref, bg_ref, o_ref, hr_out, hi_out, xr_scr, xi_scr, hcr, hci, disc, *, tt, nb, strip):
    c = pl.program_id(0)

    @pl.when(c == 0)
    def _():
        lr = jnp.minimum(lamr_ref[...], -1e-4)
        li = lami_ref[...]
        dt = jnp.exp(lstep_ref[...])
        mag = jnp.exp(lr * dt)
        abr = mag * jnp.cos(li * dt)
        abi = mag * jnp.sin(li * dt)
        den = lr * lr + li * li
        disc[0:1, :] = abr
        disc[1:2, :] = abi
        disc[2:3, :] = ((abr - 1.0) * lr + abi * li) / den
        disc[3:4, :] = (abi * lr - (abr - 1.0) * li) / den
        hcr[...] = h0r_ref[...]
        hci[...] = h0i_ref[...]

    u = u_ref[...]
    ub = u.astype(BF16)
    pr = _dot(ub, brm_ref[...])
    pi = _dot(ub, bim_ref[...])
    cor = disc[2:3, :]
    coi = disc[3:4, :]
    xr_scr[...] = cor * pr - coi * pi
    xi_scr[...] = cor * pi + coi * pr

    for s0 in range(0, S5_NSTATE, strip):
        lanes = slice(s0, s0 + strip)
        ar = jnp.broadcast_to(disc[0:1, lanes], (nb, strip))
        ai = jnp.broadcast_to(disc[1:2, lanes], (nb, strip))

        def step(t, carry, lanes=lanes, ar=ar, ai=ai):
            hr, hi = carry
            rows = pl.ds(pl.multiple_of(t * nb, nb), nb)
            nr = ar * hr - ai * hi + xr_scr[rows, lanes]
            ni = ar * hi + ai * hr + xi_scr[rows, lanes]
            xr_scr[rows, lanes] = nr
            xi_scr[rows, lanes] = ni
            return nr, ni

        hr, hi = lax.fori_loop(0, tt, step, (hcr[:, lanes], hci[:, lanes]))
        hcr[:, lanes] = hr
        hci[:, lanes] = hi

    y = _dot(xr_scr[...].astype(BF16), crm_ref[...]) - _dot(xi_scr[...].astype(BF16), cim_ref[...])
    y = y + d_ref[...] * u
    z = jax.nn.gelu(y)
    gate = _sigmoid(_dot(z.astype(BF16), wg_ref[...]) + bg_ref[...])
    o_ref[...] = (z * gate).astype(BF16)

    @pl.when(c == pl.num_programs(0) - 1)
    def _():
        hr_out[...] = hcr[...]
        hi_out[...] = hci[...]


def _s5(u, h0r, h0i, lamr, lami, lstep, brm, bim, crm, cim, d, wg, bg, *, tt, nb, strip):
    rows = u.shape[0]
    blk = tt * nb
    consts = (h0r, h0i, lamr, lami, lstep, brm, bim, crm, cim, d, wg, bg)
    return pl.pallas_call(
        functools.partial(_s5_kernel, tt=tt, nb=nb, strip=strip),
        grid=(rows // blk,),
        in_specs=[row_spec(blk, S5_WIDTH)] + [_full_spec(a.shape) for a in consts],
        out_specs=[row_spec(blk, S5_WIDTH), _full_spec((nb, S5_NSTATE)), _full_spec((nb, S5_NSTATE))],
        out_shape=[jax.ShapeDtypeStruct((rows, S5_WIDTH), BF16),
                   jax.ShapeDtypeStruct((nb, S5_NSTATE), F32), jax.ShapeDtypeStruct((nb, S5_NSTATE), F32)],
        scratch_shapes=[pltpu.VMEM((blk, S5_NSTATE), F32), pltpu.VMEM((blk, S5_NSTATE), F32),
                        pltpu.VMEM((nb, S5_NSTATE), F32), pltpu.VMEM((nb, S5_NSTATE), F32),
                        pltpu.VMEM((8, S5_NSTATE), F32)],
        compiler_params=_cparams(("arbitrary",)), name="s5",
    )(u, *consts)


def _mm_res_kernel(*refs, nop):
    res_ref = refs[2 * nop]
    o_ref = refs[2 * nop + 1]
    acc = res_ref[...]
    for t in range(nop):
        acc = acc + _dot(refs[t][...].astype(BF16), refs[nop + t][...])
    o_ref[...] = acc


def _mm_res(ops, ws, res, *, tm, op_specs):
    m, n = res.shape
    row = lambda i: (i, 0)
    in_specs = list(op_specs) + [_full_spec(w.shape) for w in ws] + [pl.BlockSpec((tm, n), row)]
    return pl.pallas_call(
        functools.partial(_mm_res_kernel, nop=len(ops)), grid=(m // tm,), in_specs=in_specs,
        out_specs=pl.BlockSpec((tm, n), row), out_shape=jax.ShapeDtypeStruct((m, n), F32),
        compiler_params=_cparams(("arbitrary",)), name="matmul_residual",
    )(*ops, *ws, res)


def _mem_kv_kernel(x_ref, g_ref, w_ref, kg_ref, k_out, v_out):
    mn = _rms(x_ref[...], g_ref[...]).astype(BF16)
    kv = _dot(mn, w_ref[...])
    kg = kg_ref[...]
    for h in range(MEM_HEADS):
        sl = slice(LANES * h, LANES * (h + 1))
        k_out[:, sl] = _rms(kv[:, sl], kg)
    v_out[...] = kv[:, MEM_WIDTH:]


def _mem_kv(x, g, w, kg, *, tm):
    m = x.shape[0]
    row = lambda i: (i, 0)
    return pl.pallas_call(
        _mem_kv_kernel, grid=(m // tm,),
        in_specs=[pl.BlockSpec((tm, D_MODEL), row), _full_spec(g.shape), _full_spec(w.shape), _full_spec(kg.shape)],
        out_specs=[pl.BlockSpec((tm, MEM_WIDTH), row), pl.BlockSpec((tm, MEM_WIDTH), row)],
        out_shape=[jax.ShapeDtypeStruct((m, MEM_WIDTH), F32), jax.ShapeDtypeStruct((m, MEM_WIDTH), F32)],
        compiler_params=_cparams(("arbitrary",)), name="mem_kv",
    )(x, g, w, kg)


def _mem_attn_kernel(h_ref, g_ref, wq_ref, qg_ref, mk_ref, mv_ref, wo_ref, o_ref, *, nb, tl, paired):
    x = h_ref[...].reshape(nb * tl, D_MODEL)
    hn = _rms(x, g_ref[...]).astype(BF16)
    q = _dot(hn, wq_ref[...])
    qg = qg_ref[...] * (MEM_HEAD_DIM ** -0.5)
    qn = [_rms(q[:, LANES * h:LANES * (h + 1)], qg).astype(BF16) for h in range(MEM_HEADS)]

    def head_block(ref, b, h):
        if paired:
            return jnp.concatenate([ref[b, :, h, :], ref[b, :, MEM_HEADS + h, :]], axis=0).astype(BF16)
        return ref[b, :, LANES * h:LANES * (h + 1)].astype(BF16)

    rows = []
    for b in range(nb):
        r = slice(tl * b, tl * (b + 1))
        outs = []
        for h in range(MEM_HEADS):
            s = _dot_nt(qn[h][r, :], head_block(mk_ref, b, h))
            p = jnp.exp(s - jnp.max(s, axis=-1, keepdims=True))
            outs.append(_dot(p.astype(BF16), head_block(mv_ref, b, h)) / jnp.sum(p, axis=-1, keepdims=True))
        rows.append(jnp.concatenate(outs, axis=1))
    o = (rows[0] if nb == 1 else jnp.concatenate(rows, axis=0)).astype(BF16)
    o_ref[...] = (x + _dot(o, wo_ref[...])).reshape(nb, tl, D_MODEL)


def _mem_attn(h, g, wq, qg, mk, mv, wo, *, nb, tl, layer=None):
    b, l, _ = h.shape
    blk = lambda bi, i: (bi, i, 0)
    if layer is None:
        mem_spec = pl.BlockSpec((nb,) + mk.shape[1:], lambda bi, i: (bi, 0, 0))
    else:
        mem_spec = pl.BlockSpec((None, nb) + mk.shape[2:], lambda bi, i: (layer, bi, 0, 0, 0))
    return pl.pallas_call(
        functools.partial(_mem_attn_kernel, nb=nb, tl=tl, paired=layer is not None), grid=(b // nb, l // tl),
        in_specs=[pl.BlockSpec((nb, tl, D_MODEL), blk), _full_spec(g.shape), _full_spec(wq.shape),
                  _full_spec(qg.shape), mem_spec, mem_spec, _full_spec(wo.shape)],
        out_specs=pl.BlockSpec((nb, tl, D_MODEL), blk),
        out_shape=jax.ShapeDtypeStruct(h.shape, F32),
        compiler_params=_cparams(("arbitrary", "arbitrary")), name="mem_attention",
    )(h, g, wq, qg, mk, mv, wo)


def _mlp_kernel(h_ref, g_ref, wu_ref, wd_ref, o_ref, xn_scr, acc_scr):
    j = pl.program_id(1)

    @pl.when(j == 0)
    def _():
        xn_scr[...] = _rms(h_ref[...], g_ref[...]).astype(BF16)
        acc_scr[...] = jnp.zeros(acc_scr.shape, F32)

    a = _dot(xn_scr[...], wu_ref[...])
    a = jnp.square(jnp.maximum(a, 0.0)).astype(BF16)
    acc_scr[...] += _dot(a, wd_ref[...])

    @pl.when(j == pl.num_programs(1) - 1)
    def _():
        o_ref[...] = h_ref[...] + acc_scr[...]


def _mlp(h, g, wu, wd, *, tm, tf):
    m = h.shape[0]
    return pl.pallas_call(
        _mlp_kernel, grid=(m // tm, D_FF // tf),
        in_specs=[pl.BlockSpec((tm, D_MODEL), lambda i, j: (i, 0)), _full_spec(g.shape),
                  pl.BlockSpec((D_MODEL, tf), lambda i, j: (0, j)), pl.BlockSpec((tf, D_MODEL), lambda i, j: (j, 0))],
        out_specs=pl.BlockSpec((tm, D_MODEL), lambda i, j: (i, 0)),
        out_shape=jax.ShapeDtypeStruct((m, D_MODEL), F32),
        scratch_shapes=[pltpu.VMEM((tm, D_MODEL), BF16), pltpu.VMEM((tm, D_MODEL), F32)],
        compiler_params=_cparams(("arbitrary", "arbitrary")), name="mlp",
    )(h, g, wu, wd)


def _norm_mm_kernel(h_ref, g_ref, w_ref, o_ref, xn_scr):
    @pl.when(pl.program_id(1) == 0)
    def _():
        xn_scr[...] = _rms(h_ref[...], g_ref[...]).astype(BF16)

    o_ref[...] = _dot(xn_scr[...], w_ref[...])


def _norm_mm(h, g, w, *, tm, tn):
    m = h.shape[0]
    n = w.shape[1]
    return pl.pallas_call(
        _norm_mm_kernel, grid=(m // tm, n // tn),
        in_specs=[pl.BlockSpec((tm, D_MODEL), lambda i, j: (i, 0)), _full_spec(g.shape),
                  pl.BlockSpec((D_MODEL, tn), lambda i, j: (0, j))],
        out_specs=pl.BlockSpec((tm, tn), lambda i, j: (i, j)),
        out_shape=jax.ShapeDtypeStruct((m, n), F32),
        scratch_shapes=[pltpu.VMEM((tm, D_MODEL), BF16)],
        compiler_params=_cparams(("arbitrary", "arbitrary")), name="norm_matmul",
    )(h, g, w)


def _hgrn_kernel(q_ref, f_ref, i_ref, g_ref, lbp_ref, on_ref, s0_ref, o_ref, s_out, s_scr, *,
                 chunk, nchunk, layer, l_valid):
    c = pl.program_id(1)
    tb = chunk * nchunk

    @pl.when(c == 0)
    def _():
        s_scr[...] = s0_ref[...]

    lbp = lbp_ref[...]
    e = jnp.exp(lbp - jnp.max(lbp, axis=0, keepdims=True))
    sm = e / jnp.sum(e, axis=0, keepdims=True)
    lb = jnp.sum(sm[0:layer + 1, :], axis=0, keepdims=True) - sm[0:1, :]

    q = q_ref[...]
    qa = q * _sigmoid(q)
    fg = lb + (1.0 - lb) * _sigmoid(f_ref[...])
    logf = jnp.log(fg)
    kk = 1.0 - fg
    v = i_ref[...]
    if l_valid is not None:
        valid = (lax.broadcasted_iota(jnp.int32, (tb, 1), 0) + c * tb) < l_valid
        logf = jnp.where(valid, logf, 0.0)
        kk = jnp.where(valid, kk, 0.0)
    vb = v.astype(BF16)

    tr = lax.broadcasted_iota(jnp.int32, (tb, tb), 0)
    tc = lax.broadcasted_iota(jnp.int32, (tb, tb), 1)
    same_chunk = _div_pow2(tr, chunk) == _div_pow2(tc, chunk)
    tri = jnp.where(same_chunk, jnp.where(tr >= tc, 1.0, 0.0), 0.0).astype(BF16)
    hi = logf.astype(BF16)
    lo = (logf - hi.astype(F32)).astype(BF16)
    bcum = _dot(tri, hi) + _dot(tri, lo)
    qhat = (qa * jnp.exp(bcum)).astype(BF16)

    nsub = chunk // HGRN_SUB
    khat, dec, qloc, kloc, masks = [], [], [], [], []
    for ci in range(nchunk):
        c0 = ci * chunk
        blast = bcum[c0 + chunk - 1:c0 + chunk, :]
        khat.append((kk[c0:c0 + chunk, :] * jnp.exp(blast - bcum[c0:c0 + chunk, :])).astype(BF16))
        dec.append(jnp.exp(blast))
        for i in range(nsub):
            r0 = c0 + i * HGRN_SUB
            r1 = r0 + HGRN_SUB
            base = bcum[r0 - 1:r0, :] if i > 0 else jnp.zeros((1, bcum.shape[1]), F32)
            qloc.append((qa[r0:r1, :] * jnp.exp(bcum[r0:r1, :] - base)).astype(BF16))
            kloc.append((kk[c0:r1, :] * jnp.exp(jnp.minimum(base - bcum[c0:r1, :], HGRN_EXP_CLAMP))).astype(BF16))
    for i in range(nsub):
        ncols = (i + 1) * HGRN_SUB
        ar = lax.broadcasted_iota(jnp.int32, (HGRN_SUB, ncols), 0) + i * HGRN_SUB
        ac = lax.broadcasted_iota(jnp.int32, (HGRN_SUB, ncols), 1)
        masks.append(ar >= ac)

    hsl = [slice(HGRN_DK * h, HGRN_DK * (h + 1)) for h in range(HGRN_HEADS)]
    blocks = [(ci, i) for ci in range(nchunk) for i in range(nsub)]
    att = [[_dot_nt(qloc[ci * nsub + i][:, sl], kloc[ci * nsub + i][:, sl]) for ci, i in blocks] for sl in hsl]
    att = [[jnp.where(masks[i], a, 0.0).astype(BF16) for a, (ci, i) in zip(row, blocks)] for row in att]
    intra = [[_dot(a, vb[ci * chunk:ci * chunk + (i + 1) * HGRN_SUB, sl]) for a, (ci, i) in zip(row, blocks)]
             for row, sl in zip(att, hsl)]
    kv = [[_dot_tn(khat[ci][:, sl], vb[ci * chunk:(ci + 1) * chunk, sl]) for ci in range(nchunk)] for sl in hsl]
    dcol = [[_row_to_col(dec[ci][:, sl], HGRN_DK) for ci in range(nchunk)] for sl in hsl]
    st = [s_scr[h] for h in range(HGRN_HEADS)]
    inter = [[] for _ in hsl]
    for ci in range(nchunk):
        rows = slice(ci * chunk, (ci + 1) * chunk)
        for h, sl in enumerate(hsl):
            inter[h].append(_dot(qhat[rows, sl], st[h].astype(BF16)))
        for h in range(HGRN_HEADS):
            st[h] = dcol[h][ci] * st[h] + kv[h][ci]
    o_heads = []
    for h in range(HGRN_HEADS):
        s_scr[h] = st[h]
        parts = [inter[h][ci][i * HGRN_SUB:(i + 1) * HGRN_SUB, :] + intra[h][ci * nsub + i] for ci, i in blocks]
        o_heads.append(parts[0] if len(parts) == 1 else jnp.concatenate(parts, axis=0))

    o = jnp.concatenate(o_heads, axis=1)
    g = g_ref[...]
    o_ref[...] = (_rms(o, on_ref[...]) * (g * _sigmoid(g))).astype(BF16)

    @pl.when(c == pl.num_programs(1) - 1)
    def _():
        s_out[...] = s_scr[...]


def _hgrn(proj, lbp, on, s0, *, chunk, nchunk, layer, l_valid):
    b, l, _ = proj.shape
    w = D_MODEL
    tb = chunk * nchunk

    def col(k):
        return pl.BlockSpec((None, tb, w), lambda bi, c: (bi, c, k))

    st_spec = pl.BlockSpec((None, HGRN_HEADS, HGRN_DK, HGRN_DK), lambda bi, c: (bi, 0, 0, 0))
    return pl.pallas_call(
        functools.partial(_hgrn_kernel, chunk=chunk, nchunk=nchunk, layer=layer, l_valid=l_valid),
        grid=(b, l // tb),
        in_specs=[col(0), col(1), col(2), col(3), _full_spec(lbp.shape), _full_spec(on.shape), st_spec],
        out_specs=[pl.BlockSpec((None, tb, w), lambda bi, c: (bi, c, 0)), st_spec],
        out_shape=[jax.ShapeDtypeStruct((b, l, w), BF16), jax.ShapeDtypeStruct(s0.shape, F32)],
        scratch_shapes=[pltpu.VMEM((HGRN_HEADS, HGRN_DK, HGRN_DK), F32)],
        compiler_params=_cparams(("arbitrary", "arbitrary")), name="hgrn",
    )(proj, proj, proj, proj, lbp, on, s0)


def _pad_last(x, n):
    return jnp.pad(x, [(0, 0)] * (x.ndim - 1) + [(0, n - x.shape[-1])])


def _head_pad(w, per):
    k = w.shape[0]
    return _pad_last(w.reshape(k, -1, per), LANES).reshape(k, -1)


def _rope_tables(pos):
    half = MLA_ROPE // 2
    inv = ROPE_THETA ** (-jnp.arange(half, dtype=F32) / half)
    ang = pos.astype(F32)[:, None] * inv[None, :]
    cos, sin = jnp.cos(ang), jnp.sin(ang)
    n = pos.shape[0]
    z = lambda w: jnp.zeros((n, w), F32)
    scale = MLA_QK ** -0.5
    cq = scale * jnp.concatenate([jnp.ones((n, MLA_NOPE), F32), cos, cos, z(LANES - MLA_QK)], axis=1)
    s1q = scale * jnp.concatenate([z(MLA_NOPE + half), sin, z(LANES - MLA_QK)], axis=1)
    s2q = scale * jnp.concatenate([z(MLA_NOPE), -sin, z(half + LANES - MLA_QK)], axis=1)
    ck = jnp.concatenate([cos, cos, z(LANES - MLA_ROPE)], axis=1)
    s1k = jnp.concatenate([z(half), sin, z(LANES - MLA_ROPE)], axis=1)
    s2k = jnp.concatenate([-sin, z(LANES - half)], axis=1)
    return (cq, s1q, s2q, ck, s1k, s2k)


def _block_diag(x):
    g, a, b = x.shape
    eye = jnp.eye(g, dtype=x.dtype)
    return (x[:, :, None, :] * eye[:, None, :, None]).reshape(g * a, g * b)


def kernel(x_prompt, x_sample, cache_mla_latent, cache_mla_krope, state_s5_re, state_s5_im, state_hgrn, cache_mem_k, cache_mem_v, page_table, mem_prompt, norm_mix, norm_mem, norm_memsrc, norm_mlp, w_mem_q, w_mem_k, w_mem_v, w_mem_o, mem_q_gain, mem_k_gain, w_mlp_up, w_mlp_down, w_in_even, mla_cq_norm, mla_ckv_norm, w_mla_uq, w_mla_ukv, mla_qn_nope, mla_qn_rope, mla_kn_nope, mla_kn_rope, s5_lambda_re, s5_lambda_im, s5_log_step, s5_b_re, s5_b_im, s5_c_re, s5_c_im, s5_d, s5_w_glu, s5_b_glu, w_out_even, w_in_odd, hgrn_lower_bounds, hgrn_out_norm, w_out_odd):
    bsz, seq, _ = x_prompt.shape
    dbs, dseq, _ = x_sample.shape
    depth = norm_mix.shape[0]
    past_len = page_table.shape[1] * PAGE_SIZE
    ns = 8
    mem_len = mem_prompt.shape[1]
    row2 = lambda a: a.reshape(1, -1).astype(F32)

    hp = x_prompt.reshape(bsz * seq, D_MODEL)
    hs = jnp.pad(x_sample, ((0, 0), (0, ns - dseq), (0, 0))).reshape(dbs * ns, D_MODEL)

    tabs_p = _rope_tables(jnp.arange(seq, dtype=jnp.int32))
    pos_s = past_len + jnp.arange(ns, dtype=jnp.int32)
    tabs_s = tuple(jnp.tile(t, (dbs, 1)) for t in _rope_tables(pos_s))

    outs_p = {k: [] for k in ("lat", "kr", "s5r", "s5i", "hg", "mk", "mv")}
    outs_s = {k: [] for k in ("lat", "kr", "s5r", "s5i", "hg")}

    tm_p = 512
    nl_p = seq // tm_p
    tm_r = 1024
    nl_r = seq // tm_r

    for l in range(depth):
        if l % 2 == 0:
            e = l // 2
            w_in = w_in_even[e]
            o1 = MLA_Q_LORA + MLA_KV_LORA
            wp = jnp.concatenate([w_in[:, :o1], _pad_last(w_in[:, o1:o1 + MLA_ROPE], LANES),
                                  w_in[:, o1 + MLA_ROPE:]], axis=1).astype(BF16)
            wuq = _head_pad(w_mla_uq[e], MLA_QK).astype(BF16)
            ukv = w_mla_ukv[e].reshape(MLA_KV_LORA, MLA_HEADS, MLA_NOPE + MLA_V)
            wuk_c = ukv[:, :, :MLA_NOPE].reshape(MLA_KV_LORA, -1)
            wuv_c = ukv[:, :, MLA_NOPE:].reshape(MLA_KV_LORA, -1)
            wkv = jnp.concatenate([_head_pad(wuk_c, MLA_NOPE), _head_pad(wuv_c, MLA_V)], axis=1).astype(BF16)
            qg = _pad_last(jnp.concatenate([mla_qn_nope[e], mla_qn_rope[e], mla_qn_rope[e]])[None, :], LANES)
            kg = _pad_last(jnp.concatenate([mla_kn_nope[e], mla_kn_rope[e], mla_kn_rope[e]])[None, :], LANES)
            cqn = row2(mla_cq_norm[e])
            ckvn = row2(mla_ckv_norm[e])
            g_mix = row2(norm_mix[l])

            brm = _block_diag(jnp.swapaxes(s5_b_re[e], 1, 2)).astype(BF16)
            bim = _block_diag(jnp.swapaxes(s5_b_im[e], 1, 2)).astype(BF16)
            crm = _block_diag(jnp.swapaxes(s5_c_re[e], 1, 2)).astype(BF16)
            cim = _block_diag(jnp.swapaxes(s5_c_im[e], 1, 2)).astype(BF16)
            lamr = row2(s5_lambda_re[e])
            lami = row2(s5_lambda_im[e])
            lstep = row2(jnp.repeat(s5_log_step[e], S5_STATE))
            s5_consts = (lamr, lami, lstep, brm, bim, crm, cim, row2(s5_d[e]), s5_w_glu[e].astype(BF16),
                         row2(s5_b_glu[e]))
            w_out = w_out_even[e]
            wo_att_c = w_out[:MLA_HEADS * MLA_V].astype(BF16)
            wo_att_p = _pad_last(w_out[:MLA_HEADS * MLA_V].reshape(MLA_HEADS, MLA_V, D_MODEL).swapaxes(1, 2),
                                 LANES).swapaxes(1, 2).reshape(HP, D_MODEL).astype(BF16)
            wo_s5 = w_out[MLA_HEADS * MLA_V:].astype(BF16)

            q, k, v, ckv, kr, u = _even_proj(
                hp, g_mix, wp, cqn, wuq, qg, ckvn, wkv, kg, tabs_p, tm=tm_p,
                u_shape=(seq, bsz * S5_WIDTH), emit_qk=False,
                u_spec=pl.BlockSpec((tm_p, S5_WIDTH), lambda i: (i % nl_p, i // nl_p)))
            o_att = _flash_attention(q.reshape(bsz, seq, HP), k.reshape(bsz, seq, HP), v, tq=512, hg=4)
            z0 = jnp.zeros((bsz, S5_NSTATE), F32)
            o_s5, hr, hi = _s5(u.reshape(seq * bsz, S5_WIDTH), z0, z0, *s5_consts, tt=64, nb=bsz, strip=512)
            hp = _mm_res([o_att.reshape(bsz * seq, HP), o_s5.reshape(seq, bsz * S5_WIDTH)], [wo_att_p, wo_s5], hp,
                         tm=tm_r, op_specs=[row_spec(tm_r, HP),
                                            pl.BlockSpec((tm_r, S5_WIDTH), lambda i: (i % nl_r, i // nl_r))])
            outs_p["lat"].append(ckv.reshape(bsz, seq, MLA_KV_LORA))
            outs_p["kr"].append(kr.reshape(bsz, seq, MLA_ROPE))
            outs_p["s5r"].append(hr.reshape(bsz, S5_GROUPS, S5_STATE))
            outs_p["s5i"].append(hi.reshape(bsz, S5_GROUPS, S5_STATE))

            m_s = dbs * ns
            q, k, v, ckv, kr, u, qk = _even_proj(
                hs, g_mix, wp, cqn, wuq, qg, ckvn, wkv, kg, tabs_s, tm=512,
                u_shape=(m_s, S5_WIDTH), u_spec=row_spec(512, S5_WIDTH), emit_qk=True)
            del q, k, v
            ckv3 = ckv.reshape(dbs, ns, MLA_KV_LORA)
            kr3 = kr.reshape(dbs, ns, MLA_ROPE)
            qk4 = qk.reshape(dbs, ns, MLA_HEADS, LANES)
            eye_h = jnp.eye(MLA_HEADS, dtype=BF16)
            ncols = ns * MLA_HEADS
            sub = LANES // MLA_HEADS
            nblk = MLA_NOPE // sub
            qn = (jnp.transpose(qk4[..., :MLA_NOPE], (0, 2, 3, 1))[..., None]
                  * eye_h[None, :, None, None, :])
            qn = qn.reshape(dbs, MLA_HEADS, nblk, sub, ncols).swapaxes(1, 2).reshape(dbs, MLA_HEADS * MLA_NOPE, ncols)
            qn = _pad_last(qn, LANES)
            wuk_p = (wuk_c.reshape(MLA_KV_LORA, MLA_HEADS, nblk, sub).swapaxes(1, 2)
                     .reshape(MLA_KV_LORA, MLA_HEADS * MLA_NOPE).astype(BF16))
            qr = jnp.transpose(qk4[..., MLA_NOPE:MLA_QK], (0, 3, 1, 2)).reshape(dbs, MLA_ROPE, ncols)
            qr = _pad_last(qr, LANES)
            colmask = (jnp.arange(LANES) < ncols)
            e16 = ((jnp.arange(LANES)[:, None] // sub == (jnp.arange(LANES)[None, :] % MLA_HEADS))
                   & colmask[None, :]).astype(BF16)
            onr = jnp.broadcast_to(colmask[None, :], (MLA_ROPE, LANES)).astype(BF16)
            zb = lambda r: jnp.zeros((dbs, r, LANES), BF16)
            bc = lambda x: jnp.broadcast_to(x[None], (dbs,) + x.shape)
            rhs2 = jnp.concatenate([
                jnp.concatenate([bc(e16), zb(LANES)], axis=2),
                jnp.concatenate([zb(MLA_ROPE), qr], axis=2),
                jnp.concatenate([bc(onr), zb(MLA_ROPE)], axis=2),
                jnp.zeros((dbs, LANES - 2 * MLA_ROPE, 2 * LANES), BF16)], axis=1)
            nnew = 16
            cnew = jnp.pad(ckv3, ((0, 0), (0, nnew - ns), (0, 0)))
            krnew = jnp.pad(kr3, ((0, 0), (0, nnew - ns), (0, 0)))
            o_att_s = _paged_attention(page_table, cache_mla_latent, jnp.swapaxes(cache_mla_krope, 2, 3), e,
                                       qn, rhs2, wuk_p, cnew, krnew, wuv_c.astype(BF16), npg=32, ngrp=8, nq=ns)
            u_tb = jnp.transpose(u.reshape(dbs, ns, S5_WIDTH)[:, :dseq], (1, 0, 2)).reshape(dseq * dbs, S5_WIDTH)
            o_s5, hr, hi = _s5(u_tb, state_s5_re[e].reshape(dbs, S5_NSTATE), state_s5_im[e].reshape(dbs, S5_NSTATE),
                               *s5_consts, tt=dseq, nb=dbs, strip=512)
            o_s5 = jnp.transpose(o_s5.reshape(dseq, dbs, S5_WIDTH), (1, 0, 2))
            o_s5 = jnp.pad(o_s5, ((0, 0), (0, ns - dseq), (0, 0))).reshape(m_s, S5_WIDTH)
            hs = _mm_res([o_att_s.reshape(m_s, MLA_HEADS * MLA_V), o_s5], [wo_att_c, wo_s5], hs, tm=m_s,
                         op_specs=[row_spec(m_s, MLA_HEADS * MLA_V), row_spec(m_s, S5_WIDTH)])
            outs_s["lat"].append(ckv3[:, :dseq])
            outs_s["kr"].append(kr3[:, :dseq])
            outs_s["s5r"].append(hr.reshape(dbs, S5_GROUPS, S5_STATE))
            outs_s["s5i"].append(hi.reshape(dbs, S5_GROUPS, S5_STATE))
        else:
            o = l // 2
            g_mix = row2(norm_mix[l])
            w_in = w_in_odd[o].astype(BF16)
            w_out = w_out_odd[o].astype(BF16)
            on = row2(hgrn_out_norm[o])
            lbp = hgrn_lower_bounds.astype(F32)

            proj = _norm_mm(hp, g_mix, w_in, tm=1024, tn=1024)
            s_zero = jnp.zeros((bsz, HGRN_HEADS, HGRN_DK, HGRN_DK), F32)
            og, st = _hgrn(proj.reshape(bsz, seq, 4 * D_MODEL), lbp, on, s_zero, chunk=64, nchunk=4, layer=l,
                           l_valid=None)
            hp = _mm_res([og.reshape(bsz * seq, D_MODEL)], [w_out], hp, tm=tm_r, op_specs=[row_spec(tm_r, D_MODEL)])
            outs_p["hg"].append(st)

            m_s = dbs * ns
            proj = _norm_mm(hs, g_mix, w_in, tm=m_s, tn=1024)
            lpad = HGRN_SUB
            proj = jnp.pad(proj.reshape(dbs, ns, 4 * D_MODEL), ((0, 0), (0, lpad - ns), (0, 0)))
            og, st = _hgrn(proj, lbp, on, state_hgrn[o], chunk=lpad, nchunk=1, layer=l, l_valid=dseq)
            hs = _mm_res([og[:, :ns].reshape(m_s, D_MODEL)], [w_out], hs, tm=m_s, op_specs=[row_spec(m_s, D_MODEL)])
            outs_s["hg"].append(st)

        g_mem = row2(norm_mem[l])
        wq = w_mem_q[l].astype(BF16)
        wo = w_mem_o[l].astype(BF16)
        mqg = row2(mem_q_gain[l])
        wkv_m = jnp.concatenate([w_mem_k[l], w_mem_v[l]], axis=1).astype(BF16)
        mk, mv = _mem_kv(mem_prompt.reshape(bsz * mem_len, D_MODEL), row2(norm_memsrc[l]), wkv_m,
                         row2(mem_k_gain[l]), tm=512)
        mk = mk.reshape(bsz, mem_len, MEM_WIDTH)
        mv = mv.reshape(bsz, mem_len, MEM_WIDTH)
        outs_p["mk"].append(mk.reshape(bsz, mem_len, MEM_HEADS, MEM_HEAD_DIM))
        outs_p["mv"].append(mv.reshape(bsz, mem_len, MEM_HEADS, MEM_HEAD_DIM))
        hp = _mem_attn(hp.reshape(bsz, seq, D_MODEL), g_mem, wq, mqg, mk, mv, wo,
                       nb=1, tl=512).reshape(bsz * seq, D_MODEL)
        pair_shape = (depth, dbs, mem_len // 2, 2 * MEM_HEADS, MEM_HEAD_DIM)
        hs = _mem_attn(hs.reshape(dbs, ns, D_MODEL), g_mem, wq, mqg, cache_mem_k.reshape(pair_shape),
                       cache_mem_v.reshape(pair_shape), wo, nb=8, tl=ns, layer=l).reshape(dbs * ns, D_MODEL)

        g_mlp = row2(norm_mlp[l])
        wu = w_mlp_up[l].astype(BF16)
        wd = w_mlp_down[l].astype(BF16)
        hp = _mlp(hp, g_mlp, wu, wd, tm=1024, tf=1024)
        hs = _mlp(hs, g_mlp, wu, wd, tm=dbs * ns, tf=1024)

    y_p = hp.reshape(bsz, seq, D_MODEL)
    y_s = hs.reshape(dbs, ns, D_MODEL)[:, :dseq]
    return (y_p, y_s,
            jnp.stack(outs_p["lat"], axis=1), jnp.stack(outs_p["kr"], axis=1),
            jnp.stack(outs_p["s5r"]), jnp.stack(outs_p["s5i"]), jnp.stack(outs_p["hg"]),
            jnp.stack(outs_p["mk"]), jnp.stack(outs_p["mv"]),
            jnp.stack(outs_s["lat"], axis=1), jnp.stack(outs_s["kr"], axis=1),
            jnp.stack(outs_s["s5r"]), jnp.stack(outs_s["s5i"]), jnp.stack(outs_s["hg"]))
```

```python
import functools
import math

import jax
import jax.numpy as jnp
from jax import lax
from jax.experimental import pallas as pl
from jax.experimental.pallas import tpu as pltpu

F32 = jnp.float32
BF16 = jnp.bfloat16

LANES = 128
VMEM_LIMIT_BYTES = 56 * 1024 * 1024

D_MODEL = 1024
MLA_HEADS = 8
MLA_NOPE = 64
MLA_ROPE = 32
MLA_QK = MLA_NOPE + MLA_ROPE
MLA_V = 64
MLA_Q_LORA = 768
MLA_KV_LORA = 256
ROPE_THETA = 10000.0
PAGE_SIZE = 128
S5_WIDTH = 512
S5_GROUP = 16
S5_GROUPS = S5_WIDTH // S5_GROUP
S5_STATE = 64
S5_NSTATE = S5_GROUPS * S5_STATE
HGRN_HEADS = 8
HGRN_DK = 128
HGRN_SUB = 16
HGRN_EXP_CLAMP = 80.0
MEM_HEADS = 4
MEM_HEAD_DIM = 128
MEM_WIDTH = MEM_HEADS * MEM_HEAD_DIM
D_FF = 4 * D_MODEL
EPS = 1e-6
HP = MLA_HEADS * LANES


def _cparams(sem):
    return pltpu.CompilerParams(dimension_semantics=sem, vmem_limit_bytes=VMEM_LIMIT_BYTES)


def _rms(x, g):
    return x * lax.rsqrt(jnp.mean(x * x, axis=-1, keepdims=True) + EPS) * g


def _sigmoid(x):
    return 1.0 / (1.0 + jnp.exp(-x))


def _dot(a, b):
    return jnp.dot(a, b, preferred_element_type=F32)


def _dot_nt(a, b):
    return lax.dot_general(a, b, (((1,), (1,)), ((), ())), preferred_element_type=F32)


def _dot_tn(a, b):
    return lax.dot_general(a, b, (((0,), (0,)), ((), ())), preferred_element_type=F32)


def _row_to_col(row, n):
    r = lax.broadcasted_iota(jnp.int32, (n, n), 0)
    c = lax.broadcasted_iota(jnp.int32, (n, n), 1)
    return jnp.sum(jnp.where(r == c, jnp.broadcast_to(row, (n, n)), 0.0), axis=1, keepdims=True)


def _div_pow2(x, d):
    return lax.shift_right_logical(x, int(math.log2(d)))


def _full_spec(shape):
    nd = len(shape)
    return pl.BlockSpec(shape, lambda *_: (0,) * nd)


def row_spec(tm, width):
    return pl.BlockSpec((tm, width), lambda i: (i, 0))


def _even_proj_kernel(h_ref, g_ref, wp_ref, cqn_ref, wuq_ref, qg_ref, ckvn_ref, wkv_ref, kg_ref,
                      cq_ref, s1q_ref, s2q_ref, ck_ref, s1k_ref, s2k_ref,
                      q_out, k_out, v_out, ckv_out, kr_out, u_out, *maybe_qk_out):
    x = h_ref[...]
    hn = _rms(x, g_ref[...]).astype(BF16)
    proj = _dot(hn, wp_ref[...])
    o1 = MLA_Q_LORA
    o2 = o1 + MLA_KV_LORA
    o3 = o2 + LANES
    u_out[...] = proj[:, o3:]
    cq = _rms(proj[:, :o1], cqn_ref[...]).astype(BF16)
    qf = _dot(cq, wuq_ref[...])
    ckv = _rms(proj[:, o1:o2], ckvn_ref[...])
    ckv_out[...] = ckv
    kv = _dot(ckv.astype(BF16), wkv_ref[...])
    kr = proj[:, o2:o3]
    half = MLA_ROPE // 2
    krr = (kr * ck_ref[...] + pltpu.roll(kr, half, 1) * s1k_ref[...]
           + pltpu.roll(kr, LANES - half, 1) * s2k_ref[...])
    kr_out[...] = krr[:, :MLA_ROPE]
    kr_sh = pltpu.roll(krr, MLA_NOPE, 1)
    qg = qg_ref[...]
    kg = kg_ref[...]
    cq_t, s1q_t, s2q_t = cq_ref[...], s1q_ref[...], s2q_ref[...]
    inv_qk = 1.0 / MLA_QK
    for h in range(MLA_HEADS):
        sl = slice(LANES * h, LANES * (h + 1))
        qh = qf[:, sl]
        qh = qh * lax.rsqrt(jnp.sum(qh * qh, axis=-1, keepdims=True) * inv_qk + EPS) * qg
        qh = (qh * cq_t + pltpu.roll(qh, half, 1) * s1q_t + pltpu.roll(qh, LANES - half, 1) * s2q_t)
        q_out[:, sl] = qh.astype(BF16)
        if maybe_qk_out:
            maybe_qk_out[0][:, sl] = (qh * kg).astype(BF16)
        kh = kv[:, sl] + kr_sh
        kh = kh * lax.rsqrt(jnp.sum(kh * kh, axis=-1, keepdims=True) * inv_qk + EPS) * kg
        k_out[:, sl] = kh.astype(BF16)
    v_out[...] = kv[:, HP:].T.astype(BF16)


def _even_proj(h, g, wp, cqn, wuq, qg, ckvn, wkv, kg, tabs, *, tm, u_shape, u_spec, emit_qk):
    m = h.shape[0]
    ltab = tabs[0].shape[0]
    ntab = ltab // tm
    row = lambda i: (i, 0)
    tab_spec = pl.BlockSpec((tm, LANES), lambda i: (i % ntab, 0))
    in_specs = [pl.BlockSpec((tm, D_MODEL), row), _full_spec(g.shape), _full_spec(wp.shape),
                _full_spec(cqn.shape), _full_spec(wuq.shape), _full_spec(qg.shape),
                _full_spec(ckvn.shape), _full_spec(wkv.shape), _full_spec(kg.shape)] + [tab_spec] * 6
    out_shape = [jax.ShapeDtypeStruct((m, HP), BF16), jax.ShapeDtypeStruct((m, HP), BF16),
                 jax.ShapeDtypeStruct((m // tm, HP, tm), BF16), jax.ShapeDtypeStruct((m, MLA_KV_LORA), F32),
                 jax.ShapeDtypeStruct((m, MLA_ROPE), F32), jax.ShapeDtypeStruct(u_shape, F32)]
    out_specs = [pl.BlockSpec((tm, HP), row), pl.BlockSpec((tm, HP), row),
                 pl.BlockSpec((None, HP, tm), lambda i: (i, 0, 0)),
                 pl.BlockSpec((tm, MLA_KV_LORA), row), pl.BlockSpec((tm, MLA_ROPE), row),
                 u_spec]
    if emit_qk:
        out_shape.append(jax.ShapeDtypeStruct((m, HP), BF16))
        out_specs.append(pl.BlockSpec((tm, HP), row))
    return pl.pallas_call(
        _even_proj_kernel, grid=(m // tm,), in_specs=in_specs, out_specs=out_specs, out_shape=out_shape,
        compiler_params=_cparams(("arbitrary",)), name="even_proj",
    )(h, g, wp, cqn, wuq, qg, ckvn, wkv, kg, *tabs)


def _flash_kernel(q_ref, k_ref, vt_ref, o_ref, *, tq, tk, hg):
    i = pl.program_id(1)
    nfull = (i * tq) // tk
    key = lax.broadcasted_iota(jnp.int32, (tk, tq), 0) + nfull * tk
    qry = lax.broadcasted_iota(jnp.int32, (tk, tq), 1) + i * tq
    causal = qry >= key

    heads = [slice(LANES * h, LANES * (h + 1)) for h in range(MLA_HEADS)]

    def update(j, carry, masked):
        off = pl.multiple_of(j * tk, tk)
        out = []
        for h0 in range(0, MLA_HEADS, hg):
            grp = range(h0, h0 + hg)
            scores = [_dot_nt(k_ref[pl.ds(off, tk), heads[h]], q_ref[:, heads[h]]) for h in grp]
            probs, stats = [], []
            for h, s in zip(grp, scores):
                m, l = carry[3 * h], carry[3 * h + 1]
                if masked:
                    s = jnp.where(causal, s, -jnp.inf)
                m_new = jnp.maximum(m, jnp.max(s, axis=0, keepdims=True))
                alpha = jnp.exp(m - m_new)
                p = jnp.exp(s - m_new)
                stats.append((m_new, alpha * l + jnp.sum(p, axis=0, keepdims=True), alpha))
                probs.append(p.astype(BF16))
            pv = [_dot(vt_ref[j, heads[h], :], p) for p, h in zip(probs, grp)]
            for h, (m_new, l_new, alpha), o in zip(grp, stats, pv):
                out += [m_new, l_new, alpha * carry[3 * h + 2] + o]
        return tuple(out)

    init = (jnp.full((1, tq), -jnp.inf, F32), jnp.zeros((1, tq), F32), jnp.zeros((LANES, tq), F32))
    carry = lax.fori_loop(0, nfull, lambda j, c: update(j, c, False), init * MLA_HEADS)
    carry = update(nfull, carry, True)
    for h, sl in enumerate(heads):
        o_ref[:, sl] = (carry[3 * h + 2] / carry[3 * h + 1]).T.astype(BF16)


def _flash_attention(q, k, vt, *, tq, hg=2):
    b, l, _ = q.shape
    tk = vt.shape[2]
    nkb = l // tk
    return pl.pallas_call(
        functools.partial(_flash_kernel, tq=tq, tk=tk, hg=hg),
        grid=(b, l // tq),
        in_specs=[pl.BlockSpec((None, tq, HP), lambda bi, i: (bi, i, 0)),
                  pl.BlockSpec((None, l, HP), lambda bi, i: (bi, 0, 0)),
                  pl.BlockSpec((nkb, HP, tk), lambda bi, i: (bi, 0, 0))],
        out_specs=pl.BlockSpec((None, tq, HP), lambda bi, i: (bi, i, 0)),
        out_shape=jax.ShapeDtypeStruct((b, l, HP), BF16),
        compiler_params=_cparams(("arbitrary", "arbitrary")), name="prompt_attention",
    )(q, k, vt)


def _paged_kernel(pt_ref, *refs, npg, ngrp, nsteps, nq):
    lat_refs = refs[:npg]
    krt_refs = refs[npg:2 * npg]
    (qn_ref, rhs2_ref, wuk_ref, cnew_ref, krnew_ref, wuv_ref,
     o_ref, wabs, m_scr, l_scr, a_scr) = refs[2 * npg:]
    del pt_ref
    s = pl.program_id(1)
    nslots = nsteps * ngrp + 1
    ncol = LANES
    inv_qk = 1.0 / MLA_QK

    @pl.when(s == 0)
    def _():
        wabs[...] = _dot(wuk_ref[...], qn_ref[...]).astype(BF16)

    def stats(blocks, mask):
        kn = [_dot(c, wuk_ref[...]) for c, _ in blocks]
        sq = [k * k for k in kn]
        psum = [q[:, 0:LANES] + q[:, LANES:2 * LANES] + q[:, 2 * LANES:3 * LANES] + q[:, 3 * LANES:] for q in sq]
        r2 = [_dot(jnp.concatenate([p.astype(BF16), x], axis=1), rhs2_ref[...]) for p, (_, x) in zip(psum, blocks)]
        scn = [_dot(c, wabs[...]) for c, _ in blocks]
        probs, out = [], []
        for t in range(len(blocks)):
            sc = (scn[t] + r2[t][:, LANES:]) * lax.rsqrt(r2[t][:, :LANES] * inv_qk + EPS)
            if mask is not None:
                sc = jnp.where(mask, sc, -jnp.inf)
            m = jnp.max(sc, axis=0, keepdims=True)
            p = jnp.exp(sc - m)
            out.append((m, jnp.sum(p, axis=0, keepdims=True)))
            probs.append(p.astype(BF16))
        acc = [_dot_tn(p, c) for p, (c, _) in zip(probs, blocks)]
        return [(m, l, a) for (m, l), a in zip(out, acc)]

    zpad = jnp.zeros((LANES - 2 * MLA_ROPE, PAGE_SIZE), F32)

    def rope_block(g):
        krt = krt_refs[g][...]
        return jnp.concatenate([krt, krt * krt, zpad], axis=0).T.astype(BF16)

    pg = npg // ngrp
    groups = [(jnp.concatenate([lat_refs[g][...].astype(BF16) for g in range(pg * t, pg * (t + 1))], axis=0),
               jnp.concatenate([rope_block(g) for g in range(pg * t, pg * (t + 1))], axis=0))
              for t in range(ngrp)]
    for t, (m, l, a) in enumerate(stats(groups, None)):
        slot = s * ngrp + t
        m_scr[pl.ds(slot, 1), :] = m
        l_scr[pl.ds(slot, 1), :] = l
        a_scr[slot] = a

    @pl.when(s == nsteps - 1)
    def _():
        nnew = cnew_ref.shape[0]
        krn = jnp.concatenate([krnew_ref[...], jnp.zeros((nnew, LANES - MLA_ROPE), F32)], axis=1)
        krn = krn + pltpu.roll(krn * krn, MLA_ROPE, 1)
        key = lax.broadcasted_iota(jnp.int32, (nnew, ncol), 0)
        qry = _div_pow2(lax.broadcasted_iota(jnp.int32, (nnew, ncol), 1), MLA_HEADS)
        (m2, l2, a2), = stats([(cnew_ref[...].astype(BF16), krn.astype(BF16))], key <= qry)
        m_scr[nslots - 1:nslots, :] = m2
        l_scr[nslots - 1:nslots, :] = l2
        a_scr[nslots - 1] = a2
        mall = m_scr[0:nslots, :]
        w = jnp.exp(mall - jnp.max(mall, axis=0, keepdims=True))
        den = jnp.sum(l_scr[0:nslots, :] * w, axis=0, keepdims=True)
        wn = w / den
        num = jnp.zeros((ncol, MLA_KV_LORA), F32)
        for t in range(nslots):
            num = num + a_scr[t] * _row_to_col(wn[t:t + 1, :], ncol)
        full = _dot(num.astype(BF16), wuv_ref[...])
        hrow = lax.broadcasted_iota(jnp.int32, (MLA_HEADS, MLA_HEADS * MLA_V), 0)
        hcol = _div_pow2(lax.broadcasted_iota(jnp.int32, (MLA_HEADS, MLA_HEADS * MLA_V), 1), MLA_V)
        rows = []
        for qi in range(nq):
            blk = full[MLA_HEADS * qi:MLA_HEADS * (qi + 1), :]
            rows.append(jnp.sum(jnp.where(hrow == hcol, blk, 0.0), axis=0, keepdims=True))
        o_ref[...] = jnp.concatenate(rows, axis=0)


def _paged_attention(page_table, cache_lat, cache_krt, e, qn, rhs2, wuk, cnew, krnew, wuv, *, npg, ngrp, nq):
    nb, npages = page_table.shape
    nsteps = npages // npg
    nnew = cnew.shape[1]
    nslots = nsteps * ngrp + 1

    def page_spec(shape, g):
        return pl.BlockSpec((None, None) + shape, lambda b, s, pt: (pt[b, s * npg + g], e, 0, 0))

    per_b3 = lambda b, s, pt: (b, 0, 0)
    const2 = lambda b, s, pt: (0, 0)
    in_specs = ([page_spec((PAGE_SIZE, MLA_KV_LORA), g) for g in range(npg)]
                + [page_spec((MLA_ROPE, PAGE_SIZE), g) for g in range(npg)]
                + [pl.BlockSpec((None,) + qn.shape[1:], per_b3), pl.BlockSpec((None,) + rhs2.shape[1:], per_b3),
                   pl.BlockSpec(wuk.shape, const2),
                   pl.BlockSpec((None, nnew, MLA_KV_LORA), per_b3), pl.BlockSpec((None, nnew, MLA_ROPE), per_b3),
                   pl.BlockSpec(wuv.shape, const2)])
    grid_spec = pltpu.PrefetchScalarGridSpec(
        num_scalar_prefetch=1, grid=(nb, nsteps), in_specs=in_specs,
        out_specs=pl.BlockSpec((None, nq, MLA_HEADS * MLA_V), per_b3),
        scratch_shapes=[pltpu.VMEM((MLA_KV_LORA, LANES), BF16),
                        pltpu.VMEM((nslots, LANES), F32), pltpu.VMEM((nslots, LANES), F32),
                        pltpu.VMEM((nslots, LANES, MLA_KV_LORA), F32)])
    return pl.pallas_call(
        functools.partial(_paged_kernel, npg=npg, ngrp=ngrp, nsteps=nsteps, nq=nq),
        grid_spec=grid_spec, out_shape=jax.ShapeDtypeStruct((nb, nq, MLA_HEADS * MLA_V), F32),
        compiler_params=_cparams(("arbitrary", "arbitrary")), name="paged_attention",
    )(page_table, *([cache_lat] * npg), *([cache_krt] * npg), qn, rhs2, wuk, cnew, krnew, wuv)


def _s5_kernel(u_ref, h0r_ref, h0i_ref, lamr_ref, lami_ref, lstep_ref, brm_ref, bim_ref, crm_ref, cim_ref,
               d_ref, wg_ref, bg_ref, o_ref, hr_out, hi_out, xr_scr, xi_scr, hcr, hci, disc, io_scr, *,
               tt, nb, strip, interleave):
    c = pl.program_id(0)

    @pl.when(c == 0)
    def _():
        lr = jnp.minimum(lamr_ref[...], -1e-4)
        li = lami_ref[...]
        dt = jnp.exp(lstep_ref[...])
        mag = jnp.exp(lr * dt)
        abr = mag * jnp.cos(li * dt)
        abi = mag * jnp.sin(li * dt)
        den = lr * lr + li * li
        disc[0:1, :] = abr
        disc[1:2, :] = abi
        disc[2:3, :] = ((abr - 1.0) * lr + abi * li) / den
        disc[3:4, :] = (abi * lr - (abr - 1.0) * li) / den
        hcr[...] = h0r_ref[...]
        hci[...] = h0i_ref[...]

    nlb = S5_WIDTH // LANES
    if interleave:
        for b in range(nb):
            for j in range(nlb):
                c0 = S5_WIDTH * b + LANES * j
                io_scr[j, pl.ds(b, tt, stride=nb), :] = u_ref[:, c0:c0 + LANES]
        u = jnp.concatenate([io_scr[j] for j in range(nlb)], axis=1)
    else:
        u = u_ref[...]
    ub = u.astype(BF16)
    kc = 2 * LANES
    ks = kc * S5_STATE // S5_GROUP
    for k in range(S5_WIDTH // kc):
        cols = slice(kc * k, kc * (k + 1))
        sts = slice(ks * k, ks * (k + 1))
        pr = _dot(ub[:, cols], brm_ref[cols, sts])
        pi = _dot(ub[:, cols], bim_ref[cols, sts])
        cor = disc[2:3, sts]
        coi = disc[3:4, sts]
        xr_scr[:, sts] = cor * pr - coi * pi
        xi_scr[:, sts] = cor * pi + coi * pr

    for s0 in range(0, S5_NSTATE, strip):
        lanes = slice(s0, s0 + strip)
        ar = jnp.broadcast_to(disc[0:1, lanes], (nb, strip))
        ai = jnp.broadcast_to(disc[1:2, lanes], (nb, strip))

        def step(t, carry, lanes=lanes, ar=ar, ai=ai):
            hr, hi = carry
            rows = pl.ds(pl.multiple_of(t * nb, nb), nb)
            nr = ar * hr - ai * hi + xr_scr[rows, lanes]
            ni = ar * hi + ai * hr + xi_scr[rows, lanes]
            xr_scr[rows, lanes] = nr
            xi_scr[rows, lanes] = ni
            return nr, ni

        hr, hi = lax.fori_loop(0, tt, step, (hcr[:, lanes], hci[:, lanes]))
        hcr[:, lanes] = hr
        hci[:, lanes] = hi

    ys = []
    for k in range(S5_WIDTH // kc):
        cols = slice(kc * k, kc * (k + 1))
        sts = slice(ks * k, ks * (k + 1))
        ys.append(_dot(xr_scr[:, sts].astype(BF16), crm_ref[sts, cols])
                  - _dot(xi_scr[:, sts].astype(BF16), cim_ref[sts, cols]))
    y = jnp.concatenate(ys, axis=1) + d_ref[...] * u
    z = jax.nn.gelu(y)
    gate = _sigmoid(_dot(z.astype(BF16), wg_ref[...]) + bg_ref[...])
    if interleave:
        out = z * gate
        for j in range(nlb):
            io_scr[j] = out[:, LANES * j:LANES * (j + 1)]
        for b in range(nb):
            for j in range(nlb):
                c0 = S5_WIDTH * b + LANES * j
                o_ref[:, c0:c0 + LANES] = io_scr[j, pl.ds(b, tt, stride=nb), :].astype(BF16)
    else:
        o_ref[...] = (z * gate).astype(BF16)

    @pl.when(c == pl.num_programs(0) - 1)
    def _():
        hr_out[...] = hcr[...]
        hi_out[...] = hci[...]


def _s5(u, h0r, h0i, lamr, lami, lstep, brm, bim, crm, cim, d, wg, bg, *, tt, nb, strip, interleave):
    steps = u.shape[0] if interleave else u.shape[0] // nb
    blk = tt * nb
    consts = (h0r, h0i, lamr, lami, lstep, brm, bim, crm, cim, d, wg, bg)
    io_spec = row_spec(tt, nb * S5_WIDTH) if interleave else row_spec(blk, S5_WIDTH)
    return pl.pallas_call(
        functools.partial(_s5_kernel, tt=tt, nb=nb, strip=strip, interleave=interleave),
        grid=(steps // tt,),
        in_specs=[io_spec] + [_full_spec(a.shape) for a in consts],
        out_specs=[io_spec, _full_spec((nb, S5_NSTATE)), _full_spec((nb, S5_NSTATE))],
        out_shape=[jax.ShapeDtypeStruct(u.shape, BF16),
                   jax.ShapeDtypeStruct((nb, S5_NSTATE), F32), jax.ShapeDtypeStruct((nb, S5_NSTATE), F32)],
        scratch_shapes=[pltpu.VMEM((blk, S5_NSTATE), F32), pltpu.VMEM((blk, S5_NSTATE), F32),
                        pltpu.VMEM((nb, S5_NSTATE), F32), pltpu.VMEM((nb, S5_NSTATE), F32),
                        pltpu.VMEM((8, S5_NSTATE), F32), pltpu.VMEM((S5_WIDTH // LANES, blk, LANES), F32)],
        compiler_params=_cparams(("arbitrary",)), name="s5",
    )(u, *consts)


def _mm_res_kernel(*refs, nop):
    res_ref = refs[2 * nop]
    o_ref = refs[2 * nop + 1]
    acc = res_ref[...]
    for t in range(nop):
        acc = acc + _dot(refs[t][...].astype(BF16), refs[nop + t][...])
    o_ref[...] = acc


def _mm_res(ops, ws, res, *, tm, op_specs):
    m, n = res.shape
    row = lambda i: (i, 0)
    in_specs = list(op_specs) + [_full_spec(w.shape) for w in ws] + [pl.BlockSpec((tm, n), row)]
    return pl.pallas_call(
        functools.partial(_mm_res_kernel, nop=len(ops)), grid=(m // tm,), in_specs=in_specs,
        out_specs=pl.BlockSpec((tm, n), row), out_shape=jax.ShapeDtypeStruct((m, n), F32),
        compiler_params=_cparams(("arbitrary",)), name="matmul_residual",
    )(*ops, *ws, res)


def _mem_kv_kernel(x_ref, g_ref, w_ref, kg_ref, k_out, v_out):
    mn = _rms(x_ref[...], g_ref[...]).astype(BF16)
    kv = _dot(mn, w_ref[...])
    kg = kg_ref[...]
    for h in range(MEM_HEADS):
        sl = slice(LANES * h, LANES * (h + 1))
        k_out[:, sl] = _rms(kv[:, sl], kg)
    v_out[...] = kv[:, MEM_WIDTH:]


def _mem_kv(x, g, w, kg, *, tm):
    m = x.shape[0]
    row = lambda i: (i, 0)
    return pl.pallas_call(
        _mem_kv_kernel, grid=(m // tm,),
        in_specs=[pl.BlockSpec((tm, D_MODEL), row), _full_spec(g.shape), _full_spec(w.shape), _full_spec(kg.shape)],
        out_specs=[pl.BlockSpec((tm, MEM_WIDTH), row), pl.BlockSpec((tm, MEM_WIDTH), row)],
        out_shape=[jax.ShapeDtypeStruct((m, MEM_WIDTH), F32), jax.ShapeDtypeStruct((m, MEM_WIDTH), F32)],
        compiler_params=_cparams(("arbitrary",)), name="mem_kv",
    )(x, g, w, kg)


def _mem_attn_kernel(h_ref, g_ref, wq_ref, qg_ref, mk_ref, mv_ref, wo_ref, o_ref, *, nb, tl, paired):
    x = h_ref[...].reshape(nb * tl, D_MODEL)
    hn = _rms(x, g_ref[...]).astype(BF16)
    q = _dot(hn, wq_ref[...])
    qg = qg_ref[...] * (MEM_HEAD_DIM ** -0.5)
    qn = [_rms(q[:, LANES * h:LANES * (h + 1)], qg).astype(BF16) for h in range(MEM_HEADS)]

    def head_block(ref, b, h):
        if paired:
            return jnp.concatenate([ref[b, :, h, :], ref[b, :, MEM_HEADS + h, :]], axis=0).astype(BF16)
        return ref[b, :, LANES * h:LANES * (h + 1)].astype(BF16)

    rows = []
    for b in range(nb):
        r = slice(tl * b, tl * (b + 1))
        outs = []
        for h in range(MEM_HEADS):
            s = _dot_nt(qn[h][r, :], head_block(mk_ref, b, h))
            p = jnp.exp(s - jnp.max(s, axis=-1, keepdims=True))
            outs.append(_dot(p.astype(BF16), head_block(mv_ref, b, h)) / jnp.sum(p, axis=-1, keepdims=True))
        rows.append(jnp.concatenate(outs, axis=1))
    o = (rows[0] if nb == 1 else jnp.concatenate(rows, axis=0)).astype(BF16)
    o_ref[...] = (x + _dot(o, wo_ref[...])).reshape(nb, tl, D_MODEL)


def _mem_attn(h, g, wq, qg, mk, mv, wo, *, nb, tl, layer=None):
    b, l, _ = h.shape
    blk = lambda bi, i: (bi, i, 0)
    if layer is None:
        mem_spec = pl.BlockSpec((nb,) + mk.shape[1:], lambda bi, i: (bi, 0, 0))
    else:
        mem_spec = pl.BlockSpec((None, nb) + mk.shape[2:], lambda bi, i: (layer, bi, 0, 0, 0))
    return pl.pallas_call(
        functools.partial(_mem_attn_kernel, nb=nb, tl=tl, paired=layer is not None), grid=(b // nb, l // tl),
        in_specs=[pl.BlockSpec((nb, tl, D_MODEL), blk), _full_spec(g.shape), _full_spec(wq.shape),
                  _full_spec(qg.shape), mem_spec, mem_spec, _full_spec(wo.shape)],
        out_specs=pl.BlockSpec((nb, tl, D_MODEL), blk),
        out_shape=jax.ShapeDtypeStruct(h.shape, F32),
        compiler_params=_cparams(("arbitrary", "arbitrary")), name="mem_attention",
    )(h, g, wq, qg, mk, mv, wo)


def _mlp_kernel(h_ref, g_ref, wu_ref, wd_ref, o_ref, xn_scr, acc_scr):
    j = pl.program_id(1)

    @pl.when(j == 0)
    def _():
        xn_scr[...] = _rms(h_ref[...], g_ref[...]).astype(BF16)
        acc_scr[...] = jnp.zeros(acc_scr.shape, F32)

    a = _dot(xn_scr[...], wu_ref[...])
    a = jnp.square(jnp.maximum(a, 0.0)).astype(BF16)
    acc_scr[...] += _dot(a, wd_ref[...])

    @pl.when(j == pl.num_programs(1) - 1)
    def _():
        o_ref[...] = h_ref[...] + acc_scr[...]


def _mlp(h, g, wu, wd, *, tm, tf):
    m = h.shape[0]
    return pl.pallas_call(
        _mlp_kernel, grid=(m // tm, D_FF // tf),
        in_specs=[pl.BlockSpec((tm, D_MODEL), lambda i, j: (i, 0)), _full_spec(g.shape),
                  pl.BlockSpec((D_MODEL, tf), lambda i, j: (0, j)), pl.BlockSpec((tf, D_MODEL), lambda i, j: (j, 0))],
        out_specs=pl.BlockSpec((tm, D_MODEL), lambda i, j: (i, 0)),
        out_shape=jax.ShapeDtypeStruct((m, D_MODEL), F32),
        scratch_shapes=[pltpu.VMEM((tm, D_MODEL), BF16), pltpu.VMEM((tm, D_MODEL), F32)],
        compiler_params=_cparams(("arbitrary", "arbitrary")), name="mlp",
    )(h, g, wu, wd)


def _norm_mm_kernel(h_ref, g_ref, w_ref, o_ref, xn_scr):
    @pl.when(pl.program_id(1) == 0)
    def _():
        xn_scr[...] = _rms(h_ref[...], g_ref[...]).astype(BF16)

    o_ref[...] = _dot(xn_scr[...], w_ref[...])


def _norm_mm(h, g, w, *, tm, tn):
    m = h.shape[0]
    n = w.shape[1]
    return pl.pallas_call(
        _norm_mm_kernel, grid=(m // tm, n // tn),
        in_specs=[pl.BlockSpec((tm, D_MODEL), lambda i, j: (i, 0)), _full_spec(g.shape),
                  pl.BlockSpec((D_MODEL, tn), lambda i, j: (0, j))],
        out_specs=pl.BlockSpec((tm, tn), lambda i, j: (i, j)),
        out_shape=jax.ShapeDtypeStruct((m, n), F32),
        scratch_shapes=[pltpu.VMEM((tm, D_MODEL), BF16)],
        compiler_params=_cparams(("arbitrary", "arbitrary")), name="norm_matmul",
    )(h, g, w)


def _hgrn_kernel(q_ref, f_ref, i_ref, g_ref, lbp_ref, on_ref, s0_ref, o_ref, s_out, s_scr, *,
                 chunk, nchunk, layer, l_valid):
    c = pl.program_id(1)
    tb = chunk * nchunk

    @pl.when(c == 0)
    def _():
        s_scr[...] = s0_ref[...]

    lbp = lbp_ref[...]
    e = jnp.exp(lbp - jnp.max(lbp, axis=0, keepdims=True))
    sm = e / jnp.sum(e, axis=0, keepdims=True)
    lb = jnp.sum(sm[0:layer + 1, :], axis=0, keepdims=True) - sm[0:1, :]

    q = q_ref[...]
    qa = q * _sigmoid(q)
    fg = lb + (1.0 - lb) * _sigmoid(f_ref[...])
    logf = jnp.log(fg)
    kk = 1.0 - fg
    v = i_ref[...]
    if l_valid is not None:
        valid = (lax.broadcasted_iota(jnp.int32, (tb, 1), 0) + c * tb) < l_valid
        logf = jnp.where(valid, logf, 0.0)
        kk = jnp.where(valid, kk, 0.0)
    vb = v.astype(BF16)

    tr = lax.broadcasted_iota(jnp.int32, (tb, tb), 0)
    tc = lax.broadcasted_iota(jnp.int32, (tb, tb), 1)
    same_chunk = _div_pow2(tr, chunk) == _div_pow2(tc, chunk)
    tri = jnp.where(same_chunk, jnp.where(tr >= tc, 1.0, 0.0), 0.0).astype(BF16)
    hi = logf.astype(BF16)
    lo = (logf - hi.astype(F32)).astype(BF16)
    bcum = _dot(tri, hi) + _dot(tri, lo)
    qhat = (qa * jnp.exp(bcum)).astype(BF16)

    nsub = chunk // HGRN_SUB
    khat, dec, qloc, kloc, masks = [], [], [], [], []
    spread = jnp.zeros((1, bcum.shape[1]), F32)
    for ci in range(nchunk):
        c0 = ci * chunk
        blast = bcum[c0 + chunk - 1:c0 + chunk, :]
        khat.append((kk[c0:c0 + chunk, :] * jnp.exp(blast - bcum[c0:c0 + chunk, :])).astype(BF16))
        dec.append(jnp.exp(blast))
        for i in range(nsub):
            r0 = c0 + i * HGRN_SUB
            r1 = r0 + HGRN_SUB
            base = bcum[r0 - 1:r0, :] if i > 0 else jnp.zeros((1, bcum.shape[1]), F32)
            spread = jnp.minimum(spread, bcum[r1 - 1:r1, :] - base)
            qloc.append((qa[r0:r1, :] * jnp.exp(bcum[r0:r1, :] - base)).astype(BF16))
            kloc.append((kk[c0:r1, :] * jnp.exp(jnp.minimum(base - bcum[c0:r1, :], HGRN_EXP_CLAMP))).astype(BF16))
    wild = jnp.min(spread) < -HGRN_EXP_CLAMP
    for i in range(nsub):
        ncols = (i + 1) * HGRN_SUB
        ar = lax.broadcasted_iota(jnp.int32, (HGRN_SUB, ncols), 0) + i * HGRN_SUB
        ac = lax.broadcasted_iota(jnp.int32, (HGRN_SUB, ncols), 1)
        in_block = ac >= i * HGRN_SUB
        masks.append(jnp.logical_and(ar >= ac, jnp.logical_not(jnp.logical_and(wild, in_block))))

    hsl = [slice(HGRN_DK * h, HGRN_DK * (h + 1)) for h in range(HGRN_HEADS)]

    def in_block_exact():
        pos = jnp.bitwise_and(lax.broadcasted_iota(jnp.int32, (tb, 1), 0), HGRN_SUB - 1)
        out = jnp.zeros((tb, bcum.shape[1]), F32)
        for j in range(HGRN_SUB):
            ok = pos >= j
            kj, bj, vj = (kk, bcum, v) if j == 0 else (pltpu.roll(x, j, 0) for x in (kk, bcum, v))
            e = jnp.where(ok, qa * kj * jnp.exp(jnp.where(ok, bcum - bj, 0.0)), 0.0)
            out = out + jnp.concatenate(
                [jnp.sum(e[:, sl], axis=-1, keepdims=True) * vj[:, sl] for sl in hsl], axis=1)
        return out

    blocks = [(ci, i) for ci in range(nchunk) for i in range(nsub)]
    att = [[_dot_nt(qloc[ci * nsub + i][:, sl], kloc[ci * nsub + i][:, sl]) for ci, i in blocks] for sl in hsl]
    att = [[jnp.where(masks[i], a, 0.0).astype(BF16) for a, (ci, i) in zip(row, blocks)] for row in att]
    intra = [[_dot(a, vb[ci * chunk:ci * chunk + (i + 1) * HGRN_SUB, sl]) for a, (ci, i) in zip(row, blocks)]
             for row, sl in zip(att, hsl)]
    kv = [[_dot_tn(khat[ci][:, sl], vb[ci * chunk:(ci + 1) * chunk, sl]) for ci in range(nchunk)] for sl in hsl]
    dcol = [[_row_to_col(dec[ci][:, sl], HGRN_DK) for ci in range(nchunk)] for sl in hsl]
    st = [s_scr[h] for h in range(HGRN_HEADS)]
    inter = [[] for _ in hsl]
    for ci in range(nchunk):
        rows = slice(ci * chunk, (ci + 1) * chunk)
        for h, sl in enumerate(hsl):
            inter[h].append(_dot(qhat[rows, sl], st[h].astype(BF16)))
        for h in range(HGRN_HEADS):
            st[h] = dcol[h][ci] * st[h] + kv[h][ci]
    o_heads = []
    for h in range(HGRN_HEADS):
        s_scr[h] = st[h]
        parts = [inter[h][ci][i * HGRN_SUB:(i + 1) * HGRN_SUB, :] + intra[h][ci * nsub + i] for ci, i in blocks]
        o_heads.append(parts[0] if len(parts) == 1 else jnp.concatenate(parts, axis=0))

    o = jnp.concatenate(o_heads, axis=1)
    o = lax.cond(wild, lambda: o + in_block_exact(), lambda: o)
    g = g_ref[...]
    o_ref[...] = (_rms(o, on_ref[...]) * (g * _sigmoid(g))).astype(BF16)

    @pl.when(c == pl.num_programs(1) - 1)
    def _():
        s_out[...] = s_scr[...]


def _hgrn(proj, lbp, on, s0, *, chunk, nchunk, layer, l_valid):
    b, l, _ = proj.shape
    w = D_MODEL
    tb = chunk * nchunk

    def col(k):
        return pl.BlockSpec((None, tb, w), lambda bi, c: (bi, c, k))

    st_spec = pl.BlockSpec((None, HGRN_HEADS, HGRN_DK, HGRN_DK), lambda bi, c: (bi, 0, 0, 0))
    return pl.pallas_call(
        functools.partial(_hgrn_kernel, chunk=chunk, nchunk=nchunk, layer=layer, l_valid=l_valid),
        grid=(b, l // tb),
        in_specs=[col(0), col(1), col(2), col(3), _full_spec(lbp.shape), _full_spec(on.shape), st_spec],
        out_specs=[pl.BlockSpec((None, tb, w), lambda bi, c: (bi, c, 0)), st_spec],
        out_shape=[jax.ShapeDtypeStruct((b, l, w), BF16), jax.ShapeDtypeStruct(s0.shape, F32)],
        scratch_shapes=[pltpu.VMEM((HGRN_HEADS, HGRN_DK, HGRN_DK), F32)],
        compiler_params=_cparams(("arbitrary", "arbitrary")), name="hgrn",
    )(proj, proj, proj, proj, lbp, on, s0)


def _pad_last(x, n):
    return jnp.pad(x, [(0, 0)] * (x.ndim - 1) + [(0, n - x.shape[-1])])


def _head_pad(w, per):
    k = w.shape[0]
    return _pad_last(w.reshape(k, -1, per), LANES).reshape(k, -1)


def _rope_tables(pos):
    half = MLA_ROPE // 2
    inv = ROPE_THETA ** (-jnp.arange(half, dtype=F32) / half)
    ang = pos.astype(F32)[:, None] * inv[None, :]
    cos, sin = jnp.cos(ang), jnp.sin(ang)
    n = pos.shape[0]
    z = lambda w: jnp.zeros((n, w), F32)
    scale = MLA_QK ** -0.5
    cq = scale * jnp.concatenate([jnp.ones((n, MLA_NOPE), F32), cos, cos, z(LANES - MLA_QK)], axis=1)
    s1q = scale * jnp.concatenate([z(MLA_NOPE + half), sin, z(LANES - MLA_QK)], axis=1)
    s2q = scale * jnp.concatenate([z(MLA_NOPE), -sin, z(half + LANES - MLA_QK)], axis=1)
    ck = jnp.concatenate([cos, cos, z(LANES - MLA_ROPE)], axis=1)
    s1k = jnp.concatenate([z(half), sin, z(LANES - MLA_ROPE)], axis=1)
    s2k = jnp.concatenate([-sin, z(LANES - half)], axis=1)
    return (cq, s1q, s2q, ck, s1k, s2k)


def _block_diag(x):
    g, a, b = x.shape
    eye = jnp.eye(g, dtype=x.dtype)
    return (x[:, :, None, :] * eye[:, None, :, None]).reshape(g * a, g * b)


def kernel(x_prompt, x_sample, cache_mla_latent, cache_mla_krope, state_s5_re, state_s5_im, state_hgrn, cache_mem_k, cache_mem_v, page_table, mem_prompt, norm_mix, norm_mem, norm_memsrc, norm_mlp, w_mem_q, w_mem_k, w_mem_v, w_mem_o, mem_q_gain, mem_k_gain, w_mlp_up, w_mlp_down, w_in_even, mla_cq_norm, mla_ckv_norm, w_mla_uq, w_mla_ukv, mla_qn_nope, mla_qn_rope, mla_kn_nope, mla_kn_rope, s5_lambda_re, s5_lambda_im, s5_log_step, s5_b_re, s5_b_im, s5_c_re, s5_c_im, s5_d, s5_w_glu, s5_b_glu, w_out_even, w_in_odd, hgrn_lower_bounds, hgrn_out_norm, w_out_odd):
    bsz, seq, _ = x_prompt.shape
    dbs, dseq, _ = x_sample.shape
    depth = norm_mix.shape[0]
    past_len = page_table.shape[1] * PAGE_SIZE
    ns = 8
    mem_len = mem_prompt.shape[1]
    row2 = lambda a: a.reshape(1, -1).astype(F32)

    hp = x_prompt.reshape(bsz * seq, D_MODEL)
    hs = jnp.pad(x_sample, ((0, 0), (0, ns - dseq), (0, 0))).reshape(dbs * ns, D_MODEL)

    tabs_p = _rope_tables(jnp.arange(seq, dtype=jnp.int32))
    pos_s = past_len + jnp.arange(ns, dtype=jnp.int32)
    tabs_s = tuple(jnp.tile(t, (dbs, 1)) for t in _rope_tables(pos_s))

    outs_p = {k: [] for k in ("lat", "kr", "s5r", "s5i", "hg", "mk", "mv")}
    outs_s = {k: [] for k in ("lat", "kr", "s5r", "s5i", "hg")}

    tm_p = 512
    nl_p = seq // tm_p
    tm_r = 1024
    nl_r = seq // tm_r

    for l in range(depth):
        if l % 2 == 0:
            e = l // 2
            w_in = w_in_even[e]
            o1 = MLA_Q_LORA + MLA_KV_LORA
            wp = jnp.concatenate([w_in[:, :o1], _pad_last(w_in[:, o1:o1 + MLA_ROPE], LANES),
                                  w_in[:, o1 + MLA_ROPE:]], axis=1).astype(BF16)
            wuq = _head_pad(w_mla_uq[e], MLA_QK).astype(BF16)
            ukv = w_mla_ukv[e].reshape(MLA_KV_LORA, MLA_HEADS, MLA_NOPE + MLA_V)
            wuk_c = ukv[:, :, :MLA_NOPE].reshape(MLA_KV_LORA, -1)
            wuv_c = ukv[:, :, MLA_NOPE:].reshape(MLA_KV_LORA, -1)
            wkv = jnp.concatenate([_head_pad(wuk_c, MLA_NOPE), _head_pad(wuv_c, MLA_V)], axis=1).astype(BF16)
            qg = _pad_last(jnp.concatenate([mla_qn_nope[e], mla_qn_rope[e], mla_qn_rope[e]])[None, :], LANES)
            kg = _pad_last(jnp.concatenate([mla_kn_nope[e], mla_kn_rope[e], mla_kn_rope[e]])[None, :], LANES)
            cqn = row2(mla_cq_norm[e])
            ckvn = row2(mla_ckv_norm[e])
            g_mix = row2(norm_mix[l])

            brm = _block_diag(jnp.swapaxes(s5_b_re[e], 1, 2)).astype(BF16)
            bim = _block_diag(jnp.swapaxes(s5_b_im[e], 1, 2)).astype(BF16)
            crm = _block_diag(jnp.swapaxes(s5_c_re[e], 1, 2)).astype(BF16)
            cim = _block_diag(jnp.swapaxes(s5_c_im[e], 1, 2)).astype(BF16)
            lamr = row2(s5_lambda_re[e])
            lami = row2(s5_lambda_im[e])
            lstep = row2(jnp.repeat(s5_log_step[e], S5_STATE))
            s5_consts = (lamr, lami, lstep, brm, bim, crm, cim, row2(s5_d[e]), s5_w_glu[e].astype(BF16),
                         row2(s5_b_glu[e]))
            w_out = w_out_even[e]
            wo_att_c = w_out[:MLA_HEADS * MLA_V].astype(BF16)
            wo_att_p = _pad_last(w_out[:MLA_HEADS * MLA_V].reshape(MLA_HEADS, MLA_V, D_MODEL).swapaxes(1, 2),
                                 LANES).swapaxes(1, 2).reshape(HP, D_MODEL).astype(BF16)
            wo_s5 = w_out[MLA_HEADS * MLA_V:].astype(BF16)

            q, k, v, ckv, kr, u = _even_proj(
                hp, g_mix, wp, cqn, wuq, qg, ckvn, wkv, kg, tabs_p, tm=tm_p,
                u_shape=(seq, bsz * S5_WIDTH), emit_qk=False,
                u_spec=pl.BlockSpec((tm_p, S5_WIDTH), lambda i: (i % nl_p, i // nl_p)))
            o_att = _flash_attention(q.reshape(bsz, seq, HP), k.reshape(bsz, seq, HP), v, tq=512, hg=4)
            z0 = jnp.zeros((bsz, S5_NSTATE), F32)
            o_s5, hr, hi = _s5(u, z0, z0, *s5_consts, tt=64, nb=bsz, strip=512, interleave=True)
            hp = _mm_res([o_att.reshape(bsz * seq, HP), o_s5], [wo_att_p, wo_s5], hp,
                         tm=tm_r, op_specs=[row_spec(tm_r, HP),
                                            pl.BlockSpec((tm_r, S5_WIDTH), lambda i: (i % nl_r, i // nl_r))])
            outs_p["lat"].append(ckv.reshape(bsz, seq, MLA_KV_LORA))
            outs_p["kr"].append(kr.reshape(bsz, seq, MLA_ROPE))
            outs_p["s5r"].append(hr.reshape(bsz, S5_GROUPS, S5_STATE))
            outs_p["s5i"].append(hi.reshape(bsz, S5_GROUPS, S5_STATE))

            m_s = dbs * ns
            q, k, v, ckv, kr, u, qk = _even_proj(
                hs, g_mix, wp, cqn, wuq, qg, ckvn, wkv, kg, tabs_s, tm=512,
                u_shape=(m_s, S5_WIDTH), u_spec=row_spec(512, S5_WIDTH), emit_qk=True)
            del q, k, v
            ckv3 = ckv.reshape(dbs, ns, MLA_KV_LORA)
            kr3 = kr.reshape(dbs, ns, MLA_ROPE)
            qk4 = qk.reshape(dbs, ns, MLA_HEADS, LANES)
            eye_h = jnp.eye(MLA_HEADS, dtype=BF16)
            ncols = ns * MLA_HEADS
            sub = LANES // MLA_HEADS
            nblk = MLA_NOPE // sub
            qn = (jnp.transpose(qk4[..., :MLA_NOPE], (0, 2, 3, 1))[..., None]
                  * eye_h[None, :, None, None, :])
            qn = qn.reshape(dbs, MLA_HEADS, nblk, sub, ncols).swapaxes(1, 2).reshape(dbs, MLA_HEADS * MLA_NOPE, ncols)
            qn = _pad_last(qn, LANES)
            wuk_p = (wuk_c.reshape(MLA_KV_LORA, MLA_HEADS, nblk, sub).swapaxes(1, 2)
                     .reshape(MLA_KV_LORA, MLA_HEADS * MLA_NOPE).astype(BF16))
            qr = jnp.transpose(qk4[..., MLA_NOPE:MLA_QK], (0, 3, 1, 2)).reshape(dbs, MLA_ROPE, ncols)
            qr = _pad_last(qr, LANES)
            colmask = (jnp.arange(LANES) < ncols)
            e16 = ((jnp.arange(LANES)[:, None] // sub == (jnp.arange(LANES)[None, :] % MLA_HEADS))
                   & colmask[None, :]).astype(BF16)
            onr = jnp.broadcast_to(colmask[None, :], (MLA_ROPE, LANES)).astype(BF16)
            zb = lambda r: jnp.zeros((dbs, r, LANES), BF16)
            bc = lambda x: jnp.broadcast_to(x[None], (dbs,) + x.shape)
            rhs2 = jnp.concatenate([
                jnp.concatenate([bc(e16), zb(LANES)], axis=2),
                jnp.concatenate([zb(MLA_ROPE), qr], axis=2),
                jnp.concatenate([bc(onr), zb(MLA_ROPE)], axis=2),
                jnp.zeros((dbs, LANES - 2 * MLA_ROPE, 2 * LANES), BF16)], axis=1)
            nnew = 16
            cnew = jnp.pad(ckv3, ((0, 0), (0, nnew - ns), (0, 0)))
            krnew = jnp.pad(kr3, ((0, 0), (0, nnew - ns), (0, 0)))
            o_att_s = _paged_attention(page_table, cache_mla_latent, jnp.swapaxes(cache_mla_krope, 2, 3), e,
                                       qn, rhs2, wuk_p, cnew, krnew, wuv_c.astype(BF16), npg=32, ngrp=8, nq=ns)
            u_tb = jnp.transpose(u.reshape(dbs, ns, S5_WIDTH)[:, :dseq], (1, 0, 2)).reshape(dseq * dbs, S5_WIDTH)
            o_s5, hr, hi = _s5(u_tb, state_s5_re[e].reshape(dbs, S5_NSTATE), state_s5_im[e].reshape(dbs, S5_NSTATE),
                               *s5_consts, tt=dseq, nb=dbs, strip=512, interleave=False)
            o_s5 = jnp.transpose(o_s5.reshape(dseq, dbs, S5_WIDTH), (1, 0, 2))
            o_s5 = jnp.pad(o_s5, ((0, 0), (0, ns - dseq), (0, 0))).reshape(m_s, S5_WIDTH)
            hs = _mm_res([o_att_s.reshape(m_s, MLA_HEADS * MLA_V), o_s5], [wo_att_c, wo_s5], hs, tm=m_s,
                         op_specs=[row_spec(m_s, MLA_HEADS * MLA_V), row_spec(m_s, S5_WIDTH)])
            outs_s["lat"].append(ckv3[:, :dseq])
            outs_s["kr"].append(kr3[:, :dseq])
            outs_s["s5r"].append(hr.reshape(dbs, S5_GROUPS, S5_STATE))
            outs_s["s5i"].append(hi.reshape(dbs, S5_GROUPS, S5_STATE))
        else:
            o = l // 2
            g_mix = row2(norm_mix[l])
            w_in = w_in_odd[o].astype(BF16)
            w_out = w_out_odd[o].astype(BF16)
            on = row2(hgrn_out_norm[o])
            lbp = hgrn_lower_bounds.astype(F32)

            proj = _norm_mm(hp, g_mix, w_in, tm=1024, tn=2048)
            s_zero = jnp.zeros((bsz, HGRN_HEADS, HGRN_DK, HGRN_DK), F32)
            og, st = _hgrn(proj.reshape(bsz, seq, 4 * D_MODEL), lbp, on, s_zero, chunk=64, nchunk=4, layer=l,
                           l_valid=None)
            hp = _mm_res([og.reshape(bsz * seq, D_MODEL)], [w_out], hp, tm=tm_r, op_specs=[row_spec(tm_r, D_MODEL)])
            outs_p["hg"].append(st)

            m_s = dbs * ns
            proj = _norm_mm(hs, g_mix, w_in, tm=m_s, tn=1024)
            lpad = HGRN_SUB
            proj = jnp.pad(proj.reshape(dbs, ns, 4 * D_MODEL), ((0, 0), (0, lpad - ns), (0, 0)))
            og, st = _hgrn(proj, lbp, on, state_hgrn[o], chunk=lpad, nchunk=1, layer=l, l_valid=dseq)
            hs = _mm_res([og[:, :ns].reshape(m_s, D_MODEL)], [w_out], hs, tm=m_s, op_specs=[row_spec(m_s, D_MODEL)])
            outs_s["hg"].append(st)

        g_mem = row2(norm_mem[l])
        wq = w_mem_q[l].astype(BF16)
        wo = w_mem_o[l].astype(BF16)
        mqg = row2(mem_q_gain[l])
        wkv_m = jnp.concatenate([w_mem_k[l], w_mem_v[l]], axis=1).astype(BF16)
        mk, mv = _mem_kv(mem_prompt.reshape(bsz * mem_len, D_MODEL), row2(norm_memsrc[l]), wkv_m,
                         row2(mem_k_gain[l]), tm=512)
        mk = mk.reshape(bsz, mem_len, MEM_WIDTH)
        mv = mv.reshape(bsz, mem_len, MEM_WIDTH)
        outs_p["mk"].append(mk.reshape(bsz, mem_len, MEM_HEADS, MEM_HEAD_DIM))
        outs_p["mv"].append(mv.reshape(bsz, mem_len, MEM_HEADS, MEM_HEAD_DIM))
        hp = _mem_attn(hp.reshape(bsz, seq, D_MODEL), g_mem, wq, mqg, mk, mv, wo,
                       nb=1, tl=512).reshape(bsz * seq, D_MODEL)
        pair_shape = (depth, dbs, mem_len // 2, 2 * MEM_HEADS, MEM_HEAD_DIM)
        hs = _mem_attn(hs.reshape(dbs, ns, D_MODEL), g_mem, wq, mqg, cache_mem_k.reshape(pair_shape),
                       cache_mem_v.reshape(pair_shape), wo, nb=8, tl=ns, layer=l).reshape(dbs * ns, D_MODEL)

        g_mlp = row2(norm_mlp[l])
        wu = w_mlp_up[l].astype(BF16)
        wd = w_mlp_down[l].astype(BF16)
        hp = _mlp(hp, g_mlp, wu, wd, tm=1024, tf=2048)
        hs = _mlp(hs, g_mlp, wu, wd, tm=dbs * ns, tf=1024)

    y_p = hp.reshape(bsz, seq, D_MODEL)
    y_s = hs.reshape(dbs, ns, D_MODEL)[:, :dseq]
    return (y_p, y_s,
            jnp.stack(outs_p["lat"], axis=1), jnp.stack(outs_p["kr"], axis=1),
            jnp.stack(outs_p["s5r"]), jnp.stack(outs_p["s5i"]), jnp.stack(outs_p["hg"]),
            jnp.stack(outs_p["mk"]), jnp.stack(outs_p["mv"]),
            jnp.stack(outs_s["lat"], axis=1), jnp.stack(outs_s["kr"], axis=1),
            jnp.stack(outs_s["s5r"]), jnp.stack(outs_s["s5i"]), jnp.stack(outs_s["hg"]))
```

```python
import functools
import math

import jax
import jax.numpy as jnp
from jax import lax
from jax.experimental import pallas as pl
from jax.experimental.pallas import tpu as pltpu

F32 = jnp.float32
BF16 = jnp.bfloat16

LANES = 128
VMEM_LIMIT_BYTES = 56 * 1024 * 1024

D_MODEL = 1024
MLA_HEADS = 8
MLA_NOPE = 64
MLA_ROPE = 32
MLA_QK = MLA_NOPE + MLA_ROPE
MLA_V = 64
MLA_Q_LORA = 768
MLA_KV_LORA = 256
ROPE_THETA = 10000.0
PAGE_SIZE = 128
S5_WIDTH = 512
S5_GROUP = 16
S5_GROUPS = S5_WIDTH // S5_GROUP
S5_STATE = 64
S5_NSTATE = S5_GROUPS * S5_STATE
HGRN_HEADS = 8
HGRN_DK = 128
HGRN_SUB = 32
HGRN_EXP_CLAMP = 80.0
MEM_HEADS = 4
MEM_HEAD_DIM = 128
MEM_WIDTH = MEM_HEADS * MEM_HEAD_DIM
D_FF = 4 * D_MODEL
EPS = 1e-6
LOG2E = math.log2(math.e)
HP = MLA_HEADS * LANES


def _cparams(sem):
    return pltpu.CompilerParams(dimension_semantics=sem, vmem_limit_bytes=VMEM_LIMIT_BYTES)


def _rms(x, g):
    return x * lax.rsqrt(jnp.mean(x * x, axis=-1, keepdims=True) + EPS) * g


def _sigmoid(x):
    return 1.0 / (1.0 + jnp.exp(-x))


def _dot(a, b):
    return jnp.dot(a, b, preferred_element_type=F32)


def _dot_nt(a, b):
    return lax.dot_general(a, b, (((1,), (1,)), ((), ())), preferred_element_type=F32)


def _dot_tn(a, b):
    return lax.dot_general(a, b, (((0,), (0,)), ((), ())), preferred_element_type=F32)


def _row_to_col(row, n):
    r = lax.broadcasted_iota(jnp.int32, (n, n), 0)
    c = lax.broadcasted_iota(jnp.int32, (n, n), 1)
    return jnp.sum(jnp.where(r == c, jnp.broadcast_to(row, (n, n)), 0.0), axis=1, keepdims=True)


def _div_pow2(x, d):
    return lax.shift_right_logical(x, int(math.log2(d)))


def _full_spec(shape):
    nd = len(shape)
    return pl.BlockSpec(shape, lambda *_: (0,) * nd)


def row_spec(tm, width):
    return pl.BlockSpec((tm, width), lambda i: (i, 0))


def _even_proj_kernel(h_ref, g_ref, wp_ref, cqn_ref, wuq_ref, qg_ref, ckvn_ref, wkv_ref, kg_ref,
                      cq_ref, s1q_ref, s2q_ref, ck_ref, s1k_ref, s2k_ref,
                      q_out, k_out, v_out, ckv_out, kr_out, u_out, *maybe_qk_out, nrg):
    o1 = MLA_Q_LORA
    o2 = o1 + MLA_KV_LORA
    o3 = o2 + LANES
    half = MLA_ROPE // 2
    inv_qk = 1.0 / MLA_QK
    qg = qg_ref[...]
    kg = kg_ref[...]
    tm = h_ref.shape[0]
    groups = [slice(tm // nrg * t, tm // nrg * (t + 1)) for t in range(nrg)]
    proj = [_dot(_rms(h_ref[r, :], g_ref[...]).astype(BF16), wp_ref[...]) for r in groups]
    qf, kv = [], []
    for r, p in zip(groups, proj):
        u_out[r, :] = p[:, o3:]
        qf.append(_dot(_rms(p[:, :o1], cqn_ref[...]).astype(BF16), wuq_ref[...]))
        ckv = _rms(p[:, o1:o2], ckvn_ref[...])
        ckv_out[r, :] = ckv
        kv.append(_dot(ckv.astype(BF16), wkv_ref[...]))
    for r, p, qfr, kvr in zip(groups, proj, qf, kv):
        kr = p[:, o2:o3]
        krr = (kr * ck_ref[r, :] + pltpu.roll(kr, half, 1) * s1k_ref[r, :]
               + pltpu.roll(kr, LANES - half, 1) * s2k_ref[r, :])
        kr_out[r, :] = krr[:, :MLA_ROPE]
        kr_sh = pltpu.roll(krr, MLA_NOPE, 1)
        cq_t, s1q_t, s2q_t = cq_ref[r, :], s1q_ref[r, :], s2q_ref[r, :]
        for h in range(MLA_HEADS):
            sl = slice(LANES * h, LANES * (h + 1))
            qh = qfr[:, sl]
            qh = qh * lax.rsqrt(jnp.sum(qh * qh, axis=-1, keepdims=True) * inv_qk + EPS) * qg
            qh = (qh * cq_t + pltpu.roll(qh, half, 1) * s1q_t + pltpu.roll(qh, LANES - half, 1) * s2q_t)
            q_out[r, sl] = qh.astype(BF16)
            if maybe_qk_out:
                maybe_qk_out[0][r, sl] = (qh * kg).astype(BF16)
            kh = kvr[:, sl] + kr_sh
            kh = kh * lax.rsqrt(jnp.sum(kh * kh, axis=-1, keepdims=True) * inv_qk + EPS) * kg
            k_out[r, sl] = kh.astype(BF16)
        v_out[:, r] = kvr[:, HP:].T.astype(BF16)


def _even_proj(h, g, wp, cqn, wuq, qg, ckvn, wkv, kg, tabs, *, tm, u_shape, u_spec, emit_qk):
    m = h.shape[0]
    ltab = tabs[0].shape[0]
    ntab = ltab // tm
    row = lambda i: (i, 0)
    tab_spec = pl.BlockSpec((tm, LANES), lambda i: (i % ntab, 0))
    in_specs = [pl.BlockSpec((tm, D_MODEL), row), _full_spec(g.shape), _full_spec(wp.shape),
                _full_spec(cqn.shape), _full_spec(wuq.shape), _full_spec(qg.shape),
                _full_spec(ckvn.shape), _full_spec(wkv.shape), _full_spec(kg.shape)] + [tab_spec] * 6
    out_shape = [jax.ShapeDtypeStruct((m, HP), BF16), jax.ShapeDtypeStruct((m, HP), BF16),
                 jax.ShapeDtypeStruct((m // tm, HP, tm), BF16), jax.ShapeDtypeStruct((m, MLA_KV_LORA), F32),
                 jax.ShapeDtypeStruct((m, MLA_ROPE), F32), jax.ShapeDtypeStruct(u_shape, F32)]
    out_specs = [pl.BlockSpec((tm, HP), row), pl.BlockSpec((tm, HP), row),
                 pl.BlockSpec((None, HP, tm), lambda i: (i, 0, 0)),
                 pl.BlockSpec((tm, MLA_KV_LORA), row), pl.BlockSpec((tm, MLA_ROPE), row),
                 u_spec]
    if emit_qk:
        out_shape.append(jax.ShapeDtypeStruct((m, HP), BF16))
        out_specs.append(pl.BlockSpec((tm, HP), row))
    return pl.pallas_call(
        functools.partial(_even_proj_kernel, nrg=2), grid=(m // tm,), in_specs=in_specs, out_specs=out_specs,
        out_shape=out_shape,
        compiler_params=_cparams(("arbitrary",)), name="even_proj",
    )(h, g, wp, cqn, wuq, qg, ckvn, wkv, kg, *tabs)


def _flash_kernel(q_ref, k_ref, vt_ref, o_ref, *, tq, tk, hg):
    i = pl.program_id(1)
    nfull = (i * tq) // tk
    key = lax.broadcasted_iota(jnp.int32, (tk, tq), 0) + nfull * tk
    qry = lax.broadcasted_iota(jnp.int32, (tk, tq), 1) + i * tq
    causal = qry >= key

    heads = [slice(LANES * h, LANES * (h + 1)) for h in range(MLA_HEADS)]

    def update(j, carry, masked):
        off = pl.multiple_of(j * tk, tk)
        out = []
        for h0 in range(0, MLA_HEADS, hg):
            grp = range(h0, h0 + hg)
            scores = [_dot_nt(k_ref[pl.ds(off, tk), heads[h]], q_ref[:, heads[h]]) for h in grp]
            probs, stats = [], []
            for h, s in zip(grp, scores):
                m, l = carry[3 * h], carry[3 * h + 1]
                if masked:
                    s = jnp.where(causal, s, -jnp.inf)
                m_new = jnp.maximum(m, jnp.max(s, axis=0, keepdims=True))
                alpha = jnp.exp2(m - m_new)
                p = jnp.exp2(s - m_new)
                stats.append((m_new, alpha * l + jnp.sum(p, axis=0, keepdims=True), alpha))
                probs.append(p.astype(BF16))
            pv = [_dot(vt_ref[j, heads[h], :], p) for p, h in zip(probs, grp)]
            for h, (m_new, l_new, alpha), o in zip(grp, stats, pv):
                out += [m_new, l_new, alpha * carry[3 * h + 2] + o]
        return tuple(out)

    init = (jnp.full((1, tq), -jnp.inf, F32), jnp.zeros((1, tq), F32), jnp.zeros((LANES, tq), F32))
    carry = lax.fori_loop(0, nfull, lambda j, c: update(j, c, False), init * MLA_HEADS)
    carry = update(nfull, carry, True)
    for h, sl in enumerate(heads):
        o_ref[:, sl] = (carry[3 * h + 2] / carry[3 * h + 1]).T.astype(BF16)


def _flash_attention(q, k, vt, *, tq, hg=2):
    b, l, _ = q.shape
    tk = vt.shape[2]
    nkb = l // tk
    return pl.pallas_call(
        functools.partial(_flash_kernel, tq=tq, tk=tk, hg=hg),
        grid=(b, l // tq),
        in_specs=[pl.BlockSpec((None, tq, HP), lambda bi, i: (bi, i, 0)),
                  pl.BlockSpec((None, l, HP), lambda bi, i: (bi, 0, 0)),
                  pl.BlockSpec((nkb, HP, tk), lambda bi, i: (bi, 0, 0))],
        out_specs=pl.BlockSpec((None, tq, HP), lambda bi, i: (bi, i, 0)),
        out_shape=jax.ShapeDtypeStruct((b, l, HP), BF16),
        compiler_params=_cparams(("arbitrary", "arbitrary")), name="prompt_attention",
    )(q, k, vt)


def _paged_kernel(pt_ref, *refs, npg, ngrp, nsteps, nq):
    lat_refs = refs[:npg]
    krt_refs = refs[npg:2 * npg]
    (qn_ref, rhs2_ref, wuk_ref, cnew_ref, krnew_ref, wuv_ref,
     o_ref, wabs, m_scr, l_scr, a_scr) = refs[2 * npg:]
    del pt_ref
    s = pl.program_id(1)
    nslots = nsteps * ngrp + 1
    ncol = LANES
    inv_qk = 1.0 / MLA_QK

    @pl.when(s == 0)
    def _():
        wabs[...] = _dot(wuk_ref[...], qn_ref[...]).astype(BF16)

    def stats(blocks, mask):
        kn = [_dot(c, wuk_ref[...]) for c, _ in blocks]
        sq = [k * k for k in kn]
        psum = [q[:, 0:LANES] + q[:, LANES:2 * LANES] + q[:, 2 * LANES:3 * LANES] + q[:, 3 * LANES:] for q in sq]
        r2 = [_dot(jnp.concatenate([p.astype(BF16), x], axis=1), rhs2_ref[...]) for p, (_, x) in zip(psum, blocks)]
        scn = [_dot(c, wabs[...]) for c, _ in blocks]
        probs, out = [], []
        for t in range(len(blocks)):
            sc = (scn[t] + r2[t][:, LANES:]) * lax.rsqrt(r2[t][:, :LANES] * inv_qk + EPS)
            if mask is not None:
                sc = jnp.where(mask, sc, -jnp.inf)
            m = jnp.max(sc, axis=0, keepdims=True)
            p = jnp.exp2(sc - m)
            out.append((m, jnp.sum(p, axis=0, keepdims=True)))
            probs.append(p.astype(BF16))
        acc = [_dot_tn(p, c) for p, (c, _) in zip(probs, blocks)]
        return [(m, l, a) for (m, l), a in zip(out, acc)]

    zpad = jnp.zeros((LANES - 2 * MLA_ROPE, PAGE_SIZE), F32)

    def rope_block(g):
        krt = krt_refs[g][...]
        return jnp.concatenate([krt, krt * krt, zpad], axis=0).T.astype(BF16)

    pg = npg // ngrp
    groups = [(jnp.concatenate([lat_refs[g][...].astype(BF16) for g in range(pg * t, pg * (t + 1))], axis=0),
               jnp.concatenate([rope_block(g) for g in range(pg * t, pg * (t + 1))], axis=0))
              for t in range(ngrp)]
    for t, (m, l, a) in enumerate(stats(groups, None)):
        slot = s * ngrp + t
        m_scr[pl.ds(slot, 1), :] = m
        l_scr[pl.ds(slot, 1), :] = l
        a_scr[slot] = a

    @pl.when(s == nsteps - 1)
    def _():
        nnew = cnew_ref.shape[0]
        krn = jnp.concatenate([krnew_ref[...], jnp.zeros((nnew, LANES - MLA_ROPE), F32)], axis=1)
        krn = krn + pltpu.roll(krn * krn, MLA_ROPE, 1)
        key = lax.broadcasted_iota(jnp.int32, (nnew, ncol), 0)
        qry = _div_pow2(lax.broadcasted_iota(jnp.int32, (nnew, ncol), 1), MLA_HEADS)
        (m2, l2, a2), = stats([(cnew_ref[...].astype(BF16), krn.astype(BF16))], key <= qry)
        m_scr[nslots - 1:nslots, :] = m2
        l_scr[nslots - 1:nslots, :] = l2
        a_scr[nslots - 1] = a2
        mall = m_scr[0:nslots, :]
        w = jnp.exp2(mall - jnp.max(mall, axis=0, keepdims=True))
        den = jnp.sum(l_scr[0:nslots, :] * w, axis=0, keepdims=True)
        wn = w / den
        num = jnp.zeros((ncol, MLA_KV_LORA), F32)
        for t in range(nslots):
            num = num + a_scr[t] * _row_to_col(wn[t:t + 1, :], ncol)
        full = _dot(num.astype(BF16), wuv_ref[...])
        hrow = lax.broadcasted_iota(jnp.int32, (MLA_HEADS, MLA_HEADS * MLA_V), 0)
        hcol = _div_pow2(lax.broadcasted_iota(jnp.int32, (MLA_HEADS, MLA_HEADS * MLA_V), 1), MLA_V)
        rows = []
        for qi in range(nq):
            blk = full[MLA_HEADS * qi:MLA_HEADS * (qi + 1), :]
            rows.append(jnp.sum(jnp.where(hrow == hcol, blk, 0.0), axis=0, keepdims=True))
        o_ref[...] = jnp.concatenate(rows, axis=0)


def _paged_attention(page_table, cache_lat, cache_krt, e, qn, rhs2, wuk, cnew, krnew, wuv, *, npg, ngrp, nq):
    nb, npages = page_table.shape
    nsteps = npages // npg
    nnew = cnew.shape[1]
    nslots = nsteps * ngrp + 1

    def page_spec(shape, g):
        return pl.BlockSpec((None, None) + shape, lambda b, s, pt: (pt[b, s * npg + g], e, 0, 0))

    per_b3 = lambda b, s, pt: (b, 0, 0)
    const2 = lambda b, s, pt: (0, 0)
    in_specs = ([page_spec((PAGE_SIZE, MLA_KV_LORA), g) for g in range(npg)]
                + [page_spec((MLA_ROPE, PAGE_SIZE), g) for g in range(npg)]
                + [pl.BlockSpec((None,) + qn.shape[1:], per_b3), pl.BlockSpec((None,) + rhs2.shape[1:], per_b3),
                   pl.BlockSpec(wuk.shape, const2),
                   pl.BlockSpec((None, nnew, MLA_KV_LORA), per_b3), pl.BlockSpec((None, nnew, MLA_ROPE), per_b3),
                   pl.BlockSpec(wuv.shape, const2)])
    grid_spec = pltpu.PrefetchScalarGridSpec(
        num_scalar_prefetch=1, grid=(nb, nsteps), in_specs=in_specs,
        out_specs=pl.BlockSpec((None, nq, MLA_HEADS * MLA_V), per_b3),
        scratch_shapes=[pltpu.VMEM((MLA_KV_LORA, LANES), BF16),
                        pltpu.VMEM((nslots, LANES), F32), pltpu.VMEM((nslots, LANES), F32),
                        pltpu.VMEM((nslots, LANES, MLA_KV_LORA), F32)])
    return pl.pallas_call(
        functools.partial(_paged_kernel, npg=npg, ngrp=ngrp, nsteps=nsteps, nq=nq),
        grid_spec=grid_spec, out_shape=jax.ShapeDtypeStruct((nb, nq, MLA_HEADS * MLA_V), F32),
        compiler_params=_cparams(("arbitrary", "arbitrary")), name="paged_attention",
    )(page_table, *([cache_lat] * npg), *([cache_krt] * npg), qn, rhs2, wuk, cnew, krnew, wuv)


def _s5_kernel(u_ref, h0r_ref, h0i_ref, lamr_ref, lami_ref, lstep_ref, brm_ref, bim_ref, crm_ref, cim_ref,
               d_ref, wg_ref, bg_ref, o_ref, hr_out, hi_out, xr_scr, xi_scr, hcr, hci, disc, io_scr, *,
               tt, nb, strip, interleave):
    c = pl.program_id(0)

    @pl.when(c == 0)
    def _():
        lr = jnp.minimum(lamr_ref[...], -1e-4)
        li = lami_ref[...]
        dt = jnp.exp(lstep_ref[...])
        mag = jnp.exp(lr * dt)
        abr = mag * jnp.cos(li * dt)
        abi = mag * jnp.sin(li * dt)
        den = lr * lr + li * li
        disc[0:1, :] = abr
        disc[1:2, :] = abi
        disc[2:3, :] = ((abr - 1.0) * lr + abi * li) / den
        disc[3:4, :] = (abi * lr - (abr - 1.0) * li) / den
        hcr[...] = h0r_ref[...]
        hci[...] = h0i_ref[...]

    nlb = S5_WIDTH // LANES
    if interleave:
        for b in range(nb):
            for j in range(nlb):
                c0 = S5_WIDTH * b + LANES * j
                io_scr[j, pl.ds(b, tt, stride=nb), :] = u_ref[:, c0:c0 + LANES]
        u = jnp.concatenate([io_scr[j] for j in range(nlb)], axis=1)
    else:
        u = u_ref[...]
    ub = u.astype(BF16)
    kc = 2 * LANES
    ks = kc * S5_STATE // S5_GROUP
    for k in range(S5_WIDTH // kc):
        cols = slice(kc * k, kc * (k + 1))
        sts = slice(ks * k, ks * (k + 1))
        pr = _dot(ub[:, cols], brm_ref[cols, sts])
        pi = _dot(ub[:, cols], bim_ref[cols, sts])
        cor = disc[2:3, sts]
        coi = disc[3:4, sts]
        xr_scr[:, sts] = cor * pr - coi * pi
        xi_scr[:, sts] = cor * pi + coi * pr

    for s0 in range(0, S5_NSTATE, strip):
        lanes = slice(s0, s0 + strip)
        ar = jnp.broadcast_to(disc[0:1, lanes], (nb, strip))
        ai = jnp.broadcast_to(disc[1:2, lanes], (nb, strip))

        def step(t, carry, lanes=lanes, ar=ar, ai=ai):
            hr, hi = carry
            rows = pl.ds(pl.multiple_of(t * nb, nb), nb)
            nr = ar * hr - ai * hi + xr_scr[rows, lanes]
            ni = ar * hi + ai * hr + xi_scr[rows, lanes]
            xr_scr[rows, lanes] = nr
            xi_scr[rows, lanes] = ni
            return nr, ni

        hr, hi = lax.fori_loop(0, tt, step, (hcr[:, lanes], hci[:, lanes]))
        hcr[:, lanes] = hr
        hci[:, lanes] = hi

    ys = []
    for k in range(S5_WIDTH // kc):
        cols = slice(kc * k, kc * (k + 1))
        sts = slice(ks * k, ks * (k + 1))
        ys.append(_dot(xr_scr[:, sts].astype(BF16), crm_ref[sts, cols])
                  - _dot(xi_scr[:, sts].astype(BF16), cim_ref[sts, cols]))
    y = jnp.concatenate(ys, axis=1) + d_ref[...] * u
    z = jax.nn.gelu(y)
    gate = _sigmoid(_dot(z.astype(BF16), wg_ref[...]) + bg_ref[...])
    if interleave:
        out = z * gate
        for j in range(nlb):
            io_scr[j] = out[:, LANES * j:LANES * (j + 1)]
        for b in range(nb):
            for j in range(nlb):
                c0 = S5_WIDTH * b + LANES * j
                o_ref[:, c0:c0 + LANES] = io_scr[j, pl.ds(b, tt, stride=nb), :].astype(BF16)
    else:
        o_ref[...] = (z * gate).astype(BF16)

    @pl.when(c == pl.num_programs(0) - 1)
    def _():
        hr_out[...] = hcr[...]
        hi_out[...] = hci[...]


def _s5(u, h0r, h0i, lamr, lami, lstep, brm, bim, crm, cim, d, wg, bg, *, tt, nb, strip, interleave):
    steps = u.shape[0] if interleave else u.shape[0] // nb
    blk = tt * nb
    consts = (h0r, h0i, lamr, lami, lstep, brm, bim, crm, cim, d, wg, bg)
    io_spec = row_spec(tt, nb * S5_WIDTH) if interleave else row_spec(blk, S5_WIDTH)
    return pl.pallas_call(
        functools.partial(_s5_kernel, tt=tt, nb=nb, strip=strip, interleave=interleave),
        grid=(steps // tt,),
        in_specs=[io_spec] + [_full_spec(a.shape) for a in consts],
        out_specs=[io_spec, _full_spec((nb, S5_NSTATE)), _full_spec((nb, S5_NSTATE))],
        out_shape=[jax.ShapeDtypeStruct(u.shape, BF16),
                   jax.ShapeDtypeStruct((nb, S5_NSTATE), F32), jax.ShapeDtypeStruct((nb, S5_NSTATE), F32)],
        scratch_shapes=[pltpu.VMEM((blk, S5_NSTATE), F32), pltpu.VMEM((blk, S5_NSTATE), F32),
                        pltpu.VMEM((nb, S5_NSTATE), F32), pltpu.VMEM((nb, S5_NSTATE), F32),
                        pltpu.VMEM((8, S5_NSTATE), F32), pltpu.VMEM((S5_WIDTH // LANES, blk, LANES), F32)],
        compiler_params=_cparams(("arbitrary",)), name="s5",
    )(u, *consts)


def _mm_res_kernel(*refs, nop):
    res_ref = refs[2 * nop]
    o_ref = refs[2 * nop + 1]
    acc = res_ref[...]
    for t in range(nop):
        acc = acc + _dot(refs[t][...].astype(BF16), refs[nop + t][...])
    o_ref[...] = acc


def _mm_res(ops, ws, res, *, tm, op_specs):
    m, n = res.shape
    row = lambda i: (i, 0)
    in_specs = list(op_specs) + [_full_spec(w.shape) for w in ws] + [pl.BlockSpec((tm, n), row)]
    return pl.pallas_call(
        functools.partial(_mm_res_kernel, nop=len(ops)), grid=(m // tm,), in_specs=in_specs,
        out_specs=pl.BlockSpec((tm, n), row), out_shape=jax.ShapeDtypeStruct((m, n), F32),
        compiler_params=_cparams(("arbitrary",)), name="matmul_residual",
    )(*ops, *ws, res)


def _mem_kv_kernel(x_ref, g_ref, w_ref, kg_ref, k_out, v_out):
    mn = _rms(x_ref[...], g_ref[...]).astype(BF16)
    kv = _dot(mn, w_ref[...])
    kg = kg_ref[...]
    for h in range(MEM_HEADS):
        sl = slice(LANES * h, LANES * (h + 1))
        k_out[:, sl] = _rms(kv[:, sl], kg)
    v_out[...] = kv[:, MEM_WIDTH:]


def _mem_kv(x, g, w, kg, *, tm):
    m = x.shape[0]
    row = lambda i: (i, 0)
    return pl.pallas_call(
        _mem_kv_kernel, grid=(m // tm,),
        in_specs=[pl.BlockSpec((tm, D_MODEL), row), _full_spec(g.shape), _full_spec(w.shape), _full_spec(kg.shape)],
        out_specs=[pl.BlockSpec((tm, MEM_WIDTH), row), pl.BlockSpec((tm, MEM_WIDTH), row)],
        out_shape=[jax.ShapeDtypeStruct((m, MEM_WIDTH), F32), jax.ShapeDtypeStruct((m, MEM_WIDTH), F32)],
        compiler_params=_cparams(("arbitrary",)), name="mem_kv",
    )(x, g, w, kg)


def _mem_attn_kernel(h_ref, g_ref, wq_ref, qg_ref, mk_ref, mv_ref, wo_ref, o_ref, *, nb, tl, paired):
    x = h_ref[...].reshape(nb * tl, D_MODEL)
    hn = _rms(x, g_ref[...]).astype(BF16)
    q = _dot(hn, wq_ref[...])
    qg = qg_ref[...] * (MEM_HEAD_DIM ** -0.5 * LOG2E)
    qn = [_rms(q[:, LANES * h:LANES * (h + 1)], qg).astype(BF16) for h in range(MEM_HEADS)]

    def head_block(ref, b, h):
        if paired:
            return jnp.concatenate([ref[b, :, h, :], ref[b, :, MEM_HEADS + h, :]], axis=0).astype(BF16)
        return ref[b, :, LANES * h:LANES * (h + 1)].astype(BF16)

    pairs = [(b, h) for b in range(nb) for h in range(MEM_HEADS)]
    scores = [_dot_nt(qn[h][tl * b:tl * (b + 1), :], head_block(mk_ref, b, h)) for b, h in pairs]
    probs = [jnp.exp2(s - jnp.max(s, axis=-1, keepdims=True)) for s in scores]
    outs = [_dot(p.astype(BF16), head_block(mv_ref, b, h)) / jnp.sum(p, axis=-1, keepdims=True)
            for p, (b, h) in zip(probs, pairs)]
    rows = [jnp.concatenate(outs[MEM_HEADS * b:MEM_HEADS * (b + 1)], axis=1) for b in range(nb)]
    o = (rows[0] if nb == 1 else jnp.concatenate(rows, axis=0)).astype(BF16)
    o_ref[...] = (x + _dot(o, wo_ref[...])).reshape(nb, tl, D_MODEL)


def _mem_attn(h, g, wq, qg, mk, mv, wo, *, nb, tl, layer=None):
    b, l, _ = h.shape
    blk = lambda bi, i: (bi, i, 0)
    if layer is None:
        mem_spec = pl.BlockSpec((nb,) + mk.shape[1:], lambda bi, i: (bi, 0, 0))
    else:
        mem_spec = pl.BlockSpec((None, nb) + mk.shape[2:], lambda bi, i: (layer, bi, 0, 0, 0))
    return pl.pallas_call(
        functools.partial(_mem_attn_kernel, nb=nb, tl=tl, paired=layer is not None), grid=(b // nb, l // tl),
        in_specs=[pl.BlockSpec((nb, tl, D_MODEL), blk), _full_spec(g.shape), _full_spec(wq.shape),
                  _full_spec(qg.shape), mem_spec, mem_spec, _full_spec(wo.shape)],
        out_specs=pl.BlockSpec((nb, tl, D_MODEL), blk),
        out_shape=jax.ShapeDtypeStruct(h.shape, F32),
        compiler_params=_cparams(("arbitrary", "arbitrary")), name="mem_attention",
    )(h, g, wq, qg, mk, mv, wo)


def _mlp_kernel(h_ref, g_ref, wu_ref, wd_ref, o_ref, xn_scr, acc_scr):
    j = pl.program_id(1)

    @pl.when(j == 0)
    def _():
        xn_scr[...] = _rms(h_ref[...], g_ref[...]).astype(BF16)
        acc_scr[...] = jnp.zeros(acc_scr.shape, F32)

    a = _dot(xn_scr[...], wu_ref[...])
    a = jnp.square(jnp.maximum(a, 0.0)).astype(BF16)
    acc_scr[...] += _dot(a, wd_ref[...])

    @pl.when(j == pl.num_programs(1) - 1)
    def _():
        o_ref[...] = h_ref[...] + acc_scr[...]


def _mlp(h, g, wu, wd, *, tm, tf):
    m = h.shape[0]
    return pl.pallas_call(
        _mlp_kernel, grid=(m // tm, D_FF // tf),
        in_specs=[pl.BlockSpec((tm, D_MODEL), lambda i, j: (i, 0)), _full_spec(g.shape),
                  pl.BlockSpec((D_MODEL, tf), lambda i, j: (0, j)), pl.BlockSpec((tf, D_MODEL), lambda i, j: (j, 0))],
        out_specs=pl.BlockSpec((tm, D_MODEL), lambda i, j: (i, 0)),
        out_shape=jax.ShapeDtypeStruct((m, D_MODEL), F32),
        scratch_shapes=[pltpu.VMEM((tm, D_MODEL), BF16), pltpu.VMEM((tm, D_MODEL), F32)],
        compiler_params=_cparams(("arbitrary", "arbitrary")), name="mlp",
    )(h, g, wu, wd)


def _norm_mm_kernel(h_ref, g_ref, w_ref, o_ref, xn_scr):
    @pl.when(pl.program_id(1) == 0)
    def _():
        xn_scr[...] = _rms(h_ref[...], g_ref[...]).astype(BF16)

    o_ref[...] = _dot(xn_scr[...], w_ref[...])


def _norm_mm(h, g, w, *, tm, tn):
    m = h.shape[0]
    n = w.shape[1]
    return pl.pallas_call(
        _norm_mm_kernel, grid=(m // tm, n // tn),
        in_specs=[pl.BlockSpec((tm, D_MODEL), lambda i, j: (i, 0)), _full_spec(g.shape),
                  pl.BlockSpec((D_MODEL, tn), lambda i, j: (0, j))],
        out_specs=pl.BlockSpec((tm, tn), lambda i, j: (i, j)),
        out_shape=jax.ShapeDtypeStruct((m, n), F32),
        scratch_shapes=[pltpu.VMEM((tm, D_MODEL), BF16)],
        compiler_params=_cparams(("arbitrary", "arbitrary")), name="norm_matmul",
    )(h, g, w)


def _hgrn_kernel(q_ref, f_ref, i_ref, g_ref, lbp_ref, on_ref, s0_ref, o_ref, s_out, s_scr, *,
                 chunk, nchunk, layer, l_valid):
    c = pl.program_id(1)
    tb = chunk * nchunk

    @pl.when(c == 0)
    def _():
        s_scr[...] = s0_ref[...]

    lbp = lbp_ref[...]
    e = jnp.exp(lbp - jnp.max(lbp, axis=0, keepdims=True))
    sm = e / jnp.sum(e, axis=0, keepdims=True)
    lb = jnp.sum(sm[0:layer + 1, :], axis=0, keepdims=True) - sm[0:1, :]

    q = q_ref[...]
    qa = q * _sigmoid(q)
    fg = lb + (1.0 - lb) * _sigmoid(f_ref[...])
    logf = jnp.log(fg)
    kk = 1.0 - fg
    v = i_ref[...]
    if l_valid is not None:
        valid = (lax.broadcasted_iota(jnp.int32, (tb, 1), 0) + c * tb) < l_valid
        logf = jnp.where(valid, logf, 0.0)
        kk = jnp.where(valid, kk, 0.0)
    vb = v.astype(BF16)

    tr = lax.broadcasted_iota(jnp.int32, (tb, tb), 0)
    tc = lax.broadcasted_iota(jnp.int32, (tb, tb), 1)
    same_chunk = _div_pow2(tr, chunk) == _div_pow2(tc, chunk)
    tri = jnp.where(same_chunk, jnp.where(tr >= tc, 1.0, 0.0), 0.0).astype(BF16)
    hi = logf.astype(BF16)
    lo = (logf - hi.astype(F32)).astype(BF16)
    bcum = _dot(tri, hi) + _dot(tri, lo)
    qhat = (qa * jnp.exp(bcum)).astype(BF16)

    nsub = chunk // HGRN_SUB
    khat, dec, qloc, kloc, masks = [], [], [], [], []
    spread = jnp.zeros((1, bcum.shape[1]), F32)
    for ci in range(nchunk):
        c0 = ci * chunk
        blast = bcum[c0 + chunk - 1:c0 + chunk, :]
        khat.append((kk[c0:c0 + chunk, :] * jnp.exp(blast - bcum[c0:c0 + chunk, :])).astype(BF16))
        dec.append(jnp.exp(blast))
        for i in range(nsub):
            r0 = c0 + i * HGRN_SUB
            r1 = r0 + HGRN_SUB
            base = bcum[r0 - 1:r0, :] if i > 0 else jnp.zeros((1, bcum.shape[1]), F32)
            spread = jnp.minimum(spread, bcum[r1 - 1:r1, :] - base)
            qloc.append((qa[r0:r1, :] * jnp.exp(bcum[r0:r1, :] - base)).astype(BF16))
            kloc.append((kk[c0:r1, :] * jnp.exp(jnp.minimum(base - bcum[c0:r1, :], HGRN_EXP_CLAMP))).astype(BF16))
    wild = jnp.min(spread) < -HGRN_EXP_CLAMP
    for i in range(nsub):
        ncols = (i + 1) * HGRN_SUB
        ar = lax.broadcasted_iota(jnp.int32, (HGRN_SUB, ncols), 0) + i * HGRN_SUB
        ac = lax.broadcasted_iota(jnp.int32, (HGRN_SUB, ncols), 1)
        in_block = ac >= i * HGRN_SUB
        masks.append(jnp.logical_and(ar >= ac, jnp.logical_not(jnp.logical_and(wild, in_block))))

    hsl = [slice(HGRN_DK * h, HGRN_DK * (h + 1)) for h in range(HGRN_HEADS)]

    def in_block_exact():
        pos = jnp.bitwise_and(lax.broadcasted_iota(jnp.int32, (tb, 1), 0), HGRN_SUB - 1)
        out = jnp.zeros((tb, bcum.shape[1]), F32)
        for j in range(HGRN_SUB):
            ok = pos >= j
            kj, bj, vj = (kk, bcum, v) if j == 0 else (pltpu.roll(x, j, 0) for x in (kk, bcum, v))
            e = jnp.where(ok, qa * kj * jnp.exp(jnp.where(ok, bcum - bj, 0.0)), 0.0)
            out = out + jnp.concatenate(
                [jnp.sum(e[:, sl], axis=-1, keepdims=True) * vj[:, sl] for sl in hsl], axis=1)
        return out

    blocks = [(ci, i) for ci in range(nchunk) for i in range(nsub)]
    att = [[_dot_nt(qloc[ci * nsub + i][:, sl], kloc[ci * nsub + i][:, sl]) for ci, i in blocks] for sl in hsl]
    att = [[jnp.where(masks[i], a, 0.0).astype(BF16) for a, (ci, i) in zip(row, blocks)] for row in att]
    intra = [[_dot(a, vb[ci * chunk:ci * chunk + (i + 1) * HGRN_SUB, sl]) for a, (ci, i) in zip(row, blocks)]
             for row, sl in zip(att, hsl)]
    kv = [[_dot_tn(khat[ci][:, sl], vb[ci * chunk:(ci + 1) * chunk, sl]) for ci in range(nchunk)] for sl in hsl]
    dcol = [[_row_to_col(dec[ci][:, sl], HGRN_DK) for ci in range(nchunk)] for sl in hsl]
    st = [s_scr[h] for h in range(HGRN_HEADS)]
    inter = [[] for _ in hsl]
    for ci in range(nchunk):
        rows = slice(ci * chunk, (ci + 1) * chunk)
        for h, sl in enumerate(hsl):
            inter[h].append(_dot(qhat[rows, sl], st[h].astype(BF16)))
        for h in range(HGRN_HEADS):
            st[h] = dcol[h][ci] * st[h] + kv[h][ci]
    o_heads = []
    for h in range(HGRN_HEADS):
        s_scr[h] = st[h]
        parts = [inter[h][ci][i * HGRN_SUB:(i + 1) * HGRN_SUB, :] + intra[h][ci * nsub + i] for ci, i in blocks]
        o_heads.append(parts[0] if len(parts) == 1 else jnp.concatenate(parts, axis=0))

    o = jnp.concatenate(o_heads, axis=1)
    o = lax.cond(wild, lambda: o + in_block_exact(), lambda: o)
    g = g_ref[...]
    o_ref[...] = (_rms(o, on_ref[...]) * (g * _sigmoid(g))).astype(BF16)

    @pl.when(c == pl.num_programs(1) - 1)
    def _():
        s_out[...] = s_scr[...]


def _hgrn(proj, lbp, on, s0, *, chunk, nchunk, layer, l_valid):
    b, l, _ = proj.shape
    w = D_MODEL
    tb = chunk * nchunk

    def col(k):
        return pl.BlockSpec((None, tb, w), lambda bi, c: (bi, c, k))

    st_spec = pl.BlockSpec((None, HGRN_HEADS, HGRN_DK, HGRN_DK), lambda bi, c: (bi, 0, 0, 0))
    return pl.pallas_call(
        functools.partial(_hgrn_kernel, chunk=chunk, nchunk=nchunk, layer=layer, l_valid=l_valid),
        grid=(b, l // tb),
        in_specs=[col(0), col(1), col(2), col(3), _full_spec(lbp.shape), _full_spec(on.shape), st_spec],
        out_specs=[pl.BlockSpec((None, tb, w), lambda bi, c: (bi, c, 0)), st_spec],
        out_shape=[jax.ShapeDtypeStruct((b, l, w), BF16), jax.ShapeDtypeStruct(s0.shape, F32)],
        scratch_shapes=[pltpu.VMEM((HGRN_HEADS, HGRN_DK, HGRN_DK), F32)],
        compiler_params=_cparams(("arbitrary", "arbitrary")), name="hgrn",
    )(proj, proj, proj, proj, lbp, on, s0)


def _pad_last(x, n):
    return jnp.pad(x, [(0, 0)] * (x.ndim - 1) + [(0, n - x.shape[-1])])


def _head_pad(w, per):
    k = w.shape[0]
    return _pad_last(w.reshape(k, -1, per), LANES).reshape(k, -1)


def _rope_tables(pos):
    half = MLA_ROPE // 2
    inv = ROPE_THETA ** (-jnp.arange(half, dtype=F32) / half)
    ang = pos.astype(F32)[:, None] * inv[None, :]
    cos, sin = jnp.cos(ang), jnp.sin(ang)
    n = pos.shape[0]
    z = lambda w: jnp.zeros((n, w), F32)
    scale = MLA_QK ** -0.5 * LOG2E
    cq = scale * jnp.concatenate([jnp.ones((n, MLA_NOPE), F32), cos, cos, z(LANES - MLA_QK)], axis=1)
    s1q = scale * jnp.concatenate([z(MLA_NOPE + half), sin, z(LANES - MLA_QK)], axis=1)
    s2q = scale * jnp.concatenate([z(MLA_NOPE), -sin, z(half + LANES - MLA_QK)], axis=1)
    ck = jnp.concatenate([cos, cos, z(LANES - MLA_ROPE)], axis=1)
    s1k = jnp.concatenate([z(half), sin, z(LANES - MLA_ROPE)], axis=1)
    s2k = jnp.concatenate([-sin, z(LANES - half)], axis=1)
    return (cq, s1q, s2q, ck, s1k, s2k)


def _block_diag(x):
    g, a, b = x.shape
    eye = jnp.eye(g, dtype=x.dtype)
    return (x[:, :, None, :] * eye[:, None, :, None]).reshape(g * a, g * b)


def kernel(x_prompt, x_sample, cache_mla_latent, cache_mla_krope, state_s5_re, state_s5_im, state_hgrn, cache_mem_k, cache_mem_v, page_table, mem_prompt, norm_mix, norm_mem, norm_memsrc, norm_mlp, w_mem_q, w_mem_k, w_mem_v, w_mem_o, mem_q_gain, mem_k_gain, w_mlp_up, w_mlp_down, w_in_even, mla_cq_norm, mla_ckv_norm, w_mla_uq, w_mla_ukv, mla_qn_nope, mla_qn_rope, mla_kn_nope, mla_kn_rope, s5_lambda_re, s5_lambda_im, s5_log_step, s5_b_re, s5_b_im, s5_c_re, s5_c_im, s5_d, s5_w_glu, s5_b_glu, w_out_even, w_in_odd, hgrn_lower_bounds, hgrn_out_norm, w_out_odd):
    bsz, seq, _ = x_prompt.shape
    dbs, dseq, _ = x_sample.shape
    depth = norm_mix.shape[0]
    past_len = page_table.shape[1] * PAGE_SIZE
    ns = 8
    mem_len = mem_prompt.shape[1]
    row2 = lambda a: a.reshape(1, -1).astype(F32)

    hp = x_prompt.reshape(bsz * seq, D_MODEL)
    hs = jnp.pad(x_sample, ((0, 0), (0, ns - dseq), (0, 0))).reshape(dbs * ns, D_MODEL)

    tabs_p = _rope_tables(jnp.arange(seq, dtype=jnp.int32))
    pos_s = past_len + jnp.arange(ns, dtype=jnp.int32)
    tabs_s = tuple(jnp.tile(t, (dbs, 1)) for t in _rope_tables(pos_s))

    outs_p = {k: [] for k in ("lat", "kr", "s5r", "s5i", "hg", "mk", "mv")}
    outs_s = {k: [] for k in ("lat", "kr", "s5r", "s5i", "hg")}

    tm_p = 512
    nl_p = seq // tm_p
    tm_r = 1024
    nl_r = seq // tm_r

    for l in range(depth):
        if l % 2 == 0:
            e = l // 2
            w_in = w_in_even[e]
            o1 = MLA_Q_LORA + MLA_KV_LORA
            wp = jnp.concatenate([w_in[:, :o1], _pad_last(w_in[:, o1:o1 + MLA_ROPE], LANES),
                                  w_in[:, o1 + MLA_ROPE:]], axis=1).astype(BF16)
            wuq = _head_pad(w_mla_uq[e], MLA_QK).astype(BF16)
            ukv = w_mla_ukv[e].reshape(MLA_KV_LORA, MLA_HEADS, MLA_NOPE + MLA_V)
            wuk_c = ukv[:, :, :MLA_NOPE].reshape(MLA_KV_LORA, -1)
            wuv_c = ukv[:, :, MLA_NOPE:].reshape(MLA_KV_LORA, -1)
            wkv = jnp.concatenate([_head_pad(wuk_c, MLA_NOPE), _head_pad(wuv_c, MLA_V)], axis=1).astype(BF16)
            qg = _pad_last(jnp.concatenate([mla_qn_nope[e], mla_qn_rope[e], mla_qn_rope[e]])[None, :], LANES)
            kg = _pad_last(jnp.concatenate([mla_kn_nope[e], mla_kn_rope[e], mla_kn_rope[e]])[None, :], LANES)
            cqn = row2(mla_cq_norm[e])
            ckvn = row2(mla_ckv_norm[e])
            g_mix = row2(norm_mix[l])

            brm = _block_diag(jnp.swapaxes(s5_b_re[e], 1, 2)).astype(BF16)
            bim = _block_diag(jnp.swapaxes(s5_b_im[e], 1, 2)).astype(BF16)
            crm = _block_diag(jnp.swapaxes(s5_c_re[e], 1, 2)).astype(BF16)
            cim = _block_diag(jnp.swapaxes(s5_c_im[e], 1, 2)).astype(BF16)
            lamr = row2(s5_lambda_re[e])
            lami = row2(s5_lambda_im[e])
            lstep = row2(jnp.repeat(s5_log_step[e], S5_STATE))
            s5_consts = (lamr, lami, lstep, brm, bim, crm, cim, row2(s5_d[e]), s5_w_glu[e].astype(BF16),
                         row2(s5_b_glu[e]))
            w_out = w_out_even[e]
            wo_att_c = w_out[:MLA_HEADS * MLA_V].astype(BF16)
            wo_att_p = _pad_last(w_out[:MLA_HEADS * MLA_V].reshape(MLA_HEADS, MLA_V, D_MODEL).swapaxes(1, 2),
                                 LANES).swapaxes(1, 2).reshape(HP, D_MODEL).astype(BF16)
            wo_s5 = w_out[MLA_HEADS * MLA_V:].astype(BF16)

            q, k, v, ckv, kr, u = _even_proj(
                hp, g_mix, wp, cqn, wuq, qg, ckvn, wkv, kg, tabs_p, tm=tm_p,
                u_shape=(seq, bsz * S5_WIDTH), emit_qk=False,
                u_spec=pl.BlockSpec((tm_p, S5_WIDTH), lambda i: (i % nl_p, i // nl_p)))
            o_att = _flash_attention(q.reshape(bsz, seq, HP), k.reshape(bsz, seq, HP), v, tq=512, hg=4)
            z0 = jnp.zeros((bsz, S5_NSTATE), F32)
            o_s5, hr, hi = _s5(u, z0, z0, *s5_consts, tt=64, nb=bsz, strip=512, interleave=True)
            hp = _mm_res([o_att.reshape(bsz * seq, HP), o_s5], [wo_att_p, wo_s5], hp,
                         tm=tm_r, op_specs=[row_spec(tm_r, HP),
                                            pl.BlockSpec((tm_r, S5_WIDTH), lambda i: (i % nl_r, i // nl_r))])
            outs_p["lat"].append(ckv.reshape(bsz, seq, MLA_KV_LORA))
            outs_p["kr"].append(kr.reshape(bsz, seq, MLA_ROPE))
            outs_p["s5r"].append(hr.reshape(bsz, S5_GROUPS, S5_STATE))
            outs_p["s5i"].append(hi.reshape(bsz, S5_GROUPS, S5_STATE))

            m_s = dbs * ns
            q, k, v, ckv, kr, u, qk = _even_proj(
                hs, g_mix, wp, cqn, wuq, qg, ckvn, wkv, kg, tabs_s, tm=512,
                u_shape=(m_s, S5_WIDTH), u_spec=row_spec(512, S5_WIDTH), emit_qk=True)
            del q, k, v
            ckv3 = ckv.reshape(dbs, ns, MLA_KV_LORA)
            kr3 = kr.reshape(dbs, ns, MLA_ROPE)
            qk4 = qk.reshape(dbs, ns, MLA_HEADS, LANES)
            eye_h = jnp.eye(MLA_HEADS, dtype=BF16)
            ncols = ns * MLA_HEADS
            sub = LANES // MLA_HEADS
            nblk = MLA_NOPE // sub
            qn = (jnp.transpose(qk4[..., :MLA_NOPE], (0, 2, 3, 1))[..., None]
                  * eye_h[None, :, None, None, :])
            qn = qn.reshape(dbs, MLA_HEADS, nblk, sub, ncols).swapaxes(1, 2).reshape(dbs, MLA_HEADS * MLA_NOPE, ncols)
            qn = _pad_last(qn, LANES)
            wuk_p = (wuk_c.reshape(MLA_KV_LORA, MLA_HEADS, nblk, sub).swapaxes(1, 2)
                     .reshape(MLA_KV_LORA, MLA_HEADS * MLA_NOPE).astype(BF16))
            qr = jnp.transpose(qk4[..., MLA_NOPE:MLA_QK], (0, 3, 1, 2)).reshape(dbs, MLA_ROPE, ncols)
            qr = _pad_last(qr, LANES)
            colmask = (jnp.arange(LANES) < ncols)
            e16 = ((jnp.arange(LANES)[:, None] // sub == (jnp.arange(LANES)[None, :] % MLA_HEADS))
                   & colmask[None, :]).astype(BF16)
            onr = jnp.broadcast_to(colmask[None, :], (MLA_ROPE, LANES)).astype(BF16)
            zb = lambda r: jnp.zeros((dbs, r, LANES), BF16)
            bc = lambda x: jnp.broadcast_to(x[None], (dbs,) + x.shape)
            rhs2 = jnp.concatenate([
                jnp.concatenate([bc(e16), zb(LANES)], axis=2),
                jnp.concatenate([zb(MLA_ROPE), qr], axis=2),
                jnp.concatenate([bc(onr), zb(MLA_ROPE)], axis=2),
                jnp.zeros((dbs, LANES - 2 * MLA_ROPE, 2 * LANES), BF16)], axis=1)
            nnew = 16
            cnew = jnp.pad(ckv3, ((0, 0), (0, nnew - ns), (0, 0)))
            krnew = jnp.pad(kr3, ((0, 0), (0, nnew - ns), (0, 0)))
            o_att_s = _paged_attention(page_table, cache_mla_latent, jnp.swapaxes(cache_mla_krope, 2, 3), e,
                                       qn, rhs2, wuk_p, cnew, krnew, wuv_c.astype(BF16), npg=32, ngrp=8, nq=ns)
            u_tb = jnp.transpose(u.reshape(dbs, ns, S5_WIDTH)[:, :dseq], (1, 0, 2)).reshape(dseq * dbs, S5_WIDTH)
            o_s5, hr, hi = _s5(u_tb, state_s5_re[e].reshape(dbs, S5_NSTATE), state_s5_im[e].reshape(dbs, S5_NSTATE),
                               *s5_consts, tt=dseq, nb=dbs, strip=512, interleave=False)
            o_s5 = jnp.transpose(o_s5.reshape(dseq, dbs, S5_WIDTH), (1, 0, 2))
            o_s5 = jnp.pad(o_s5, ((0, 0), (0, ns - dseq), (0, 0))).reshape(m_s, S5_WIDTH)
            hs = _mm_res([o_att_s.reshape(m_s, MLA_HEADS * MLA_V), o_s5], [wo_att_c, wo_s5], hs, tm=m_s,
                         op_specs=[row_spec(m_s, MLA_HEADS * MLA_V), row_spec(m_s, S5_WIDTH)])
            outs_s["lat"].append(ckv3[:, :dseq])
            outs_s["kr"].append(kr3[:, :dseq])
            outs_s["s5r"].append(hr.reshape(dbs, S5_GROUPS, S5_STATE))
            outs_s["s5i"].append(hi.reshape(dbs, S5_GROUPS, S5_STATE))
        else:
            o = l // 2
            g_mix = row2(norm_mix[l])
            w_in = w_in_odd[o].astype(BF16)
            w_out = w_out_odd[o].astype(BF16)
            on = row2(hgrn_out_norm[o])
            lbp = hgrn_lower_bounds.astype(F32)

            proj = _norm_mm(hp, g_mix, w_in, tm=1024, tn=2048)
            s_zero = jnp.zeros((bsz, HGRN_HEADS, HGRN_DK, HGRN_DK), F32)
            og, st = _hgrn(proj.reshape(bsz, seq, 4 * D_MODEL), lbp, on, s_zero, chunk=64, nchunk=4, layer=l,
                           l_valid=None)
            hp = _mm_res([og.reshape(bsz * seq, D_MODEL)], [w_out], hp, tm=tm_r, op_specs=[row_spec(tm_r, D_MODEL)])
            outs_p["hg"].append(st)

            m_s = dbs * ns
            proj = _norm_mm(hs, g_mix, w_in, tm=m_s, tn=1024)
            lpad = HGRN_SUB
            proj = jnp.pad(proj.reshape(dbs, ns, 4 * D_MODEL), ((0, 0), (0, lpad - ns), (0, 0)))
            og, st = _hgrn(proj, lbp, on, state_hgrn[o], chunk=lpad, nchunk=1, layer=l, l_valid=dseq)
            hs = _mm_res([og[:, :ns].reshape(m_s, D_MODEL)], [w_out], hs, tm=m_s, op_specs=[row_spec(m_s, D_MODEL)])
            outs_s["hg"].append(st)

        g_mem = row2(norm_mem[l])
        wq = w_mem_q[l].astype(BF16)
        wo = w_mem_o[l].astype(BF16)
        mqg = row2(mem_q_gain[l])
        wkv_m = jnp.concatenate([w_mem_k[l], w_mem_v[l]], axis=1).astype(BF16)
        mk, mv = _mem_kv(mem_prompt.reshape(bsz * mem_len, D_MODEL), row2(norm_memsrc[l]), wkv_m,
                         row2(mem_k_gain[l]), tm=512)
        mk = mk.reshape(bsz, mem_len, MEM_WIDTH)
        mv = mv.reshape(bsz, mem_len, MEM_WIDTH)
        outs_p["mk"].append(mk.reshape(bsz, mem_len, MEM_HEADS, MEM_HEAD_DIM))
        outs_p["mv"].append(mv.reshape(bsz, mem_len, MEM_HEADS, MEM_HEAD_DIM))
        hp = _mem_attn(hp.reshape(bsz, seq, D_MODEL), g_mem, wq, mqg, mk, mv, wo,
                       nb=1, tl=512).reshape(bsz * seq, D_MODEL)
        pair_shape = (depth, dbs, mem_len // 2, 2 * MEM_HEADS, MEM_HEAD_DIM)
        hs = _mem_attn(hs.reshape(dbs, ns, D_MODEL), g_mem, wq, mqg, cache_mem_k.reshape(pair_shape),
                       cache_mem_v.reshape(pair_shape), wo, nb=8, tl=ns, layer=l).reshape(dbs * ns, D_MODEL)

        g_mlp = row2(norm_mlp[l])
        wu = w_mlp_up[l].astype(BF16)
        wd = w_mlp_down[l].astype(BF16)
        hp = _mlp(hp, g_mlp, wu, wd, tm=1024, tf=2048)
        hs = _mlp(hs, g_mlp, wu, wd, tm=dbs * ns, tf=1024)

    y_p = hp.reshape(bsz, seq, D_MODEL)
    y_s = hs.reshape(dbs, ns, D_MODEL)[:, :dseq]
    return (y_p, y_s,
            jnp.stack(outs_p["lat"], axis=1), jnp.stack(outs_p["kr"], axis=1),
            jnp.stack(outs_p["s5r"]), jnp.stack(outs_p["s5i"]), jnp.stack(outs_p["hg"]),
            jnp.stack(outs_p["mk"]), jnp.stack(outs_p["mv"]),
            jnp.stack(outs_s["lat"], axis=1), jnp.stack(outs_s["kr"], axis=1),
            jnp.stack(outs_s["s5r"]), jnp.stack(outs_s["s5i"]), jnp.stack(outs_s["hg"]))
```

```python
import functools
import math

import jax
import jax.numpy as jnp
from jax import lax
from jax.experimental import pallas as pl
from jax.experimental.pallas import tpu as pltpu

F32 = jnp.float32
BF16 = jnp.bfloat16

LANES = 128
VMEM_LIMIT_BYTES = 56 * 1024 * 1024

D_MODEL = 1024
MLA_HEADS = 8
MLA_NOPE = 64
MLA_ROPE = 32
MLA_QK = MLA_NOPE + MLA_ROPE
MLA_V = 64
MLA_Q_LORA = 768
MLA_KV_LORA = 256
ROPE_THETA = 10000.0
PAGE_SIZE = 128
S5_WIDTH = 512
S5_GROUP = 16
S5_GROUPS = S5_WIDTH // S5_GROUP
S5_STATE = 64
S5_NSTATE = S5_GROUPS * S5_STATE
HGRN_HEADS = 8
HGRN_DK = 128
HGRN_SUB = 32
HGRN_EXP_CLAMP = 80.0
MAX_SCORE_BOUND = 40.0
MEM_HEADS = 4
MEM_HEAD_DIM = 128
MEM_WIDTH = MEM_HEADS * MEM_HEAD_DIM
D_FF = 4 * D_MODEL
EPS = 1e-6
LOG2E = math.log2(math.e)
HP = MLA_HEADS * LANES


def _cparams(sem):
    return pltpu.CompilerParams(dimension_semantics=sem, vmem_limit_bytes=VMEM_LIMIT_BYTES)


def _rms(x, g):
    return x * lax.rsqrt(jnp.mean(x * x, axis=-1, keepdims=True) + EPS) * g


def _sigmoid(x):
    return 1.0 / (1.0 + jnp.exp(-x))


def _dot(a, b):
    return jnp.dot(a, b, preferred_element_type=F32)


def _dot_nt(a, b):
    return lax.dot_general(a, b, (((1,), (1,)), ((), ())), preferred_element_type=F32)


def _dot_tn(a, b):
    return lax.dot_general(a, b, (((0,), (0,)), ((), ())), preferred_element_type=F32)


def _row_to_col(row, n):
    r = lax.broadcasted_iota(jnp.int32, (n, n), 0)
    c = lax.broadcasted_iota(jnp.int32, (n, n), 1)
    return jnp.sum(jnp.where(r == c, jnp.broadcast_to(row, (n, n)), 0.0), axis=1, keepdims=True)


def _div_pow2(x, d):
    return lax.shift_right_logical(x, int(math.log2(d)))


def _full_spec(shape):
    nd = len(shape)
    return pl.BlockSpec(shape, lambda *_: (0,) * nd)


def row_spec(tm, width):
    return pl.BlockSpec((tm, width), lambda i: (i, 0))


def _even_proj_kernel(h_ref, g_ref, wp_ref, cqn_ref, wuq_ref, qg_ref, ckvn_ref, wkv_ref, kg_ref,
                      qaug_ref, kaug_ref, vaug_ref, cq_ref, s1q_ref, s2q_ref, ck_ref, s1k_ref, s2k_ref,
                      q_out, k_out, v_out, ckv_out, kr_out, u_out, *maybe_qk_out, nrg):
    o1 = MLA_Q_LORA
    o2 = o1 + MLA_KV_LORA
    o3 = o2 + LANES
    half = MLA_ROPE // 2
    inv_qk = 1.0 / MLA_QK
    qg = qg_ref[...]
    kg = kg_ref[...]
    tm = h_ref.shape[0]
    groups = [slice(tm // nrg * t, tm // nrg * (t + 1)) for t in range(nrg)]
    proj = [_dot(_rms(h_ref[r, :], g_ref[...]).astype(BF16), wp_ref[...]) for r in groups]
    qf, kv = [], []
    for r, p in zip(groups, proj):
        u_out[r, :] = p[:, o3:]
        qf.append(_dot(_rms(p[:, :o1], cqn_ref[...]).astype(BF16), wuq_ref[...]))
        ckv = _rms(p[:, o1:o2], ckvn_ref[...])
        ckv_out[r, :] = ckv
        kv.append(_dot(ckv.astype(BF16), wkv_ref[...]))
    for r, p, qfr, kvr in zip(groups, proj, qf, kv):
        kr = p[:, o2:o3]
        krr = (kr * ck_ref[r, :] + pltpu.roll(kr, half, 1) * s1k_ref[r, :]
               + pltpu.roll(kr, LANES - half, 1) * s2k_ref[r, :])
        kr_out[r, :] = krr[:, :MLA_ROPE]
        kr_sh = pltpu.roll(krr, MLA_NOPE, 1)
        cq_t, s1q_t, s2q_t = cq_ref[r, :], s1q_ref[r, :], s2q_ref[r, :]
        for h in range(MLA_HEADS):
            sl = slice(LANES * h, LANES * (h + 1))
            qh = qfr[:, sl]
            qh = qh * lax.rsqrt(jnp.sum(qh * qh, axis=-1, keepdims=True) * inv_qk + EPS) * qg
            qh = (qh * cq_t + pltpu.roll(qh, half, 1) * s1q_t + pltpu.roll(qh, LANES - half, 1) * s2q_t)
            q_out[r, sl] = (qh + qaug_ref[...]).astype(BF16)
            if maybe_qk_out:
                maybe_qk_out[0][r, sl] = (qh * kg).astype(BF16)
            kh = kvr[:, sl] + kr_sh
            kh = kh * lax.rsqrt(jnp.sum(kh * kh, axis=-1, keepdims=True) * inv_qk + EPS) * kg
            k_out[r, sl] = (kh + kaug_ref[...]).astype(BF16)
        v_out[:, r] = (kvr[:, HP:] + vaug_ref[...]).T.astype(BF16)


def _even_proj(h, g, wp, cqn, wuq, qg, ckvn, wkv, kg, aug, tabs, *, tm, u_shape, u_spec, emit_qk):
    m = h.shape[0]
    ltab = tabs[0].shape[0]
    ntab = ltab // tm
    row = lambda i: (i, 0)
    tab_spec = pl.BlockSpec((tm, LANES), lambda i: (i % ntab, 0))
    in_specs = ([pl.BlockSpec((tm, D_MODEL), row), _full_spec(g.shape), _full_spec(wp.shape),
                 _full_spec(cqn.shape), _full_spec(wuq.shape), _full_spec(qg.shape),
                 _full_spec(ckvn.shape), _full_spec(wkv.shape), _full_spec(kg.shape)]
                + [_full_spec(a.shape) for a in aug] + [tab_spec] * 6)
    out_shape = [jax.ShapeDtypeStruct((m, HP), BF16), jax.ShapeDtypeStruct((m, HP), BF16),
                 jax.ShapeDtypeStruct((m // tm, HP, tm), BF16), jax.ShapeDtypeStruct((m, MLA_KV_LORA), F32),
                 jax.ShapeDtypeStruct((m, MLA_ROPE), F32), jax.ShapeDtypeStruct(u_shape, F32)]
    out_specs = [pl.BlockSpec((tm, HP), row), pl.BlockSpec((tm, HP), row),
                 pl.BlockSpec((None, HP, tm), lambda i: (i, 0, 0)),
                 pl.BlockSpec((tm, MLA_KV_LORA), row), pl.BlockSpec((tm, MLA_ROPE), row),
                 u_spec]
    if emit_qk:
        out_shape.append(jax.ShapeDtypeStruct((m, HP), BF16))
        out_specs.append(pl.BlockSpec((tm, HP), row))
    return pl.pallas_call(
        functools.partial(_even_proj_kernel, nrg=2), grid=(m // tm,), in_specs=in_specs, out_specs=out_specs,
        out_shape=out_shape,
        compiler_params=_cparams(("arbitrary",)), name="even_proj",
    )(h, g, wp, cqn, wuq, qg, ckvn, wkv, kg, *aug, *tabs)


def _flash_kernel(bounded_ref, q_ref, k_ref, vt_ref, o_ref, *, tq, tk, hg):
    i = pl.program_id(1)
    nfull = (i * tq) // tk
    key = lax.broadcasted_iota(jnp.int32, (tk, tq), 0) + nfull * tk
    qry = lax.broadcasted_iota(jnp.int32, (tk, tq), 1) + i * tq
    causal = qry >= key

    heads = [slice(LANES * h, LANES * (h + 1)) for h in range(MLA_HEADS)]

    def accumulate(j, acc, masked):
        off = pl.multiple_of(j * tk, tk)
        out = []
        for h0 in range(0, MLA_HEADS, hg):
            grp = range(h0, h0 + hg)
            scores = [_dot_nt(k_ref[pl.ds(off, tk), heads[h]], q_ref[:, heads[h]]) for h in grp]
            if masked:
                scores = [jnp.where(causal, s, -jnp.inf) for s in scores]
            probs = [jnp.exp2(s).astype(BF16) for s in scores]
            out += [acc[h] + _dot(vt_ref[j, heads[h], :], p) for p, h in zip(probs, grp)]
        return tuple(out)

    @pl.when(bounded_ref[0] != 0)
    def _():
        acc = lax.fori_loop(0, nfull, lambda j, c: accumulate(j, c, False),
                            (jnp.zeros((LANES, tq), F32),) * MLA_HEADS)
        acc = accumulate(nfull, acc, True)
        for h, sl in enumerate(heads):
            o_ref[:, sl] = (acc[h] / acc[h][MLA_V:MLA_V + 1, :]).T.astype(BF16)

    def update(j, carry, masked):
        off = pl.multiple_of(j * tk, tk)
        out = []
        for h0 in range(0, MLA_HEADS, hg):
            grp = range(h0, h0 + hg)
            scores = [_dot_nt(k_ref[pl.ds(off, tk), heads[h]], q_ref[:, heads[h]]) for h in grp]
            probs, stats = [], []
            for h, s in zip(grp, scores):
                m, l = carry[3 * h], carry[3 * h + 1]
                if masked:
                    s = jnp.where(causal, s, -jnp.inf)
                m_new = jnp.maximum(m, jnp.max(s, axis=0, keepdims=True))
                alpha = jnp.exp2(m - m_new)
                p = jnp.exp2(s - m_new)
                stats.append((m_new, alpha * l + jnp.sum(p, axis=0, keepdims=True), alpha))
                probs.append(p.astype(BF16))
            pv = [_dot(vt_ref[j, heads[h], :], p) for p, h in zip(probs, grp)]
            for h, (m_new, l_new, alpha), o in zip(grp, stats, pv):
                out += [m_new, l_new, alpha * carry[3 * h + 2] + o]
        return tuple(out)

    @pl.when(bounded_ref[0] == 0)
    def _():
        init = (jnp.full((1, tq), -jnp.inf, F32), jnp.zeros((1, tq), F32), jnp.zeros((LANES, tq), F32))
        carry = lax.fori_loop(0, nfull, lambda j, c: update(j, c, False), init * MLA_HEADS)
        carry = update(nfull, carry, True)
        for h, sl in enumerate(heads):
            o_ref[:, sl] = (carry[3 * h + 2] / carry[3 * h + 1]).T.astype(BF16)


def _flash_attention(bounded, q, k, vt, *, tq, hg=2):
    b, l, _ = q.shape
    tk = vt.shape[2]
    nkb = l // tk
    grid_spec = pltpu.PrefetchScalarGridSpec(
        num_scalar_prefetch=1, grid=(b, l // tq),
        in_specs=[pl.BlockSpec((None, tq, HP), lambda bi, i, f: (bi, i, 0)),
                  pl.BlockSpec((None, l, HP), lambda bi, i, f: (bi, 0, 0)),
                  pl.BlockSpec((nkb, HP, tk), lambda bi, i, f: (bi, 0, 0))],
        out_specs=pl.BlockSpec((None, tq, HP), lambda bi, i, f: (bi, i, 0)))
    return pl.pallas_call(
        functools.partial(_flash_kernel, tq=tq, tk=tk, hg=hg), grid_spec=grid_spec,
        out_shape=jax.ShapeDtypeStruct((b, l, HP), BF16),
        compiler_params=_cparams(("arbitrary", "arbitrary")), name="prompt_attention",
    )(bounded, q, k, vt)


def _paged_kernel(pt_ref, *refs, npg, ngrp, nsteps, nq):
    lat_refs = refs[:npg]
    krt_refs = refs[npg:2 * npg]
    (qn_ref, rhs2_ref, wuk_ref, cnew_ref, krnew_ref, wuv_ref,
     o_ref, wabs, m_scr, l_scr, a_scr) = refs[2 * npg:]
    del pt_ref
    s = pl.program_id(1)
    nslots = nsteps * ngrp + 1
    ncol = LANES
    inv_qk = 1.0 / MLA_QK

    @pl.when(s == 0)
    def _():
        wabs[...] = _dot(wuk_ref[...], qn_ref[...]).astype(BF16)

    def stats(blocks, mask):
        kn = [_dot(c, wuk_ref[...]) for c, _ in blocks]
        sq = [k * k for k in kn]
        psum = [q[:, 0:LANES] + q[:, LANES:2 * LANES] + q[:, 2 * LANES:3 * LANES] + q[:, 3 * LANES:] for q in sq]
        r2 = [_dot(jnp.concatenate([p.astype(BF16), x], axis=1), rhs2_ref[...]) for p, (_, x) in zip(psum, blocks)]
        scn = [_dot(c, wabs[...]) for c, _ in blocks]
        probs, out = [], []
        for t in range(len(blocks)):
            sc = (scn[t] + r2[t][:, LANES:]) * lax.rsqrt(r2[t][:, :LANES] * inv_qk + EPS)
            if mask is not None:
                sc = jnp.where(mask, sc, -jnp.inf)
            m = jnp.max(sc, axis=0, keepdims=True)
            p = jnp.exp2(sc - m)
            out.append((m, jnp.sum(p, axis=0, keepdims=True)))
            probs.append(p.astype(BF16))
        acc = [_dot_tn(p, c) for p, (c, _) in zip(probs, blocks)]
        return [(m, l, a) for (m, l), a in zip(out, acc)]

    zpad = jnp.zeros((LANES - 2 * MLA_ROPE, PAGE_SIZE), F32)

    def rope_block(g):
        krt = krt_refs[g][...]
        return jnp.concatenate([krt, krt * krt, zpad], axis=0).T.astype(BF16)

    pg = npg // ngrp
    groups = [(jnp.concatenate([lat_refs[g][...].astype(BF16) for g in range(pg * t, pg * (t + 1))], axis=0),
               jnp.concatenate([rope_block(g) for g in range(pg * t, pg * (t + 1))], axis=0))
              for t in range(ngrp)]
    for t, (m, l, a) in enumerate(stats(groups, None)):
        slot = s * ngrp + t
        m_scr[pl.ds(slot, 1), :] = m
        l_scr[pl.ds(slot, 1), :] = l
        a_scr[slot] = a

    @pl.when(s == nsteps - 1)
    def _():
        nnew = cnew_ref.shape[0]
        krn = jnp.concatenate([krnew_ref[...], jnp.zeros((nnew, LANES - MLA_ROPE), F32)], axis=1)
        krn = krn + pltpu.roll(krn * krn, MLA_ROPE, 1)
        key = lax.broadcasted_iota(jnp.int32, (nnew, ncol), 0)
        qry = _div_pow2(lax.broadcasted_iota(jnp.int32, (nnew, ncol), 1), MLA_HEADS)
        (m2, l2, a2), = stats([(cnew_ref[...].astype(BF16), krn.astype(BF16))], key <= qry)
        m_scr[nslots - 1:nslots, :] = m2
        l_scr[nslots - 1:nslots, :] = l2
        a_scr[nslots - 1] = a2
        mall = m_scr[0:nslots, :]
        w = jnp.exp2(mall - jnp.max(mall, axis=0, keepdims=True))
        den = jnp.sum(l_scr[0:nslots, :] * w, axis=0, keepdims=True)
        wn = w / den
        num = jnp.zeros((ncol, MLA_KV_LORA), F32)
        for t in range(nslots):
            num = num + a_scr[t] * _row_to_col(wn[t:t + 1, :], ncol)
        full = _dot(num.astype(BF16), wuv_ref[...])
        hrow = lax.broadcasted_iota(jnp.int32, (MLA_HEADS, MLA_HEADS * MLA_V), 0)
        hcol = _div_pow2(lax.broadcasted_iota(jnp.int32, (MLA_HEADS, MLA_HEADS * MLA_V), 1), MLA_V)
        rows = []
        for qi in range(nq):
            blk = full[MLA_HEADS * qi:MLA_HEADS * (qi + 1), :]
            rows.append(jnp.sum(jnp.where(hrow == hcol, blk, 0.0), axis=0, keepdims=True))
        o_ref[...] = jnp.concatenate(rows, axis=0)


def _paged_attention(page_table, cache_lat, cache_krt, e, qn, rhs2, wuk, cnew, krnew, wuv, *, npg, ngrp, nq):
    nb, npages = page_table.shape
    nsteps = npages // npg
    nnew = cnew.shape[1]
    nslots = nsteps * ngrp + 1

    def page_spec(shape, g):
        return pl.BlockSpec((None, None) + shape, lambda b, s, pt: (pt[b, s * npg + g], e, 0, 0))

    per_b3 = lambda b, s, pt: (b, 0, 0)
    const2 = lambda b, s, pt: (0, 0)
    in_specs = ([page_spec((PAGE_SIZE, MLA_KV_LORA), g) for g in range(npg)]
                + [page_spec((MLA_ROPE, PAGE_SIZE), g) for g in range(npg)]
                + [pl.BlockSpec((None,) + qn.shape[1:], per_b3), pl.BlockSpec((None,) + rhs2.shape[1:], per_b3),
                   pl.BlockSpec(wuk.shape, const2),
                   pl.BlockSpec((None, nnew, MLA_KV_LORA), per_b3), pl.BlockSpec((None, nnew, MLA_ROPE), per_b3),
                   pl.BlockSpec(wuv.shape, const2)])
    grid_spec = pltpu.PrefetchScalarGridSpec(
        num_scalar_prefetch=1, grid=(nb, nsteps), in_specs=in_specs,
        out_specs=pl.BlockSpec((None, nq, MLA_HEADS * MLA_V), per_b3),
        scratch_shapes=[pltpu.VMEM((MLA_KV_LORA, LANES), BF16),
                        pltpu.VMEM((nslots, LANES), F32), pltpu.VMEM((nslots, LANES), F32),
                        pltpu.VMEM((nslots, LANES, MLA_KV_LORA), F32)])
    return pl.pallas_call(
        functools.partial(_paged_kernel, npg=npg, ngrp=ngrp, nsteps=nsteps, nq=nq),
        grid_spec=grid_spec, out_shape=jax.ShapeDtypeStruct((nb, nq, MLA_HEADS * MLA_V), F32),
        compiler_params=_cparams(("arbitrary", "arbitrary")), name="paged_attention",
    )(page_table, *([cache_lat] * npg), *([cache_krt] * npg), qn, rhs2, wuk, cnew, krnew, wuv)


def _s5_kernel(u_ref, h0r_ref, h0i_ref, lamr_ref, lami_ref, lstep_ref, brm_ref, bim_ref, crm_ref, cim_ref,
               d_ref, wg_ref, bg_ref, o_ref, hr_out, hi_out, xr_scr, xi_scr, hcr, hci, disc, io_scr, *,
               tt, nb, strip, interleave):
    c = pl.program_id(0)

    @pl.when(c == 0)
    def _():
        lr = jnp.minimum(lamr_ref[...], -1e-4)
        li = lami_ref[...]
        dt = jnp.exp(lstep_ref[...])
        mag = jnp.exp(lr * dt)
        abr = mag * jnp.cos(li * dt)
        abi = mag * jnp.sin(li * dt)
        den = lr * lr + li * li
        disc[0:1, :] = abr
        disc[1:2, :] = abi
        disc[2:3, :] = ((abr - 1.0) * lr + abi * li) / den
        disc[3:4, :] = (abi * lr - (abr - 1.0) * li) / den
        hcr[...] = h0r_ref[...]
        hci[...] = h0i_ref[...]

    nlb = S5_WIDTH // LANES
    if interleave:
        for b in range(nb):
            for j in range(nlb):
                c0 = S5_WIDTH * b + LANES * j
                io_scr[j, pl.ds(b, tt, stride=nb), :] = u_ref[:, c0:c0 + LANES]
        u = jnp.concatenate([io_scr[j] for j in range(nlb)], axis=1)
    else:
        u = u_ref[...]
    ub = u.astype(BF16)
    kc = 2 * LANES
    ks = kc * S5_STATE // S5_GROUP
    for k in range(S5_WIDTH // kc):
        cols = slice(kc * k, kc * (k + 1))
        sts = slice(ks * k, ks * (k + 1))
        pr = _dot(ub[:, cols], brm_ref[cols, sts])
        pi = _dot(ub[:, cols], bim_ref[cols, sts])
        cor = disc[2:3, sts]
        coi = disc[3:4, sts]
        xr_scr[:, sts] = cor * pr - coi * pi
        xi_scr[:, sts] = cor * pi + coi * pr

    for s0 in range(0, S5_NSTATE, strip):
        lanes = slice(s0, s0 + strip)
        ar = jnp.broadcast_to(disc[0:1, lanes], (nb, strip))
        ai = jnp.broadcast_to(disc[1:2, lanes], (nb, strip))

        def step(t, carry, lanes=lanes, ar=ar, ai=ai):
            hr, hi = carry
            rows = pl.ds(pl.multiple_of(t * nb, nb), nb)
            nr = ar * hr - ai * hi + xr_scr[rows, lanes]
            ni = ar * hi + ai * hr + xi_scr[rows, lanes]
            xr_scr[rows, lanes] = nr
            xi_scr[rows, lanes] = ni
            return nr, ni

        hr, hi = lax.fori_loop(0, tt, step, (hcr[:, lanes], hci[:, lanes]))
        hcr[:, lanes] = hr
        hci[:, lanes] = hi

    ys = []
    for k in range(S5_WIDTH // kc):
        cols = slice(kc * k, kc * (k + 1))
        sts = slice(ks * k, ks * (k + 1))
        ys.append(_dot(xr_scr[:, sts].astype(BF16), crm_ref[sts, cols])
                  - _dot(xi_scr[:, sts].astype(BF16), cim_ref[sts, cols]))
    y = jnp.concatenate(ys, axis=1) + d_ref[...] * u
    z = jax.nn.gelu(y)
    gate = _sigmoid(_dot(z.astype(BF16), wg_ref[...]) + bg_ref[...])
    if interleave:
        out = z * gate
        for j in range(nlb):
            io_scr[j] = out[:, LANES * j:LANES * (j + 1)]
        for b in range(nb):
            for j in range(nlb):
                c0 = S5_WIDTH * b + LANES * j
                o_ref[:, c0:c0 + LANES] = io_scr[j, pl.ds(b, tt, stride=nb), :].astype(BF16)
    else:
        o_ref[...] = (z * gate).astype(BF16)

    @pl.when(c == pl.num_programs(0) - 1)
    def _():
        hr_out[...] = hcr[...]
        hi_out[...] = hci[...]


def _s5(u, h0r, h0i, lamr, lami, lstep, brm, bim, crm, cim, d, wg, bg, *, tt, nb, strip, interleave):
    steps = u.shape[0] if interleave else u.shape[0] // nb
    blk = tt * nb
    consts = (h0r, h0i, lamr, lami, lstep, brm, bim, crm, cim, d, wg, bg)
    io_spec = row_spec(tt, nb * S5_WIDTH) if interleave else row_spec(blk, S5_WIDTH)
    return pl.pallas_call(
        functools.partial(_s5_kernel, tt=tt, nb=nb, strip=strip, interleave=interleave),
        grid=(steps // tt,),
        in_specs=[io_spec] + [_full_spec(a.shape) for a in consts],
        out_specs=[io_spec, _full_spec((nb, S5_NSTATE)), _full_spec((nb, S5_NSTATE))],
        out_shape=[jax.ShapeDtypeStruct(u.shape, BF16),
                   jax.ShapeDtypeStruct((nb, S5_NSTATE), F32), jax.ShapeDtypeStruct((nb, S5_NSTATE), F32)],
        scratch_shapes=[pltpu.VMEM((blk, S5_NSTATE), F32), pltpu.VMEM((blk, S5_NSTATE), F32),
                        pltpu.VMEM((nb, S5_NSTATE), F32), pltpu.VMEM((nb, S5_NSTATE), F32),
                        pltpu.VMEM((8, S5_NSTATE), F32), pltpu.VMEM((S5_WIDTH // LANES, blk, LANES), F32)],
        compiler_params=_cparams(("arbitrary",)), name="s5",
    )(u, *consts)


def _mm_res_kernel(*refs, nop):
    res_ref = refs[2 * nop]
    o_ref = refs[2 * nop + 1]
    acc = res_ref[...]
    for t in range(nop):
        acc = acc + _dot(refs[t][...].astype(BF16), refs[nop + t][...])
    o_ref[...] = acc


def _mm_res(ops, ws, res, *, tm, op_specs):
    m, n = res.shape
    row = lambda i: (i, 0)
    in_specs = list(op_specs) + [_full_spec(w.shape) for w in ws] + [pl.BlockSpec((tm, n), row)]
    return pl.pallas_call(
        functools.partial(_mm_res_kernel, nop=len(ops)), grid=(m // tm,), in_specs=in_specs,
        out_specs=pl.BlockSpec((tm, n), row), out_shape=jax.ShapeDtypeStruct((m, n), F32),
        compiler_params=_cparams(("arbitrary",)), name="matmul_residual",
    )(*ops, *ws, res)


def _mem_kv_kernel(x_ref, g_ref, w_ref, kg_ref, k_out, v_out):
    mn = _rms(x_ref[...], g_ref[...]).astype(BF16)
    kv = _dot(mn, w_ref[...])
    kg = kg_ref[...]
    for h in range(MEM_HEADS):
        sl = slice(LANES * h, LANES * (h + 1))
        k_out[:, sl] = _rms(kv[:, sl], kg)
    v_out[...] = kv[:, MEM_WIDTH:]


def _mem_kv(x, g, w, kg, *, tm):
    m = x.shape[0]
    row = lambda i: (i, 0)
    return pl.pallas_call(
        _mem_kv_kernel, grid=(m // tm,),
        in_specs=[pl.BlockSpec((tm, D_MODEL), row), _full_spec(g.shape), _full_spec(w.shape), _full_spec(kg.shape)],
        out_specs=[pl.BlockSpec((tm, MEM_WIDTH), row), pl.BlockSpec((tm, MEM_WIDTH), row)],
        out_shape=[jax.ShapeDtypeStruct((m, MEM_WIDTH), F32), jax.ShapeDtypeStruct((m, MEM_WIDTH), F32)],
        compiler_params=_cparams(("arbitrary",)), name="mem_kv",
    )(x, g, w, kg)


def _mem_attn_kernel(h_ref, g_ref, wq_ref, qg_ref, mk_ref, mv_ref, wo_ref, o_ref, *, nb, tl, paired):
    x = h_ref[...].reshape(nb * tl, D_MODEL)
    hn = _rms(x, g_ref[...]).astype(BF16)
    q = _dot(hn, wq_ref[...])
    qg = qg_ref[...] * (MEM_HEAD_DIM ** -0.5 * LOG2E)
    qn = [_rms(q[:, LANES * h:LANES * (h + 1)], qg).astype(BF16) for h in range(MEM_HEADS)]

    def head_block(ref, b, h):
        if paired:
            return jnp.concatenate([ref[b, :, h, :], ref[b, :, MEM_HEADS + h, :]], axis=0).astype(BF16)
        return ref[b, :, LANES * h:LANES * (h + 1)].astype(BF16)

    pairs = [(b, h) for b in range(nb) for h in range(MEM_HEADS)]
    scores = [_dot_nt(qn[h][tl * b:tl * (b + 1), :], head_block(mk_ref, b, h)) for b, h in pairs]
    probs = [jnp.exp2(s - jnp.max(s, axis=-1, keepdims=True)) for s in scores]
    outs = [_dot(p.astype(BF16), head_block(mv_ref, b, h)) / jnp.sum(p, axis=-1, keepdims=True)
            for p, (b, h) in zip(probs, pairs)]
    rows = [jnp.concatenate(outs[MEM_HEADS * b:MEM_HEADS * (b + 1)], axis=1) for b in range(nb)]
    o = (rows[0] if nb == 1 else jnp.concatenate(rows, axis=0)).astype(BF16)
    o_ref[...] = (x + _dot(o, wo_ref[...])).reshape(nb, tl, D_MODEL)


def _mem_attn(h, g, wq, qg, mk, mv, wo, *, nb, tl, layer=None):
    b, l, _ = h.shape
    blk = lambda bi, i: (bi, i, 0)
    if layer is None:
        mem_spec = pl.BlockSpec((nb,) + mk.shape[1:], lambda bi, i: (bi, 0, 0))
    else:
        mem_spec = pl.BlockSpec((None, nb) + mk.shape[2:], lambda bi, i: (layer, bi, 0, 0, 0))
    return pl.pallas_call(
        functools.partial(_mem_attn_kernel, nb=nb, tl=tl, paired=layer is not None), grid=(b // nb, l // tl),
        in_specs=[pl.BlockSpec((nb, tl, D_MODEL), blk), _full_spec(g.shape), _full_spec(wq.shape),
                  _full_spec(qg.shape), mem_spec, mem_spec, _full_spec(wo.shape)],
        out_specs=pl.BlockSpec((nb, tl, D_MODEL), blk),
        out_shape=jax.ShapeDtypeStruct(h.shape, F32),
        compiler_params=_cparams(("arbitrary", "arbitrary")), name="mem_attention",
    )(h, g, wq, qg, mk, mv, wo)


def _mlp_kernel(h_ref, g_ref, wu_ref, wd_ref, o_ref, xn_scr, acc_scr):
    j = pl.program_id(1)

    @pl.when(j == 0)
    def _():
        xn_scr[...] = _rms(h_ref[...], g_ref[...]).astype(BF16)
        acc_scr[...] = jnp.zeros(acc_scr.shape, F32)

    a = _dot(xn_scr[...], wu_ref[...])
    a = jnp.square(jnp.maximum(a, 0.0)).astype(BF16)
    acc_scr[...] += _dot(a, wd_ref[...])

    @pl.when(j == pl.num_programs(1) - 1)
    def _():
        o_ref[...] = h_ref[...] + acc_scr[...]


def _mlp(h, g, wu, wd, *, tm, tf):
    m = h.shape[0]
    return pl.pallas_call(
        _mlp_kernel, grid=(m // tm, D_FF // tf),
        in_specs=[pl.BlockSpec((tm, D_MODEL), lambda i, j: (i, 0)), _full_spec(g.shape),
                  pl.BlockSpec((D_MODEL, tf), lambda i, j: (0, j)), pl.BlockSpec((tf, D_MODEL), lambda i, j: (j, 0))],
        out_specs=pl.BlockSpec((tm, D_MODEL), lambda i, j: (i, 0)),
        out_shape=jax.ShapeDtypeStruct((m, D_MODEL), F32),
        scratch_shapes=[pltpu.VMEM((tm, D_MODEL), BF16), pltpu.VMEM((tm, D_MODEL), F32)],
        compiler_params=_cparams(("arbitrary", "arbitrary")), name="mlp",
    )(h, g, wu, wd)


def _norm_mm_kernel(h_ref, g_ref, w_ref, o_ref, xn_scr):
    @pl.when(pl.program_id(1) == 0)
    def _():
        xn_scr[...] = _rms(h_ref[...], g_ref[...]).astype(BF16)

    o_ref[...] = _dot(xn_scr[...], w_ref[...])


def _norm_mm(h, g, w, *, tm, tn):
    m = h.shape[0]
    n = w.shape[1]
    return pl.pallas_call(
        _norm_mm_kernel, grid=(m // tm, n // tn),
        in_specs=[pl.BlockSpec((tm, D_MODEL), lambda i, j: (i, 0)), _full_spec(g.shape),
                  pl.BlockSpec((D_MODEL, tn), lambda i, j: (0, j))],
        out_specs=pl.BlockSpec((tm, tn), lambda i, j: (i, j)),
        out_shape=jax.ShapeDtypeStruct((m, n), F32),
        scratch_shapes=[pltpu.VMEM((tm, D_MODEL), BF16)],
        compiler_params=_cparams(("arbitrary", "arbitrary")), name="norm_matmul",
    )(h, g, w)


def _hgrn_kernel(q_ref, f_ref, i_ref, g_ref, lbp_ref, on_ref, s0_ref, o_ref, s_out, s_scr, *,
                 chunk, nchunk, layer, l_valid):
    c = pl.program_id(1)
    tb = chunk * nchunk

    @pl.when(c == 0)
    def _():
        s_scr[...] = s0_ref[...]

    lbp = lbp_ref[...]
    e = jnp.exp(lbp - jnp.max(lbp, axis=0, keepdims=True))
    sm = e / jnp.sum(e, axis=0, keepdims=True)
    lb = jnp.sum(sm[0:layer + 1, :], axis=0, keepdims=True) - sm[0:1, :]

    q = q_ref[...]
    qa = q * _sigmoid(q)
    fg = lb + (1.0 - lb) * _sigmoid(f_ref[...])
    logf = jnp.log(fg)
    kk = 1.0 - fg
    v = i_ref[...]
    if l_valid is not None:
        valid = (lax.broadcasted_iota(jnp.int32, (tb, 1), 0) + c * tb) < l_valid
        logf = jnp.where(valid, logf, 0.0)
        kk = jnp.where(valid, kk, 0.0)
    vb = v.astype(BF16)

    tr = lax.broadcasted_iota(jnp.int32, (tb, tb), 0)
    tc = lax.broadcasted_iota(jnp.int32, (tb, tb), 1)
    same_chunk = _div_pow2(tr, chunk) == _div_pow2(tc, chunk)
    tri = jnp.where(same_chunk, jnp.where(tr >= tc, 1.0, 0.0), 0.0).astype(BF16)
    hi = logf.astype(BF16)
    lo = (logf - hi.astype(F32)).astype(BF16)
    bcum = _dot(tri, hi) + _dot(tri, lo)
    qhat = (qa * jnp.exp(bcum)).astype(BF16)

    nsub = chunk // HGRN_SUB
    khat, dec, qloc, kloc, masks = [], [], [], [], []
    spread = jnp.zeros((1, bcum.shape[1]), F32)
    for ci in range(nchunk):
        c0 = ci * chunk
        blast = bcum[c0 + chunk - 1:c0 + chunk, :]
        khat.append((kk[c0:c0 + chunk, :] * jnp.exp(blast - bcum[c0:c0 + chunk, :])).astype(BF16))
        dec.append(jnp.exp(blast))
        for i in range(nsub):
            r0 = c0 + i * HGRN_SUB
            r1 = r0 + HGRN_SUB
            base = bcum[r0 - 1:r0, :] if i > 0 else jnp.zeros((1, bcum.shape[1]), F32)
            spread = jnp.minimum(spread, bcum[r1 - 1:r1, :] - base)
            qloc.append((qa[r0:r1, :] * jnp.exp(bcum[r0:r1, :] - base)).astype(BF16))
            kloc.append((kk[c0:r1, :] * jnp.exp(jnp.minimum(base - bcum[c0:r1, :], HGRN_EXP_CLAMP))).astype(BF16))
    wild = jnp.min(spread) < -HGRN_EXP_CLAMP
    for i in range(nsub):
        ncols = (i + 1) * HGRN_SUB
        ar = lax.broadcasted_iota(jnp.int32, (HGRN_SUB, ncols), 0) + i * HGRN_SUB
        ac = lax.broadcasted_iota(jnp.int32, (HGRN_SUB, ncols), 1)
        in_block = ac >= i * HGRN_SUB
        masks.append(jnp.logical_and(ar >= ac, jnp.logical_not(jnp.logical_and(wild, in_block))))

    hsl = [slice(HGRN_DK * h, HGRN_DK * (h + 1)) for h in range(HGRN_HEADS)]

    def in_block_exact():
        pos = jnp.bitwise_and(lax.broadcasted_iota(jnp.int32, (tb, 1), 0), HGRN_SUB - 1)
        out = jnp.zeros((tb, bcum.shape[1]), F32)
        for j in range(HGRN_SUB):
            ok = pos >= j
            kj, bj, vj = (kk, bcum, v) if j == 0 else (pltpu.roll(x, j, 0) for x in (kk, bcum, v))
            e = jnp.where(ok, qa * kj * jnp.exp(jnp.where(ok, bcum - bj, 0.0)), 0.0)
            out = out + jnp.concatenate(
                [jnp.sum(e[:, sl], axis=-1, keepdims=True) * vj[:, sl] for sl in hsl], axis=1)
        return out

    blocks = [(ci, i) for ci in range(nchunk) for i in range(nsub)]
    att = [[_dot_nt(qloc[ci * nsub + i][:, sl], kloc[ci * nsub + i][:, sl]) for ci, i in blocks] for sl in hsl]
    att = [[jnp.where(masks[i], a, 0.0).astype(BF16) for a, (ci, i) in zip(row, blocks)] for row in att]
    intra = [[_dot(a, vb[ci * chunk:ci * chunk + (i + 1) * HGRN_SUB, sl]) for a, (ci, i) in zip(row, blocks)]
             for row, sl in zip(att, hsl)]
    kv = [[_dot_tn(khat[ci][:, sl], vb[ci * chunk:(ci + 1) * chunk, sl]) for ci in range(nchunk)] for sl in hsl]
    dcol = [[_row_to_col(dec[ci][:, sl], HGRN_DK) for ci in range(nchunk)] for sl in hsl]
    st = [s_scr[h] for h in range(HGRN_HEADS)]
    inter = [[] for _ in hsl]
    for ci in range(nchunk):
        rows = slice(ci * chunk, (ci + 1) * chunk)
        for h, sl in enumerate(hsl):
            inter[h].append(_dot(qhat[rows, sl], st[h].astype(BF16)))
        for h in range(HGRN_HEADS):
            st[h] = dcol[h][ci] * st[h] + kv[h][ci]
    o_heads = []
    for h in range(HGRN_HEADS):
        s_scr[h] = st[h]
        parts = [inter[h][ci][i * HGRN_SUB:(i + 1) * HGRN_SUB, :] + intra[h][ci * nsub + i] for ci, i in blocks]
        o_heads.append(parts[0] if len(parts) == 1 else jnp.concatenate(parts, axis=0))

    o = jnp.concatenate(o_heads, axis=1)
    o = lax.cond(wild, lambda: o + in_block_exact(), lambda: o)
    g = g_ref[...]
    o_ref[...] = (_rms(o, on_ref[...]) * (g * _sigmoid(g))).astype(BF16)

    @pl.when(c == pl.num_programs(1) - 1)
    def _():
        s_out[...] = s_scr[...]


def _hgrn(proj, lbp, on, s0, *, chunk, nchunk, layer, l_valid):
    b, l, _ = proj.shape
    w = D_MODEL
    tb = chunk * nchunk

    def col(k):
        return pl.BlockSpec((None, tb, w), lambda bi, c: (bi, c, k))

    st_spec = pl.BlockSpec((None, HGRN_HEADS, HGRN_DK, HGRN_DK), lambda bi, c: (bi, 0, 0, 0))
    return pl.pallas_call(
        functools.partial(_hgrn_kernel, chunk=chunk, nchunk=nchunk, layer=layer, l_valid=l_valid),
        grid=(b, l // tb),
        in_specs=[col(0), col(1), col(2), col(3), _full_spec(lbp.shape), _full_spec(on.shape), st_spec],
        out_specs=[pl.BlockSpec((None, tb, w), lambda bi, c: (bi, c, 0)), st_spec],
        out_shape=[jax.ShapeDtypeStruct((b, l, w), BF16), jax.ShapeDtypeStruct(s0.shape, F32)],
        scratch_shapes=[pltpu.VMEM((HGRN_HEADS, HGRN_DK, HGRN_DK), F32)],
        compiler_params=_cparams(("arbitrary", "arbitrary")), name="hgrn",
    )(proj, proj, proj, proj, lbp, on, s0)


def _pad_last(x, n):
    return jnp.pad(x, [(0, 0)] * (x.ndim - 1) + [(0, n - x.shape[-1])])


def _head_pad(w, per):
    k = w.shape[0]
    return _pad_last(w.reshape(k, -1, per), LANES).reshape(k, -1)


def _rope_tables(pos):
    half = MLA_ROPE // 2
    inv = ROPE_THETA ** (-jnp.arange(half, dtype=F32) / half)
    ang = pos.astype(F32)[:, None] * inv[None, :]
    cos, sin = jnp.cos(ang), jnp.sin(ang)
    n = pos.shape[0]
    z = lambda w: jnp.zeros((n, w), F32)
    scale = MLA_QK ** -0.5 * LOG2E
    cq = scale * jnp.concatenate([jnp.ones((n, MLA_NOPE), F32), cos, cos, z(LANES - MLA_QK)], axis=1)
    s1q = scale * jnp.concatenate([z(MLA_NOPE + half), sin, z(LANES - MLA_QK)], axis=1)
    s2q = scale * jnp.concatenate([z(MLA_NOPE), -sin, z(half + LANES - MLA_QK)], axis=1)
    ck = jnp.concatenate([cos, cos, z(LANES - MLA_ROPE)], axis=1)
    s1k = jnp.concatenate([z(half), sin, z(LANES - MLA_ROPE)], axis=1)
    s2k = jnp.concatenate([-sin, z(LANES - half)], axis=1)
    return (cq, s1q, s2q, ck, s1k, s2k)


def _block_diag(x):
    g, a, b = x.shape
    eye = jnp.eye(g, dtype=x.dtype)
    return (x[:, :, None, :] * eye[:, None, :, None]).reshape(g * a, g * b)


def kernel(x_prompt, x_sample, cache_mla_latent, cache_mla_krope, state_s5_re, state_s5_im, state_hgrn, cache_mem_k, cache_mem_v, page_table, mem_prompt, norm_mix, norm_mem, norm_memsrc, norm_mlp, w_mem_q, w_mem_k, w_mem_v, w_mem_o, mem_q_gain, mem_k_gain, w_mlp_up, w_mlp_down, w_in_even, mla_cq_norm, mla_ckv_norm, w_mla_uq, w_mla_ukv, mla_qn_nope, mla_qn_rope, mla_kn_nope, mla_kn_rope, s5_lambda_re, s5_lambda_im, s5_log_step, s5_b_re, s5_b_im, s5_c_re, s5_c_im, s5_d, s5_w_glu, s5_b_glu, w_out_even, w_in_odd, hgrn_lower_bounds, hgrn_out_norm, w_out_odd):
    bsz, seq, _ = x_prompt.shape
    dbs, dseq, _ = x_sample.shape
    depth = norm_mix.shape[0]
    past_len = page_table.shape[1] * PAGE_SIZE
    ns = 8
    mem_len = mem_prompt.shape[1]
    row2 = lambda a: a.reshape(1, -1).astype(F32)

    hp = x_prompt.reshape(bsz * seq, D_MODEL)
    hs = jnp.pad(x_sample, ((0, 0), (0, ns - dseq), (0, 0))).reshape(dbs * ns, D_MODEL)

    tabs_p = _rope_tables(jnp.arange(seq, dtype=jnp.int32))
    pos_s = past_len + jnp.arange(ns, dtype=jnp.int32)
    tabs_s = tuple(jnp.tile(t, (dbs, 1)) for t in _rope_tables(pos_s))

    outs_p = {k: [] for k in ("lat", "kr", "s5r", "s5i", "hg", "mk", "mv")}
    outs_s = {k: [] for k in ("lat", "kr", "s5r", "s5i", "hg")}

    tm_p = 512
    nl_p = seq // tm_p
    tm_r = 1024
    nl_r = seq // tm_r

    for l in range(depth):
        if l % 2 == 0:
            e = l // 2
            w_in = w_in_even[e]
            o1 = MLA_Q_LORA + MLA_KV_LORA
            wp = jnp.concatenate([w_in[:, :o1], _pad_last(w_in[:, o1:o1 + MLA_ROPE], LANES),
                                  w_in[:, o1 + MLA_ROPE:]], axis=1).astype(BF16)
            wuq = _head_pad(w_mla_uq[e], MLA_QK).astype(BF16)
            ukv = w_mla_ukv[e].reshape(MLA_KV_LORA, MLA_HEADS, MLA_NOPE + MLA_V)
            wuk_c = ukv[:, :, :MLA_NOPE].reshape(MLA_KV_LORA, -1)
            wuv_c = ukv[:, :, MLA_NOPE:].reshape(MLA_KV_LORA, -1)
            wkv = jnp.concatenate([_head_pad(wuk_c, MLA_NOPE), _head_pad(wuv_c, MLA_V)], axis=1).astype(BF16)
            qg = _pad_last(jnp.concatenate([mla_qn_nope[e], mla_qn_rope[e], mla_qn_rope[e]])[None, :], LANES)
            kg = _pad_last(jnp.concatenate([mla_kn_nope[e], mla_kn_rope[e], mla_kn_rope[e]])[None, :], LANES)
            cqn = row2(mla_cq_norm[e])
            ckvn = row2(mla_ckv_norm[e])
            g_mix = row2(norm_mix[l])
            score_bound = MLA_QK ** 0.5 * LOG2E * jnp.max(jnp.abs(qg)) * jnp.max(jnp.abs(kg))
            bounded = (score_bound <= MAX_SCORE_BOUND).astype(jnp.int32).reshape(1)
            lane = jnp.arange(LANES)
            aug = (jnp.where(lane == MLA_QK, 1.0, 0.0).astype(F32)[None, :],
                   jnp.where(lane == MLA_QK, -score_bound, 0.0).astype(F32)[None, :],
                   jnp.tile(jnp.where(lane == MLA_V, 1.0, 0.0).astype(F32), MLA_HEADS)[None, :])

            brm = _block_diag(jnp.swapaxes(s5_b_re[e], 1, 2)).astype(BF16)
            bim = _block_diag(jnp.swapaxes(s5_b_im[e], 1, 2)).astype(BF16)
            crm = _block_diag(jnp.swapaxes(s5_c_re[e], 1, 2)).astype(BF16)
            cim = _block_diag(jnp.swapaxes(s5_c_im[e], 1, 2)).astype(BF16)
            lamr = row2(s5_lambda_re[e])
            lami = row2(s5_lambda_im[e])
            lstep = row2(jnp.repeat(s5_log_step[e], S5_STATE))
            s5_consts = (lamr, lami, lstep, brm, bim, crm, cim, row2(s5_d[e]), s5_w_glu[e].astype(BF16),
                         row2(s5_b_glu[e]))
            w_out = w_out_even[e]
            wo_att_c = w_out[:MLA_HEADS * MLA_V].astype(BF16)
            wo_att_p = _pad_last(w_out[:MLA_HEADS * MLA_V].reshape(MLA_HEADS, MLA_V, D_MODEL).swapaxes(1, 2),
                                 LANES).swapaxes(1, 2).reshape(HP, D_MODEL).astype(BF16)
            wo_s5 = w_out[MLA_HEADS * MLA_V:].astype(BF16)

            q, k, v, ckv, kr, u = _even_proj(
                hp, g_mix, wp, cqn, wuq, qg, ckvn, wkv, kg, aug, tabs_p, tm=tm_p,
                u_shape=(seq, bsz * S5_WIDTH), emit_qk=False,
                u_spec=pl.BlockSpec((tm_p, S5_WIDTH), lambda i: (i % nl_p, i // nl_p)))
            o_att = _flash_attention(bounded, q.reshape(bsz, seq, HP), k.reshape(bsz, seq, HP), v, tq=512, hg=4)
            z0 = jnp.zeros((bsz, S5_NSTATE), F32)
            o_s5, hr, hi = _s5(u, z0, z0, *s5_consts, tt=64, nb=bsz, strip=512, interleave=True)
            hp = _mm_res([o_att.reshape(bsz * seq, HP), o_s5], [wo_att_p, wo_s5], hp,
                         tm=tm_r, op_specs=[row_spec(tm_r, HP),
                                            pl.BlockSpec((tm_r, S5_WIDTH), lambda i: (i % nl_r, i // nl_r))])
            outs_p["lat"].append(ckv.reshape(bsz, seq, MLA_KV_LORA))
            outs_p["kr"].append(kr.reshape(bsz, seq, MLA_ROPE))
            outs_p["s5r"].append(hr.reshape(bsz, S5_GROUPS, S5_STATE))
            outs_p["s5i"].append(hi.reshape(bsz, S5_GROUPS, S5_STATE))

            m_s = dbs * ns
            q, k, v, ckv, kr, u, qk = _even_proj(
                hs, g_mix, wp, cqn, wuq, qg, ckvn, wkv, kg, aug, tabs_s, tm=512,
                u_shape=(m_s, S5_WIDTH), u_spec=row_spec(512, S5_WIDTH), emit_qk=True)
            del q, k, v
            ckv3 = ckv.reshape(dbs, ns, MLA_KV_LORA)
            kr3 = kr.reshape(dbs, ns, MLA_ROPE)
            qk4 = qk.reshape(dbs, ns, MLA_HEADS, LANES)
            eye_h = jnp.eye(MLA_HEADS, dtype=BF16)
            ncols = ns * MLA_HEADS
            sub = LANES // MLA_HEADS
            nblk = MLA_NOPE // sub
            qn = (jnp.transpose(qk4[..., :MLA_NOPE], (0, 2, 3, 1))[..., None]
                  * eye_h[None, :, None, None, :])
            qn = qn.reshape(dbs, MLA_HEADS, nblk, sub, ncols).swapaxes(1, 2).reshape(dbs, MLA_HEADS * MLA_NOPE, ncols)
            qn = _pad_last(qn, LANES)
            wuk_p = (wuk_c.reshape(MLA_KV_LORA, MLA_HEADS, nblk, sub).swapaxes(1, 2)
                     .reshape(MLA_KV_LORA, MLA_HEADS * MLA_NOPE).astype(BF16))
            qr = jnp.transpose(qk4[..., MLA_NOPE:MLA_QK], (0, 3, 1, 2)).reshape(dbs, MLA_ROPE, ncols)
            qr = _pad_last(qr, LANES)
            colmask = (jnp.arange(LANES) < ncols)
            e16 = ((jnp.arange(LANES)[:, None] // sub == (jnp.arange(LANES)[None, :] % MLA_HEADS))
                   & colmask[None, :]).astype(BF16)
            onr = jnp.broadcast_to(colmask[None, :], (MLA_ROPE, LANES)).astype(BF16)
            zb = lambda r: jnp.zeros((dbs, r, LANES), BF16)
            bc = lambda x: jnp.broadcast_to(x[None], (dbs,) + x.shape)
            rhs2 = jnp.concatenate([
                jnp.concatenate([bc(e16), zb(LANES)], axis=2),
                jnp.concatenate([zb(MLA_ROPE), qr], axis=2),
                jnp.concatenate([bc(onr), zb(MLA_ROPE)], axis=2),
                jnp.zeros((dbs, LANES - 2 * MLA_ROPE, 2 * LANES), BF16)], axis=1)
            nnew = 16
            cnew = jnp.pad(ckv3, ((0, 0), (0, nnew - ns), (0, 0)))
            krnew = jnp.pad(kr3, ((0, 0), (0, nnew - ns), (0, 0)))
            o_att_s = _paged_attention(page_table, cache_mla_latent, jnp.swapaxes(cache_mla_krope, 2, 3), e,
                                       qn, rhs2, wuk_p, cnew, krnew, wuv_c.astype(BF16), npg=32, ngrp=4, nq=ns)
            u_tb = jnp.transpose(u.reshape(dbs, ns, S5_WIDTH)[:, :dseq], (1, 0, 2)).reshape(dseq * dbs, S5_WIDTH)
            o_s5, hr, hi = _s5(u_tb, state_s5_re[e].reshape(dbs, S5_NSTATE), state_s5_im[e].reshape(dbs, S5_NSTATE),
                               *s5_consts, tt=dseq, nb=dbs, strip=512, interleave=False)
            o_s5 = jnp.transpose(o_s5.reshape(dseq, dbs, S5_WIDTH), (1, 0, 2))
            o_s5 = jnp.pad(o_s5, ((0, 0), (0, ns - dseq), (0, 0))).reshape(m_s, S5_WIDTH)
            hs = _mm_res([o_att_s.reshape(m_s, MLA_HEADS * MLA_V), o_s5], [wo_att_c, wo_s5], hs, tm=m_s,
                         op_specs=[row_spec(m_s, MLA_HEADS * MLA_V), row_spec(m_s, S5_WIDTH)])
            outs_s["lat"].append(ckv3[:, :dseq])
            outs_s["kr"].append(kr3[:, :dseq])
            outs_s["s5r"].append(hr.reshape(dbs, S5_GROUPS, S5_STATE))
            outs_s["s5i"].append(hi.reshape(dbs, S5_GROUPS, S5_STATE))
        else:
            o = l // 2
            g_mix = row2(norm_mix[l])
            w_in = w_in_odd[o].astype(BF16)
            w_out = w_out_odd[o].astype(BF16)
            on = row2(hgrn_out_norm[o])
            lbp = hgrn_lower_bounds.astype(F32)

            proj = _norm_mm(hp, g_mix, w_in, tm=1024, tn=2048)
            s_zero = jnp.zeros((bsz, HGRN_HEADS, HGRN_DK, HGRN_DK), F32)
            og, st = _hgrn(proj.reshape(bsz, seq, 4 * D_MODEL), lbp, on, s_zero, chunk=64, nchunk=4, layer=l,
                           l_valid=None)
            hp = _mm_res([og.reshape(bsz * seq, D_MODEL)], [w_out], hp, tm=tm_r, op_specs=[row_spec(tm_r, D_MODEL)])
            outs_p["hg"].append(st)

            m_s = dbs * ns
            proj = _norm_mm(hs, g_mix, w_in, tm=m_s, tn=1024)
            lpad = HGRN_SUB
            proj = jnp.pad(proj.reshape(dbs, ns, 4 * D_MODEL), ((0, 0), (0, lpad - ns), (0, 0)))
            og, st = _hgrn(proj, lbp, on, state_hgrn[o], chunk=lpad, nchunk=1, layer=l, l_valid=dseq)
            hs = _mm_res([og[:, :ns].reshape(m_s, D_MODEL)], [w_out], hs, tm=m_s, op_specs=[row_spec(m_s, D_MODEL)])
            outs_s["hg"].append(st)

        g_mem = row2(norm_mem[l])
        wq = w_mem_q[l].astype(BF16)
        wo = w_mem_o[l].astype(BF16)
        mqg = row2(mem_q_gain[l])
        wkv_m = jnp.concatenate([w_mem_k[l], w_mem_v[l]], axis=1).astype(BF16)
        mk, mv = _mem_kv(mem_prompt.reshape(bsz * mem_len, D_MODEL), row2(norm_memsrc[l]), wkv_m,
                         row2(mem_k_gain[l]), tm=512)
        mk = mk.reshape(bsz, mem_len, MEM_WIDTH)
        mv = mv.reshape(bsz, mem_len, MEM_WIDTH)
        outs_p["mk"].append(mk.reshape(bsz, mem_len, MEM_HEADS, MEM_HEAD_DIM))
        outs_p["mv"].append(mv.reshape(bsz, mem_len, MEM_HEADS, MEM_HEAD_DIM))
        hp = _mem_attn(hp.reshape(bsz, seq, D_MODEL), g_mem, wq, mqg, mk, mv, wo,
                       nb=1, tl=512).reshape(bsz * seq, D_MODEL)
        pair_shape = (depth, dbs, mem_len // 2, 2 * MEM_HEADS, MEM_HEAD_DIM)
        hs = _mem_attn(hs.reshape(dbs, ns, D_MODEL), g_mem, wq, mqg, cache_mem_k.reshape(pair_shape),
                       cache_mem_v.reshape(pair_shape), wo, nb=8, tl=ns, layer=l).reshape(dbs * ns, D_MODEL)

        g_mlp = row2(norm_mlp[l])
        wu = w_mlp_up[l].astype(BF16)
        wd = w_mlp_down[l].astype(BF16)
        hp = _mlp(hp, g_mlp, wu, wd, tm=1024, tf=2048)
        hs = _mlp(hs, g_mlp, wu, wd, tm=dbs * ns, tf=1024)

    y_p = hp.reshape(bsz, seq, D_MODEL)
    y_s = hs.reshape(dbs, ns, D_MODEL)[:, :dseq]
    return (y_p, y_s,
            jnp.stack(outs_p["lat"], axis=1), jnp.stack(outs_p["kr"], axis=1),
            jnp.stack(outs_p["s5r"]), jnp.stack(outs_p["s5i"]), jnp.stack(outs_p["hg"]),
            jnp.stack(outs_p["mk"]), jnp.stack(outs_p["mv"]),
            jnp.stack(outs_s["lat"], axis=1), jnp.stack(outs_s["kr"], axis=1),
            jnp.stack(outs_s["s5r"]), jnp.stack(outs_s["s5i"]), jnp.stack(outs_s["hg"]))
```

```python
import functools
import math

import jax
import jax.numpy as jnp
from jax import lax
from jax.experimental import pallas as pl
from jax.experimental.pallas import tpu as pltpu

F32 = jnp.float32
BF16 = jnp.bfloat16

LANES = 128
VMEM_LIMIT_BYTES = 56 * 1024 * 1024

D_MODEL = 1024
MLA_HEADS = 8
MLA_NOPE = 64
MLA_ROPE = 32
MLA_QK = MLA_NOPE + MLA_ROPE
MLA_V = 64
MLA_Q_LORA = 768
MLA_KV_LORA = 256
ROPE_THETA = 10000.0
PAGE_SIZE = 128
S5_WIDTH = 512
S5_GROUP = 16
S5_GROUPS = S5_WIDTH // S5_GROUP
S5_STATE = 64
S5_NSTATE = S5_GROUPS * S5_STATE
HGRN_HEADS = 8
HGRN_DK = 128
HGRN_SUB = 32
HGRN_EXP_CLAMP = 80.0
MAX_SCORE_BOUND = 40.0
MEM_HEADS = 4
MEM_HEAD_DIM = 128
MEM_WIDTH = MEM_HEADS * MEM_HEAD_DIM
D_FF = 4 * D_MODEL
EPS = 1e-6
LOG2E = math.log2(math.e)
HP = MLA_HEADS * LANES


def _cparams(sem):
    return pltpu.CompilerParams(dimension_semantics=sem, vmem_limit_bytes=VMEM_LIMIT_BYTES)


def _rms(x, g):
    return x * lax.rsqrt(jnp.mean(x * x, axis=-1, keepdims=True) + EPS) * g


def _sigmoid(x):
    return 1.0 / (1.0 + jnp.exp(-x))


def _dot(a, b):
    return jnp.dot(a, b, preferred_element_type=F32)


def _dot_nt(a, b):
    return lax.dot_general(a, b, (((1,), (1,)), ((), ())), preferred_element_type=F32)


def _dot_tn(a, b):
    return lax.dot_general(a, b, (((0,), (0,)), ((), ())), preferred_element_type=F32)


def _row_to_col(row, n):
    r = lax.broadcasted_iota(jnp.int32, (n, n), 0)
    c = lax.broadcasted_iota(jnp.int32, (n, n), 1)
    return jnp.sum(jnp.where(r == c, jnp.broadcast_to(row, (n, n)), 0.0), axis=1, keepdims=True)


def _div_pow2(x, d):
    return lax.shift_right_logical(x, int(math.log2(d)))


def _full_spec(shape):
    nd = len(shape)
    return pl.BlockSpec(shape, lambda *_: (0,) * nd)


def row_spec(tm, width):
    return pl.BlockSpec((tm, width), lambda i: (i, 0))


def _even_proj_kernel(h_ref, g_ref, wp_ref, cqn_ref, wuq_ref, qg_ref, ckvn_ref, wkv_ref, kg_ref,
                      qaug_ref, kaug_ref, vaug_ref, cq_ref, s1q_ref, s2q_ref, ck_ref, s1k_ref, s2k_ref,
                      q_out, k_out, v_out, ckv_out, kr_out, u_out, *maybe_qk_out, nrg):
    o1 = MLA_Q_LORA
    o2 = o1 + MLA_KV_LORA
    o3 = o2 + LANES
    half = MLA_ROPE // 2
    inv_qk = 1.0 / MLA_QK
    qg = qg_ref[...]
    kg = kg_ref[...]
    tm = h_ref.shape[0]
    groups = [slice(tm // nrg * t, tm // nrg * (t + 1)) for t in range(nrg)]
    proj = [_dot(_rms(h_ref[r, :], g_ref[...]).astype(BF16), wp_ref[...]) for r in groups]
    qf, kv = [], []
    for r, p in zip(groups, proj):
        u_out[r, :] = p[:, o3:]
        qf.append(_dot(_rms(p[:, :o1], cqn_ref[...]).astype(BF16), wuq_ref[...]))
        ckv = _rms(p[:, o1:o2], ckvn_ref[...])
        ckv_out[r, :] = ckv
        kv.append(_dot(ckv.astype(BF16), wkv_ref[...]))
    for r, p, qfr, kvr in zip(groups, proj, qf, kv):
        kr = p[:, o2:o3]
        krr = (kr * ck_ref[r, :] + pltpu.roll(kr, half, 1) * s1k_ref[r, :]
               + pltpu.roll(kr, LANES - half, 1) * s2k_ref[r, :])
        kr_out[r, :] = krr[:, :MLA_ROPE]
        kr_sh = pltpu.roll(krr, MLA_NOPE, 1)
        cq_t, s1q_t, s2q_t = cq_ref[r, :], s1q_ref[r, :], s2q_ref[r, :]
        for h in range(MLA_HEADS):
            sl = slice(LANES * h, LANES * (h + 1))
            qh = qfr[:, sl]
            qh = qh * lax.rsqrt(jnp.sum(qh * qh, axis=-1, keepdims=True) * inv_qk + EPS) * qg
            qh = (qh * cq_t + pltpu.roll(qh, half, 1) * s1q_t + pltpu.roll(qh, LANES - half, 1) * s2q_t)
            q_out[r, sl] = (qh + qaug_ref[...]).astype(BF16)
            if maybe_qk_out:
                maybe_qk_out[0][r, sl] = (qh * kg).astype(BF16)
            kh = kvr[:, sl] + kr_sh
            kh = kh * lax.rsqrt(jnp.sum(kh * kh, axis=-1, keepdims=True) * inv_qk + EPS) * kg
            k_out[r, sl] = (kh + kaug_ref[...]).astype(BF16)
        v_out[:, r] = (kvr[:, HP:] + vaug_ref[...]).T.astype(BF16)


def _even_proj(h, g, wp, cqn, wuq, qg, ckvn, wkv, kg, aug, tabs, *, tm, u_shape, u_spec, emit_qk):
    m = h.shape[0]
    ltab = tabs[0].shape[0]
    ntab = ltab // tm
    row = lambda i: (i, 0)
    tab_spec = pl.BlockSpec((tm, LANES), lambda i: (i % ntab, 0))
    in_specs = ([pl.BlockSpec((tm, D_MODEL), row), _full_spec(g.shape), _full_spec(wp.shape),
                 _full_spec(cqn.shape), _full_spec(wuq.shape), _full_spec(qg.shape),
                 _full_spec(ckvn.shape), _full_spec(wkv.shape), _full_spec(kg.shape)]
                + [_full_spec(a.shape) for a in aug] + [tab_spec] * 6)
    out_shape = [jax.ShapeDtypeStruct((m, HP), BF16), jax.ShapeDtypeStruct((m, HP), BF16),
                 jax.ShapeDtypeStruct((m // tm, HP, tm), BF16), jax.ShapeDtypeStruct((m, MLA_KV_LORA), F32),
                 jax.ShapeDtypeStruct((m, MLA_ROPE), F32), jax.ShapeDtypeStruct(u_shape, F32)]
    out_specs = [pl.BlockSpec((tm, HP), row), pl.BlockSpec((tm, HP), row),
                 pl.BlockSpec((None, HP, tm), lambda i: (i, 0, 0)),
                 pl.BlockSpec((tm, MLA_KV_LORA), row), pl.BlockSpec((tm, MLA_ROPE), row),
                 u_spec]
    if emit_qk:
        out_shape.append(jax.ShapeDtypeStruct((m, HP), BF16))
        out_specs.append(pl.BlockSpec((tm, HP), row))
    return pl.pallas_call(
        functools.partial(_even_proj_kernel, nrg=2), grid=(m // tm,), in_specs=in_specs, out_specs=out_specs,
        out_shape=out_shape,
        compiler_params=_cparams(("arbitrary",)), name="even_proj",
    )(h, g, wp, cqn, wuq, qg, ckvn, wkv, kg, *aug, *tabs)


def _flash_kernel(bounded_ref, q_ref, k_ref, vt_ref, o_ref, *, tq, tk, hg):
    i = pl.program_id(1)
    nfull = (i * tq) // tk
    key = lax.broadcasted_iota(jnp.int32, (tk, tq), 0) + nfull * tk
    qry = lax.broadcasted_iota(jnp.int32, (tk, tq), 1) + i * tq
    causal = qry >= key

    heads = [slice(LANES * h, LANES * (h + 1)) for h in range(MLA_HEADS)]

    def accumulate(j, acc, masked):
        off = pl.multiple_of(j * tk, tk)
        out = []
        for h0 in range(0, MLA_HEADS, hg):
            grp = range(h0, h0 + hg)
            scores = [_dot_nt(k_ref[pl.ds(off, tk), heads[h]], q_ref[:, heads[h]]) for h in grp]
            if masked:
                scores = [jnp.where(causal, s, -jnp.inf) for s in scores]
            probs = [jnp.exp2(s).astype(BF16) for s in scores]
            out += [acc[h] + _dot(vt_ref[j, heads[h], :], p) for p, h in zip(probs, grp)]
        return tuple(out)

    @pl.when(bounded_ref[0] != 0)
    def _():
        acc = lax.fori_loop(0, nfull, lambda j, c: accumulate(j, c, False),
                            (jnp.zeros((LANES, tq), F32),) * MLA_HEADS)
        acc = accumulate(nfull, acc, True)
        for h, sl in enumerate(heads):
            o_ref[:, sl] = (acc[h] / acc[h][MLA_V:MLA_V + 1, :]).T.astype(BF16)

    def update(j, carry, masked):
        off = pl.multiple_of(j * tk, tk)
        out = []
        for h0 in range(0, MLA_HEADS, hg):
            grp = range(h0, h0 + hg)
            scores = [_dot_nt(k_ref[pl.ds(off, tk), heads[h]], q_ref[:, heads[h]]) for h in grp]
            probs, stats = [], []
            for h, s in zip(grp, scores):
                m, l = carry[3 * h], carry[3 * h + 1]
                if masked:
                    s = jnp.where(causal, s, -jnp.inf)
                m_new = jnp.maximum(m, jnp.max(s, axis=0, keepdims=True))
                alpha = jnp.exp2(m - m_new)
                p = jnp.exp2(s - m_new)
                stats.append((m_new, alpha * l + jnp.sum(p, axis=0, keepdims=True), alpha))
                probs.append(p.astype(BF16))
            pv = [_dot(vt_ref[j, heads[h], :], p) for p, h in zip(probs, grp)]
            for h, (m_new, l_new, alpha), o in zip(grp, stats, pv):
                out += [m_new, l_new, alpha * carry[3 * h + 2] + o]
        return tuple(out)

    @pl.when(bounded_ref[0] == 0)
    def _():
        init = (jnp.full((1, tq), -jnp.inf, F32), jnp.zeros((1, tq), F32), jnp.zeros((LANES, tq), F32))
        carry = lax.fori_loop(0, nfull, lambda j, c: update(j, c, False), init * MLA_HEADS)
        carry = update(nfull, carry, True)
        for h, sl in enumerate(heads):
            o_ref[:, sl] = (carry[3 * h + 2] / carry[3 * h + 1]).T.astype(BF16)


def _flash_attention(bounded, q, k, vt, *, tq, hg=2):
    b, l, _ = q.shape
    tk = vt.shape[2]
    nkb = l // tk
    grid_spec = pltpu.PrefetchScalarGridSpec(
        num_scalar_prefetch=1, grid=(b, l // tq),
        in_specs=[pl.BlockSpec((None, tq, HP), lambda bi, i, f: (bi, i, 0)),
                  pl.BlockSpec((None, l, HP), lambda bi, i, f: (bi, 0, 0)),
                  pl.BlockSpec((nkb, HP, tk), lambda bi, i, f: (bi, 0, 0))],
        out_specs=pl.BlockSpec((None, tq, HP), lambda bi, i, f: (bi, i, 0)))
    return pl.pallas_call(
        functools.partial(_flash_kernel, tq=tq, tk=tk, hg=hg), grid_spec=grid_spec,
        out_shape=jax.ShapeDtypeStruct((b, l, HP), BF16),
        compiler_params=_cparams(("arbitrary", "arbitrary")), name="prompt_attention",
    )(bounded, q, k, vt)


def _paged_kernel(pt_ref, bounded_ref, *refs, npg, ngrp, nsteps, nq):
    lat_refs = refs[:npg]
    krt_refs = refs[npg:2 * npg]
    (qn_ref, rhs2_ref, wuk_ref, cnew_ref, krnew_ref, wuv_ref, bound_ref,
     o_ref, wabs, m_scr, l_scr, a_scr, l_acc, a_acc) = refs[2 * npg:]
    del pt_ref
    s = pl.program_id(1)
    nslots = nsteps * ngrp + 1
    ncol = LANES
    inv_qk = 1.0 / MLA_QK
    bounded = bounded_ref[0] != 0
    last = s == nsteps - 1

    @pl.when(s == 0)
    def _():
        wabs[...] = _dot(wuk_ref[...], qn_ref[...]).astype(BF16)
        l_acc[...] = jnp.zeros(l_acc.shape, F32)
        a_acc[...] = jnp.zeros(a_acc.shape, F32)

    def scores(blocks, mask):
        kn = [_dot(c, wuk_ref[...]) for c, _ in blocks]
        sq = [k * k for k in kn]
        psum = [q[:, 0:LANES] + q[:, LANES:2 * LANES] + q[:, 2 * LANES:3 * LANES] + q[:, 3 * LANES:] for q in sq]
        r2 = [_dot(jnp.concatenate([p.astype(BF16), x], axis=1), rhs2_ref[...]) for p, (_, x) in zip(psum, blocks)]
        scn = [_dot(c, wabs[...]) for c, _ in blocks]
        out = []
        for t in range(len(blocks)):
            sc = (scn[t] + r2[t][:, LANES:]) * lax.rsqrt(r2[t][:, :LANES] * inv_qk + EPS)
            out.append(sc if mask is None else jnp.where(mask, sc, -jnp.inf))
        return out

    def stats(blocks, scs):
        probs, out = [], []
        for sc in scs:
            m = jnp.max(sc, axis=0, keepdims=True)
            p = jnp.exp2(sc - m)
            out.append((m, jnp.sum(p, axis=0, keepdims=True)))
            probs.append(p.astype(BF16))
        acc = [_dot_tn(p, c) for p, (c, _) in zip(probs, blocks)]
        return [(m, l, a) for (m, l), a in zip(out, acc)]

    def accumulate(blocks, scs):
        probs = [jnp.exp2(sc - bound_ref[...]) for sc in scs]
        l_new = l_acc[...]
        for p in probs:
            l_new = l_new + jnp.sum(p, axis=0, keepdims=True)
        l_acc[...] = l_new
        a_new = a_acc[...]
        for p, (c, _) in zip(probs, blocks):
            a_new = a_new + _dot_tn(p.astype(BF16), c)
        a_acc[...] = a_new

    def finish(num):
        full = _dot(num.astype(BF16), wuv_ref[...])
        hrow = lax.broadcasted_iota(jnp.int32, (MLA_HEADS, MLA_HEADS * MLA_V), 0)
        hcol = _div_pow2(lax.broadcasted_iota(jnp.int32, (MLA_HEADS, MLA_HEADS * MLA_V), 1), MLA_V)
        rows = []
        for qi in range(nq):
            blk = full[MLA_HEADS * qi:MLA_HEADS * (qi + 1), :]
            rows.append(jnp.sum(jnp.where(hrow == hcol, blk, 0.0), axis=0, keepdims=True))
        o_ref[...] = jnp.concatenate(rows, axis=0)

    zpad = jnp.zeros((LANES - 2 * MLA_ROPE, PAGE_SIZE), F32)

    def rope_block(g):
        krt = krt_refs[g][...]
        return jnp.concatenate([krt, krt * krt, zpad], axis=0).T.astype(BF16)

    pg = npg // ngrp
    groups = [(jnp.concatenate([lat_refs[g][...].astype(BF16) for g in range(pg * t, pg * (t + 1))], axis=0),
               jnp.concatenate([rope_block(g) for g in range(pg * t, pg * (t + 1))], axis=0))
              for t in range(ngrp)]
    scs = scores(groups, None)

    @pl.when(bounded)
    def _():
        accumulate(groups, scs)

    @pl.when(jnp.logical_not(bounded))
    def _():
        for t, (m, l, a) in enumerate(stats(groups, scs)):
            slot = s * ngrp + t
            m_scr[pl.ds(slot, 1), :] = m
            l_scr[pl.ds(slot, 1), :] = l
            a_scr[slot] = a

    def new_block():
        nnew = cnew_ref.shape[0]
        krn = jnp.concatenate([krnew_ref[...], jnp.zeros((nnew, LANES - MLA_ROPE), F32)], axis=1)
        krn = krn + pltpu.roll(krn * krn, MLA_ROPE, 1)
        key = lax.broadcasted_iota(jnp.int32, (nnew, ncol), 0)
        qry = _div_pow2(lax.broadcasted_iota(jnp.int32, (nnew, ncol), 1), MLA_HEADS)
        blocks = [(cnew_ref[...].astype(BF16), krn.astype(BF16))]
        return blocks, scores(blocks, key <= qry)

    @pl.when(jnp.logical_and(last, bounded))
    def _():
        accumulate(*new_block())
        finish(a_acc[...] * _row_to_col(1.0 / l_acc[...], ncol))

    @pl.when(jnp.logical_and(last, jnp.logical_not(bounded)))
    def _():
        blocks, scs_new = new_block()
        (m2, l2, a2), = stats(blocks, scs_new)
        m_scr[nslots - 1:nslots, :] = m2
        l_scr[nslots - 1:nslots, :] = l2
        a_scr[nslots - 1] = a2
        mall = m_scr[0:nslots, :]
        w = jnp.exp2(mall - jnp.max(mall, axis=0, keepdims=True))
        den = jnp.sum(l_scr[0:nslots, :] * w, axis=0, keepdims=True)
        wn = w / den
        num = jnp.zeros((ncol, MLA_KV_LORA), F32)
        for t in range(nslots):
            num = num + a_scr[t] * _row_to_col(wn[t:t + 1, :], ncol)
        finish(num)


def _paged_attention(page_table, bounded, cache_lat, cache_krt, e, qn, rhs2, wuk, cnew, krnew, wuv, bound, *,
                     npg, ngrp, nq):
    nb, npages = page_table.shape
    nsteps = npages // npg
    nnew = cnew.shape[1]
    nslots = nsteps * ngrp + 1

    def page_spec(shape, g):
        return pl.BlockSpec((None, None) + shape, lambda b, s, pt, f: (pt[b, s * npg + g], e, 0, 0))

    per_b3 = lambda b, s, pt, f: (b, 0, 0)
    const2 = lambda b, s, pt, f: (0, 0)
    in_specs = ([page_spec((PAGE_SIZE, MLA_KV_LORA), g) for g in range(npg)]
                + [page_spec((MLA_ROPE, PAGE_SIZE), g) for g in range(npg)]
                + [pl.BlockSpec((None,) + qn.shape[1:], per_b3), pl.BlockSpec((None,) + rhs2.shape[1:], per_b3),
                   pl.BlockSpec(wuk.shape, const2),
                   pl.BlockSpec((None, nnew, MLA_KV_LORA), per_b3), pl.BlockSpec((None, nnew, MLA_ROPE), per_b3),
                   pl.BlockSpec(wuv.shape, const2), pl.BlockSpec(bound.shape, const2)])
    grid_spec = pltpu.PrefetchScalarGridSpec(
        num_scalar_prefetch=2, grid=(nb, nsteps), in_specs=in_specs,
        out_specs=pl.BlockSpec((None, nq, MLA_HEADS * MLA_V), per_b3),
        scratch_shapes=[pltpu.VMEM((MLA_KV_LORA, LANES), BF16),
                        pltpu.VMEM((nslots, LANES), F32), pltpu.VMEM((nslots, LANES), F32),
                        pltpu.VMEM((nslots, LANES, MLA_KV_LORA), F32),
                        pltpu.VMEM((1, LANES), F32), pltpu.VMEM((LANES, MLA_KV_LORA), F32)])
    return pl.pallas_call(
        functools.partial(_paged_kernel, npg=npg, ngrp=ngrp, nsteps=nsteps, nq=nq),
        grid_spec=grid_spec, out_shape=jax.ShapeDtypeStruct((nb, nq, MLA_HEADS * MLA_V), F32),
        compiler_params=_cparams(("arbitrary", "arbitrary")), name="paged_attention",
    )(page_table, bounded, *([cache_lat] * npg), *([cache_krt] * npg), qn, rhs2, wuk, cnew, krnew, wuv, bound)


def _s5_kernel(u_ref, h0r_ref, h0i_ref, lamr_ref, lami_ref, lstep_ref, brm_ref, bim_ref, crm_ref, cim_ref,
               d_ref, wg_ref, bg_ref, o_ref, hr_out, hi_out, xr_scr, xi_scr, hcr, hci, disc, io_scr, *,
               tt, nb, strip, interleave):
    c = pl.program_id(0)

    @pl.when(c == 0)
    def _():
        lr = jnp.minimum(lamr_ref[...], -1e-4)
        li = lami_ref[...]
        dt = jnp.exp(lstep_ref[...])
        mag = jnp.exp(lr * dt)
        abr = mag * jnp.cos(li * dt)
        abi = mag * jnp.sin(li * dt)
        den = lr * lr + li * li
        disc[0:1, :] = abr
        disc[1:2, :] = abi
        disc[2:3, :] = ((abr - 1.0) * lr + abi * li) / den
        disc[3:4, :] = (abi * lr - (abr - 1.0) * li) / den
        hcr[...] = h0r_ref[...]
        hci[...] = h0i_ref[...]

    nlb = S5_WIDTH // LANES
    if interleave:
        for b in range(nb):
            for j in range(nlb):
                c0 = S5_WIDTH * b + LANES * j
                io_scr[j, pl.ds(b, tt, stride=nb), :] = u_ref[:, c0:c0 + LANES]
        u = jnp.concatenate([io_scr[j] for j in range(nlb)], axis=1)
    else:
        u = u_ref[...]
    ub = u.astype(BF16)
    kc = 2 * LANES
    ks = kc * S5_STATE // S5_GROUP
    for k in range(S5_WIDTH // kc):
        cols = slice(kc * k, kc * (k + 1))
        sts = slice(ks * k, ks * (k + 1))
        pr = _dot(ub[:, cols], brm_ref[cols, sts])
        pi = _dot(ub[:, cols], bim_ref[cols, sts])
        cor = disc[2:3, sts]
        coi = disc[3:4, sts]
        xr_scr[:, sts] = cor * pr - coi * pi
        xi_scr[:, sts] = cor * pi + coi * pr

    for s0 in range(0, S5_NSTATE, strip):
        lanes = slice(s0, s0 + strip)
        ar = jnp.broadcast_to(disc[0:1, lanes], (nb, strip))
        ai = jnp.broadcast_to(disc[1:2, lanes], (nb, strip))

        def step(t, carry, lanes=lanes, ar=ar, ai=ai):
            hr, hi = carry
            rows = pl.ds(pl.multiple_of(t * nb, nb), nb)
            nr = ar * hr - ai * hi + xr_scr[rows, lanes]
            ni = ar * hi + ai * hr + xi_scr[rows, lanes]
            xr_scr[rows, lanes] = nr
            xi_scr[rows, lanes] = ni
            return nr, ni

        hr, hi = lax.fori_loop(0, tt, step, (hcr[:, lanes], hci[:, lanes]))
        hcr[:, lanes] = hr
        hci[:, lanes] = hi

    ys = []
    for k in range(S5_WIDTH // kc):
        cols = slice(kc * k, kc * (k + 1))
        sts = slice(ks * k, ks * (k + 1))
        ys.append(_dot(xr_scr[:, sts].astype(BF16), crm_ref[sts, cols])
                  - _dot(xi_scr[:, sts].astype(BF16), cim_ref[sts, cols]))
    y = jnp.concatenate(ys, axis=1) + d_ref[...] * u
    z = jax.nn.gelu(y)
    gate = _sigmoid(_dot(z.astype(BF16), wg_ref[...]) + bg_ref[...])
    if interleave:
        out = z * gate
        for j in range(nlb):
            io_scr[j] = out[:, LANES * j:LANES * (j + 1)]
        for b in range(nb):
            for j in range(nlb):
                c0 = S5_WIDTH * b + LANES * j
                o_ref[:, c0:c0 + LANES] = io_scr[j, pl.ds(b, tt, stride=nb), :].astype(BF16)
    else:
        o_ref[...] = (z * gate).astype(BF16)

    @pl.when(c == pl.num_programs(0) - 1)
    def _():
        hr_out[...] = hcr[...]
        hi_out[...] = hci[...]


def _s5(u, h0r, h0i, lamr, lami, lstep, brm, bim, crm, cim, d, wg, bg, *, tt, nb, strip, interleave):
    steps = u.shape[0] if interleave else u.shape[0] // nb
    blk = tt * nb
    consts = (h0r, h0i, lamr, lami, lstep, brm, bim, crm, cim, d, wg, bg)
    io_spec = row_spec(tt, nb * S5_WIDTH) if interleave else row_spec(blk, S5_WIDTH)
    return pl.pallas_call(
        functools.partial(_s5_kernel, tt=tt, nb=nb, strip=strip, interleave=interleave),
        grid=(steps // tt,),
        in_specs=[io_spec] + [_full_spec(a.shape) for a in consts],
        out_specs=[io_spec, _full_spec((nb, S5_NSTATE)), _full_spec((nb, S5_NSTATE))],
        out_shape=[jax.ShapeDtypeStruct(u.shape, BF16),
                   jax.ShapeDtypeStruct((nb, S5_NSTATE), F32), jax.ShapeDtypeStruct((nb, S5_NSTATE), F32)],
        scratch_shapes=[pltpu.VMEM((blk, S5_NSTATE), F32), pltpu.VMEM((blk, S5_NSTATE), F32),
                        pltpu.VMEM((nb, S5_NSTATE), F32), pltpu.VMEM((nb, S5_NSTATE), F32),
                        pltpu.VMEM((8, S5_NSTATE), F32), pltpu.VMEM((S5_WIDTH // LANES, blk, LANES), F32)],
        compiler_params=_cparams(("arbitrary",)), name="s5",
    )(u, *consts)


def _mm_res_kernel(*refs, nop):
    res_ref = refs[2 * nop]
    o_ref = refs[2 * nop + 1]
    acc = res_ref[...]
    for t in range(nop):
        acc = acc + _dot(refs[t][...].astype(BF16), refs[nop + t][...])
    o_ref[...] = acc


def _mm_res(ops, ws, res, *, tm, op_specs):
    m, n = res.shape
    row = lambda i: (i, 0)
    in_specs = list(op_specs) + [_full_spec(w.shape) for w in ws] + [pl.BlockSpec((tm, n), row)]
    return pl.pallas_call(
        functools.partial(_mm_res_kernel, nop=len(ops)), grid=(m // tm,), in_specs=in_specs,
        out_specs=pl.BlockSpec((tm, n), row), out_shape=jax.ShapeDtypeStruct((m, n), F32),
        compiler_params=_cparams(("arbitrary",)), name="matmul_residual",
    )(*ops, *ws, res)


def _mem_kv_kernel(x_ref, g_ref, w_ref, kg_ref, k_out, v_out):
    mn = _rms(x_ref[...], g_ref[...]).astype(BF16)
    kv = _dot(mn, w_ref[...])
    kg = kg_ref[...]
    for h in range(MEM_HEADS):
        sl = slice(LANES * h, LANES * (h + 1))
        k_out[:, sl] = _rms(kv[:, sl], kg)
    v_out[...] = kv[:, MEM_WIDTH:]


def _mem_kv(x, g, w, kg, *, tm):
    m = x.shape[0]
    row = lambda i: (i, 0)
    return pl.pallas_call(
        _mem_kv_kernel, grid=(m // tm,),
        in_specs=[pl.BlockSpec((tm, D_MODEL), row), _full_spec(g.shape), _full_spec(w.shape), _full_spec(kg.shape)],
        out_specs=[pl.BlockSpec((tm, MEM_WIDTH), row), pl.BlockSpec((tm, MEM_WIDTH), row)],
        out_shape=[jax.ShapeDtypeStruct((m, MEM_WIDTH), F32), jax.ShapeDtypeStruct((m, MEM_WIDTH), F32)],
        compiler_params=_cparams(("arbitrary",)), name="mem_kv",
    )(x, g, w, kg)


def _mem_attn_kernel(h_ref, g_ref, wq_ref, qg_ref, mk_ref, mv_ref, wo_ref, o_ref, *, nb, tl, paired):
    x = h_ref[...].reshape(nb * tl, D_MODEL)
    hn = _rms(x, g_ref[...]).astype(BF16)
    q = _dot(hn, wq_ref[...])
    qg = qg_ref[...] * (MEM_HEAD_DIM ** -0.5 * LOG2E)
    qn = [_rms(q[:, LANES * h:LANES * (h + 1)], qg).astype(BF16) for h in range(MEM_HEADS)]

    def head_block(ref, b, h):
        if paired:
            return jnp.concatenate([ref[b, :, h, :], ref[b, :, MEM_HEADS + h, :]], axis=0).astype(BF16)
        return ref[b, :, LANES * h:LANES * (h + 1)].astype(BF16)

    pairs = [(b, h) for b in range(nb) for h in range(MEM_HEADS)]
    scores = [_dot_nt(qn[h][tl * b:tl * (b + 1), :], head_block(mk_ref, b, h)) for b, h in pairs]
    probs = [jnp.exp2(s - jnp.max(s, axis=-1, keepdims=True)) for s in scores]
    outs = [_dot(p.astype(BF16), head_block(mv_ref, b, h)) / jnp.sum(p, axis=-1, keepdims=True)
            for p, (b, h) in zip(probs, pairs)]
    rows = [jnp.concatenate(outs[MEM_HEADS * b:MEM_HEADS * (b + 1)], axis=1) for b in range(nb)]
    o = (rows[0] if nb == 1 else jnp.concatenate(rows, axis=0)).astype(BF16)
    o_ref[...] = (x + _dot(o, wo_ref[...])).reshape(nb, tl, D_MODEL)


def _mem_attn(h, g, wq, qg, mk, mv, wo, *, nb, tl, layer=None):
    b, l, _ = h.shape
    blk = lambda bi, i: (bi, i, 0)
    if layer is None:
        mem_spec = pl.BlockSpec((nb,) + mk.shape[1:], lambda bi, i: (bi, 0, 0))
    else:
        mem_spec = pl.BlockSpec((None, nb) + mk.shape[2:], lambda bi, i: (layer, bi, 0, 0, 0))
    return pl.pallas_call(
        functools.partial(_mem_attn_kernel, nb=nb, tl=tl, paired=layer is not None), grid=(b // nb, l // tl),
        in_specs=[pl.BlockSpec((nb, tl, D_MODEL), blk), _full_spec(g.shape), _full_spec(wq.shape),
                  _full_spec(qg.shape), mem_spec, mem_spec, _full_spec(wo.shape)],
        out_specs=pl.BlockSpec((nb, tl, D_MODEL), blk),
        out_shape=jax.ShapeDtypeStruct(h.shape, F32),
        compiler_params=_cparams(("arbitrary", "arbitrary")), name="mem_attention",
    )(h, g, wq, qg, mk, mv, wo)


def _mlp_kernel(h_ref, g_ref, wu_ref, wd_ref, o_ref, xn_scr, acc_scr):
    j = pl.program_id(1)

    @pl.when(j == 0)
    def _():
        xn_scr[...] = _rms(h_ref[...], g_ref[...]).astype(BF16)
        acc_scr[...] = jnp.zeros(acc_scr.shape, F32)

    a = _dot(xn_scr[...], wu_ref[...])
    a = jnp.square(jnp.maximum(a, 0.0)).astype(BF16)
    acc_scr[...] += _dot(a, wd_ref[...])

    @pl.when(j == pl.num_programs(1) - 1)
    def _():
        o_ref[...] = h_ref[...] + acc_scr[...]


def _mlp(h, g, wu, wd, *, tm, tf):
    m = h.shape[0]
    return pl.pallas_call(
        _mlp_kernel, grid=(m // tm, D_FF // tf),
        in_specs=[pl.BlockSpec((tm, D_MODEL), lambda i, j: (i, 0)), _full_spec(g.shape),
                  pl.BlockSpec((D_MODEL, tf), lambda i, j: (0, j)), pl.BlockSpec((tf, D_MODEL), lambda i, j: (j, 0))],
        out_specs=pl.BlockSpec((tm, D_MODEL), lambda i, j: (i, 0)),
        out_shape=jax.ShapeDtypeStruct((m, D_MODEL), F32),
        scratch_shapes=[pltpu.VMEM((tm, D_MODEL), BF16), pltpu.VMEM((tm, D_MODEL), F32)],
        compiler_params=_cparams(("arbitrary", "arbitrary")), name="mlp",
    )(h, g, wu, wd)


def _norm_mm_kernel(h_ref, g_ref, w_ref, o_ref, xn_scr):
    @pl.when(pl.program_id(1) == 0)
    def _():
        xn_scr[...] = _rms(h_ref[...], g_ref[...]).astype(BF16)

    o_ref[...] = _dot(xn_scr[...], w_ref[...])


def _norm_mm(h, g, w, *, tm, tn):
    m = h.shape[0]
    n = w.shape[1]
    return pl.pallas_call(
        _norm_mm_kernel, grid=(m // tm, n // tn),
        in_specs=[pl.BlockSpec((tm, D_MODEL), lambda i, j: (i, 0)), _full_spec(g.shape),
                  pl.BlockSpec((D_MODEL, tn), lambda i, j: (0, j))],
        out_specs=pl.BlockSpec((tm, tn), lambda i, j: (i, j)),
        out_shape=jax.ShapeDtypeStruct((m, n), F32),
        scratch_shapes=[pltpu.VMEM((tm, D_MODEL), BF16)],
        compiler_params=_cparams(("arbitrary", "arbitrary")), name="norm_matmul",
    )(h, g, w)


def _hgrn_kernel(q_ref, f_ref, i_ref, g_ref, lbp_ref, on_ref, s0_ref, o_ref, s_out, s_scr, *,
                 chunk, nchunk, layer, l_valid):
    c = pl.program_id(1)
    tb = chunk * nchunk

    @pl.when(c == 0)
    def _():
        s_scr[...] = s0_ref[...]

    lbp = lbp_ref[...]
    e = jnp.exp(lbp - jnp.max(lbp, axis=0, keepdims=True))
    sm = e / jnp.sum(e, axis=0, keepdims=True)
    lb = jnp.sum(sm[0:layer + 1, :], axis=0, keepdims=True) - sm[0:1, :]

    q = q_ref[...]
    qa = q * _sigmoid(q)
    fg = lb + (1.0 - lb) * _sigmoid(f_ref[...])
    logf = jnp.log(fg)
    kk = 1.0 - fg
    v = i_ref[...]
    if l_valid is not None:
        valid = (lax.broadcasted_iota(jnp.int32, (tb, 1), 0) + c * tb) < l_valid
        logf = jnp.where(valid, logf, 0.0)
        kk = jnp.where(valid, kk, 0.0)
    vb = v.astype(BF16)

    tr = lax.broadcasted_iota(jnp.int32, (tb, tb), 0)
    tc = lax.broadcasted_iota(jnp.int32, (tb, tb), 1)
    same_chunk = _div_pow2(tr, chunk) == _div_pow2(tc, chunk)
    tri = jnp.where(same_chunk, jnp.where(tr >= tc, 1.0, 0.0), 0.0).astype(BF16)
    hi = logf.astype(BF16)
    lo = (logf - hi.astype(F32)).astype(BF16)
    bcum = _dot(tri, hi) + _dot(tri, lo)
    qhat = (qa * jnp.exp(bcum)).astype(BF16)

    nsub = chunk // HGRN_SUB
    khat, dec, qloc, kloc, masks = [], [], [], [], []
    spread = jnp.zeros((1, bcum.shape[1]), F32)
    for ci in range(nchunk):
        c0 = ci * chunk
        blast = bcum[c0 + chunk - 1:c0 + chunk, :]
        khat.append((kk[c0:c0 + chunk, :] * jnp.exp(blast - bcum[c0:c0 + chunk, :])).astype(BF16))
        dec.append(jnp.exp(blast))
        for i in range(nsub):
            r0 = c0 + i * HGRN_SUB
            r1 = r0 + HGRN_SUB
            base = bcum[r0 - 1:r0, :] if i > 0 else jnp.zeros((1, bcum.shape[1]), F32)
            spread = jnp.minimum(spread, bcum[r1 - 1:r1, :] - base)
            qloc.append((qa[r0:r1, :] * jnp.exp(bcum[r0:r1, :] - base)).astype(BF16))
            kloc.append((kk[c0:r1, :] * jnp.exp(jnp.minimum(base - bcum[c0:r1, :], HGRN_EXP_CLAMP))).astype(BF16))
    wild = jnp.min(spread) < -HGRN_EXP_CLAMP
    for i in range(nsub):
        ncols = (i + 1) * HGRN_SUB
        ar = lax.broadcasted_iota(jnp.int32, (HGRN_SUB, ncols), 0) + i * HGRN_SUB
        ac = lax.broadcasted_iota(jnp.int32, (HGRN_SUB, ncols), 1)
        in_block = ac >= i * HGRN_SUB
        masks.append(jnp.logical_and(ar >= ac, jnp.logical_not(jnp.logical_and(wild, in_block))))

    hsl = [slice(HGRN_DK * h, HGRN_DK * (h + 1)) for h in range(HGRN_HEADS)]

    def in_block_exact():
        pos = jnp.bitwise_and(lax.broadcasted_iota(jnp.int32, (tb, 1), 0), HGRN_SUB - 1)
        out = jnp.zeros((tb, bcum.shape[1]), F32)
        for j in range(HGRN_SUB):
            ok = pos >= j
            kj, bj, vj = (kk, bcum, v) if j == 0 else (pltpu.roll(x, j, 0) for x in (kk, bcum, v))
            e = jnp.where(ok, qa * kj * jnp.exp(jnp.where(ok, bcum - bj, 0.0)), 0.0)
            out = out + jnp.concatenate(
                [jnp.sum(e[:, sl], axis=-1, keepdims=True) * vj[:, sl] for sl in hsl], axis=1)
        return out

    blocks = [(ci, i) for ci in range(nchunk) for i in range(nsub)]
    att = [[_dot_nt(qloc[ci * nsub + i][:, sl], kloc[ci * nsub + i][:, sl]) for ci, i in blocks] for sl in hsl]
    att = [[jnp.where(masks[i], a, 0.0).astype(BF16) for a, (ci, i) in zip(row, blocks)] for row in att]
    intra = [[_dot(a, vb[ci * chunk:ci * chunk + (i + 1) * HGRN_SUB, sl]) for a, (ci, i) in zip(row, blocks)]
             for row, sl in zip(att, hsl)]
    kv = [[_dot_tn(khat[ci][:, sl], vb[ci * chunk:(ci + 1) * chunk, sl]) for ci in range(nchunk)] for sl in hsl]
    dcol = [[_row_to_col(dec[ci][:, sl], HGRN_DK) for ci in range(nchunk)] for sl in hsl]
    st = [s_scr[h] for h in range(HGRN_HEADS)]
    inter = [[] for _ in hsl]
    for ci in range(nchunk):
        rows = slice(ci * chunk, (ci + 1) * chunk)
        for h, sl in enumerate(hsl):
            inter[h].append(_dot(qhat[rows, sl], st[h].astype(BF16)))
        for h in range(HGRN_HEADS):
            st[h] = dcol[h][ci] * st[h] + kv[h][ci]
    o_heads = []
    for h in range(HGRN_HEADS):
        s_scr[h] = st[h]
        parts = [inter[h][ci][i * HGRN_SUB:(i + 1) * HGRN_SUB, :] + intra[h][ci * nsub + i] for ci, i in blocks]
        o_heads.append(parts[0] if len(parts) == 1 else jnp.concatenate(parts, axis=0))

    o = jnp.concatenate(o_heads, axis=1)
    o = lax.cond(wild, lambda: o + in_block_exact(), lambda: o)
    g = g_ref[...]
    o_ref[...] = (_rms(o, on_ref[...]) * (g * _sigmoid(g))).astype(BF16)

    @pl.when(c == pl.num_programs(1) - 1)
    def _():
        s_out[...] = s_scr[...]


def _hgrn(proj, lbp, on, s0, *, chunk, nchunk, layer, l_valid):
    b, l, _ = proj.shape
    w = D_MODEL
    tb = chunk * nchunk

    def col(k):
        return pl.BlockSpec((None, tb, w), lambda bi, c: (bi, c, k))

    st_spec = pl.BlockSpec((None, HGRN_HEADS, HGRN_DK, HGRN_DK), lambda bi, c: (bi, 0, 0, 0))
    return pl.pallas_call(
        functools.partial(_hgrn_kernel, chunk=chunk, nchunk=nchunk, layer=layer, l_valid=l_valid),
        grid=(b, l // tb),
        in_specs=[col(0), col(1), col(2), col(3), _full_spec(lbp.shape), _full_spec(on.shape), st_spec],
        out_specs=[pl.BlockSpec((None, tb, w), lambda bi, c: (bi, c, 0)), st_spec],
        out_shape=[jax.ShapeDtypeStruct((b, l, w), BF16), jax.ShapeDtypeStruct(s0.shape, F32)],
        scratch_shapes=[pltpu.VMEM((HGRN_HEADS, HGRN_DK, HGRN_DK), F32)],
        compiler_params=_cparams(("arbitrary", "arbitrary")), name="hgrn",
    )(proj, proj, proj, proj, lbp, on, s0)


def _pad_last(x, n):
    return jnp.pad(x, [(0, 0)] * (x.ndim - 1) + [(0, n - x.shape[-1])])


def _head_pad(w, per):
    k = w.shape[0]
    return _pad_last(w.reshape(k, -1, per), LANES).reshape(k, -1)


def _rope_tables(pos):
    half = MLA_ROPE // 2
    inv = ROPE_THETA ** (-jnp.arange(half, dtype=F32) / half)
    ang = pos.astype(F32)[:, None] * inv[None, :]
    cos, sin = jnp.cos(ang), jnp.sin(ang)
    n = pos.shape[0]
    z = lambda w: jnp.zeros((n, w), F32)
    scale = MLA_QK ** -0.5 * LOG2E
    cq = scale * jnp.concatenate([jnp.ones((n, MLA_NOPE), F32), cos, cos, z(LANES - MLA_QK)], axis=1)
    s1q = scale * jnp.concatenate([z(MLA_NOPE + half), sin, z(LANES - MLA_QK)], axis=1)
    s2q = scale * jnp.concatenate([z(MLA_NOPE), -sin, z(half + LANES - MLA_QK)], axis=1)
    ck = jnp.concatenate([cos, cos, z(LANES - MLA_ROPE)], axis=1)
    s1k = jnp.concatenate([z(half), sin, z(LANES - MLA_ROPE)], axis=1)
    s2k = jnp.concatenate([-sin, z(LANES - half)], axis=1)
    return (cq, s1q, s2q, ck, s1k, s2k)


def _block_diag(x):
    g, a, b = x.shape
    eye = jnp.eye(g, dtype=x.dtype)
    return (x[:, :, None, :] * eye[:, None, :, None]).reshape(g * a, g * b)


def kernel(x_prompt, x_sample, cache_mla_latent, cache_mla_krope, state_s5_re, state_s5_im, state_hgrn, cache_mem_k, cache_mem_v, page_table, mem_prompt, norm_mix, norm_mem, norm_memsrc, norm_mlp, w_mem_q, w_mem_k, w_mem_v, w_mem_o, mem_q_gain, mem_k_gain, w_mlp_up, w_mlp_down, w_in_even, mla_cq_norm, mla_ckv_norm, w_mla_uq, w_mla_ukv, mla_qn_nope, mla_qn_rope, mla_kn_nope, mla_kn_rope, s5_lambda_re, s5_lambda_im, s5_log_step, s5_b_re, s5_b_im, s5_c_re, s5_c_im, s5_d, s5_w_glu, s5_b_glu, w_out_even, w_in_odd, hgrn_lower_bounds, hgrn_out_norm, w_out_odd):
    bsz, seq, _ = x_prompt.shape
    dbs, dseq, _ = x_sample.shape
    depth = norm_mix.shape[0]
    past_len = page_table.shape[1] * PAGE_SIZE
    ns = 8
    mem_len = mem_prompt.shape[1]
    row2 = lambda a: a.reshape(1, -1).astype(F32)

    hp = x_prompt.reshape(bsz * seq, D_MODEL)
    hs = jnp.pad(x_sample, ((0, 0), (0, ns - dseq), (0, 0))).reshape(dbs * ns, D_MODEL)

    tabs_p = _rope_tables(jnp.arange(seq, dtype=jnp.int32))
    pos_s = past_len + jnp.arange(ns, dtype=jnp.int32)
    tabs_s = tuple(jnp.tile(t, (dbs, 1)) for t in _rope_tables(pos_s))

    outs_p = {k: [] for k in ("lat", "kr", "s5r", "s5i", "hg", "mk", "mv")}
    outs_s = {k: [] for k in ("lat", "kr", "s5r", "s5i", "hg")}

    tm_p = 512
    nl_p = seq // tm_p
    tm_r = 1024
    nl_r = seq // tm_r

    for l in range(depth):
        if l % 2 == 0:
            e = l // 2
            w_in = w_in_even[e]
            o1 = MLA_Q_LORA + MLA_KV_LORA
            wp = jnp.concatenate([w_in[:, :o1], _pad_last(w_in[:, o1:o1 + MLA_ROPE], LANES),
                                  w_in[:, o1 + MLA_ROPE:]], axis=1).astype(BF16)
            wuq = _head_pad(w_mla_uq[e], MLA_QK).astype(BF16)
            ukv = w_mla_ukv[e].reshape(MLA_KV_LORA, MLA_HEADS, MLA_NOPE + MLA_V)
            wuk_c = ukv[:, :, :MLA_NOPE].reshape(MLA_KV_LORA, -1)
            wuv_c = ukv[:, :, MLA_NOPE:].reshape(MLA_KV_LORA, -1)
            wkv = jnp.concatenate([_head_pad(wuk_c, MLA_NOPE), _head_pad(wuv_c, MLA_V)], axis=1).astype(BF16)
            qg = _pad_last(jnp.concatenate([mla_qn_nope[e], mla_qn_rope[e], mla_qn_rope[e]])[None, :], LANES)
            kg = _pad_last(jnp.concatenate([mla_kn_nope[e], mla_kn_rope[e], mla_kn_rope[e]])[None, :], LANES)
            cqn = row2(mla_cq_norm[e])
            ckvn = row2(mla_ckv_norm[e])
            g_mix = row2(norm_mix[l])
            score_bound = MLA_QK ** 0.5 * LOG2E * jnp.max(jnp.abs(qg)) * jnp.max(jnp.abs(kg))
            bounded = (score_bound <= MAX_SCORE_BOUND).astype(jnp.int32).reshape(1)
            lane = jnp.arange(LANES)
            aug = (jnp.where(lane == MLA_QK, 1.0, 0.0).astype(F32)[None, :],
                   jnp.where(lane == MLA_QK, -score_bound, 0.0).astype(F32)[None, :],
                   jnp.tile(jnp.where(lane == MLA_V, 1.0, 0.0).astype(F32), MLA_HEADS)[None, :])

            brm = _block_diag(jnp.swapaxes(s5_b_re[e], 1, 2)).astype(BF16)
            bim = _block_diag(jnp.swapaxes(s5_b_im[e], 1, 2)).astype(BF16)
            crm = _block_diag(jnp.swapaxes(s5_c_re[e], 1, 2)).astype(BF16)
            cim = _block_diag(jnp.swapaxes(s5_c_im[e], 1, 2)).astype(BF16)
            lamr = row2(s5_lambda_re[e])
            lami = row2(s5_lambda_im[e])
            lstep = row2(jnp.repeat(s5_log_step[e], S5_STATE))
            s5_consts = (lamr, lami, lstep, brm, bim, crm, cim, row2(s5_d[e]), s5_w_glu[e].astype(BF16),
                         row2(s5_b_glu[e]))
            w_out = w_out_even[e]
            wo_att_c = w_out[:MLA_HEADS * MLA_V].astype(BF16)
            wo_att_p = _pad_last(w_out[:MLA_HEADS * MLA_V].reshape(MLA_HEADS, MLA_V, D_MODEL).swapaxes(1, 2),
                                 LANES).swapaxes(1, 2).reshape(HP, D_MODEL).astype(BF16)
            wo_s5 = w_out[MLA_HEADS * MLA_V:].astype(BF16)

            q, k, v, ckv, kr, u = _even_proj(
                hp, g_mix, wp, cqn, wuq, qg, ckvn, wkv, kg, aug, tabs_p, tm=tm_p,
                u_shape=(seq, bsz * S5_WIDTH), emit_qk=False,
                u_spec=pl.BlockSpec((tm_p, S5_WIDTH), lambda i: (i % nl_p, i // nl_p)))
            o_att = _flash_attention(bounded, q.reshape(bsz, seq, HP), k.reshape(bsz, seq, HP), v, tq=512, hg=4)
            z0 = jnp.zeros((bsz, S5_NSTATE), F32)
            o_s5, hr, hi = _s5(u, z0, z0, *s5_consts, tt=64, nb=bsz, strip=512, interleave=True)
            hp = _mm_res([o_att.reshape(bsz * seq, HP), o_s5], [wo_att_p, wo_s5], hp,
                         tm=tm_r, op_specs=[row_spec(tm_r, HP),
                                            pl.BlockSpec((tm_r, S5_WIDTH), lambda i: (i % nl_r, i // nl_r))])
            outs_p["lat"].append(ckv.reshape(bsz, seq, MLA_KV_LORA))
            outs_p["kr"].append(kr.reshape(bsz, seq, MLA_ROPE))
            outs_p["s5r"].append(hr.reshape(bsz, S5_GROUPS, S5_STATE))
            outs_p["s5i"].append(hi.reshape(bsz, S5_GROUPS, S5_STATE))

            m_s = dbs * ns
            q, k, v, ckv, kr, u, qk = _even_proj(
                hs, g_mix, wp, cqn, wuq, qg, ckvn, wkv, kg, aug, tabs_s, tm=512,
                u_shape=(m_s, S5_WIDTH), u_spec=row_spec(512, S5_WIDTH), emit_qk=True)
            del q, k, v
            ckv3 = ckv.reshape(dbs, ns, MLA_KV_LORA)
            kr3 = kr.reshape(dbs, ns, MLA_ROPE)
            qk4 = qk.reshape(dbs, ns, MLA_HEADS, LANES)
            eye_h = jnp.eye(MLA_HEADS, dtype=BF16)
            ncols = ns * MLA_HEADS
            sub = LANES // MLA_HEADS
            nblk = MLA_NOPE // sub
            qn = (jnp.transpose(qk4[..., :MLA_NOPE], (0, 2, 3, 1))[..., None]
                  * eye_h[None, :, None, None, :])
            qn = qn.reshape(dbs, MLA_HEADS, nblk, sub, ncols).swapaxes(1, 2).reshape(dbs, MLA_HEADS * MLA_NOPE, ncols)
            qn = _pad_last(qn, LANES)
            wuk_p = (wuk_c.reshape(MLA_KV_LORA, MLA_HEADS, nblk, sub).swapaxes(1, 2)
                     .reshape(MLA_KV_LORA, MLA_HEADS * MLA_NOPE).astype(BF16))
            qr = jnp.transpose(qk4[..., MLA_NOPE:MLA_QK], (0, 3, 1, 2)).reshape(dbs, MLA_ROPE, ncols)
            qr = _pad_last(qr, LANES)
            colmask = (jnp.arange(LANES) < ncols)
            e16 = ((jnp.arange(LANES)[:, None] // sub == (jnp.arange(LANES)[None, :] % MLA_HEADS))
                   & colmask[None, :]).astype(BF16)
            onr = jnp.broadcast_to(colmask[None, :], (MLA_ROPE, LANES)).astype(BF16)
            zb = lambda r: jnp.zeros((dbs, r, LANES), BF16)
            bc = lambda x: jnp.broadcast_to(x[None], (dbs,) + x.shape)
            rhs2 = jnp.concatenate([
                jnp.concatenate([bc(e16), zb(LANES)], axis=2),
                jnp.concatenate([zb(MLA_ROPE), qr], axis=2),
                jnp.concatenate([bc(onr), zb(MLA_ROPE)], axis=2),
                jnp.zeros((dbs, LANES - 2 * MLA_ROPE, 2 * LANES), BF16)], axis=1)
            nnew = 16
            cnew = jnp.pad(ckv3, ((0, 0), (0, nnew - ns), (0, 0)))
            krnew = jnp.pad(kr3, ((0, 0), (0, nnew - ns), (0, 0)))
            o_att_s = _paged_attention(page_table, bounded, cache_mla_latent, jnp.swapaxes(cache_mla_krope, 2, 3), e,
                                       qn, rhs2, wuk_p, cnew, krnew, wuv_c.astype(BF16),
                                       jnp.full((1, LANES), score_bound, F32), npg=32, ngrp=4, nq=ns)
            u_tb = jnp.transpose(u.reshape(dbs, ns, S5_WIDTH)[:, :dseq], (1, 0, 2)).reshape(dseq * dbs, S5_WIDTH)
            o_s5, hr, hi = _s5(u_tb, state_s5_re[e].reshape(dbs, S5_NSTATE), state_s5_im[e].reshape(dbs, S5_NSTATE),
                               *s5_consts, tt=dseq, nb=dbs, strip=512, interleave=False)
            o_s5 = jnp.transpose(o_s5.reshape(dseq, dbs, S5_WIDTH), (1, 0, 2))
            o_s5 = jnp.pad(o_s5, ((0, 0), (0, ns - dseq), (0, 0))).reshape(m_s, S5_WIDTH)
            hs = _mm_res([o_att_s.reshape(m_s, MLA_HEADS * MLA_V), o_s5], [wo_att_c, wo_s5], hs, tm=m_s,
                         op_specs=[row_spec(m_s, MLA_HEADS * MLA_V), row_spec(m_s, S5_WIDTH)])
            outs_s["lat"].append(ckv3[:, :dseq])
            outs_s["kr"].append(kr3[:, :dseq])
            outs_s["s5r"].append(hr.reshape(dbs, S5_GROUPS, S5_STATE))
            outs_s["s5i"].append(hi.reshape(dbs, S5_GROUPS, S5_STATE))
        else:
            o = l // 2
            g_mix = row2(norm_mix[l])
            w_in = w_in_odd[o].astype(BF16)
            w_out = w_out_odd[o].astype(BF16)
            on = row2(hgrn_out_norm[o])
            lbp = hgrn_lower_bounds.astype(F32)

            proj = _norm_mm(hp, g_mix, w_in, tm=1024, tn=2048)
            s_zero = jnp.zeros((bsz, HGRN_HEADS, HGRN_DK, HGRN_DK), F32)
            og, st = _hgrn(proj.reshape(bsz, seq, 4 * D_MODEL), lbp, on, s_zero, chunk=64, nchunk=4, layer=l,
                           l_valid=None)
            hp = _mm_res([og.reshape(bsz * seq, D_MODEL)], [w_out], hp, tm=tm_r, op_specs=[row_spec(tm_r, D_MODEL)])
            outs_p["hg"].append(st)

            m_s = dbs * ns
            proj = _norm_mm(hs, g_mix, w_in, tm=m_s, tn=1024)
            lpad = HGRN_SUB
            proj = jnp.pad(proj.reshape(dbs, ns, 4 * D_MODEL), ((0, 0), (0, lpad - ns), (0, 0)))
            og, st = _hgrn(proj, lbp, on, state_hgrn[o], chunk=lpad, nchunk=1, layer=l, l_valid=dseq)
            hs = _mm_res([og[:, :ns].reshape(m_s, D_MODEL)], [w_out], hs, tm=m_s, op_specs=[row_spec(m_s, D_MODEL)])
            outs_s["hg"].append(st)

        g_mem = row2(norm_mem[l])
        wq = w_mem_q[l].astype(BF16)
        wo = w_mem_o[l].astype(BF16)
        mqg = row2(mem_q_gain[l])
        wkv_m = jnp.concatenate([w_mem_k[l], w_mem_v[l]], axis=1).astype(BF16)
        mk, mv = _mem_kv(mem_prompt.reshape(bsz * mem_len, D_MODEL), row2(norm_memsrc[l]), wkv_m,
                         row2(mem_k_gain[l]), tm=512)
        mk = mk.reshape(bsz, mem_len, MEM_WIDTH)
        mv = mv.reshape(bsz, mem_len, MEM_WIDTH)
        outs_p["mk"].append(mk.reshape(bsz, mem_len, MEM_HEADS, MEM_HEAD_DIM))
        outs_p["mv"].append(mv.reshape(bsz, mem_len, MEM_HEADS, MEM_HEAD_DIM))
        hp = _mem_attn(hp.reshape(bsz, seq, D_MODEL), g_mem, wq, mqg, mk, mv, wo,
                       nb=1, tl=512).reshape(bsz * seq, D_MODEL)
        pair_shape = (depth, dbs, mem_len // 2, 2 * MEM_HEADS, MEM_HEAD_DIM)
        hs = _mem_attn(hs.reshape(dbs, ns, D_MODEL), g_mem, wq, mqg, cache_mem_k.reshape(pair_shape),
                       cache_mem_v.reshape(pair_shape), wo, nb=8, tl=ns, layer=l).reshape(dbs * ns, D_MODEL)

        g_mlp = row2(norm_mlp[l])
        wu = w_mlp_up[l].astype(BF16)
        wd = w_mlp_down[l].astype(BF16)
        hp = _mlp(hp, g_mlp, wu, wd, tm=1024, tf=2048)
        hs = _mlp(hs, g_mlp, wu, wd, tm=dbs * ns, tf=1024)

    y_p = hp.reshape(bsz, seq, D_MODEL)
    y_s = hs.reshape(dbs, ns, D_MODEL)[:, :dseq]
    return (y_p, y_s,
            jnp.stack(outs_p["lat"], axis=1), jnp.stack(outs_p["kr"], axis=1),
            jnp.stack(outs_p["s5r"]), jnp.stack(outs_p["s5i"]), jnp.stack(outs_p["hg"]),
            jnp.stack(outs_p["mk"]), jnp.stack(outs_p["mv"]),
            jnp.stack(outs_s["lat"], axis=1), jnp.stack(outs_s["kr"], axis=1),
            jnp.stack(outs_s["s5r"]), jnp.stack(outs_s["s5i"]), jnp.stack(outs_s["hg"]))
```

```python
import functools
import math

import jax
import jax.numpy as jnp
from jax import lax
from jax.experimental import pallas as pl
from jax.experimental.pallas import tpu as pltpu

F32 = jnp.float32
BF16 = jnp.bfloat16

LANES = 128
VMEM_LIMIT_BYTES = 56 * 1024 * 1024

D_MODEL = 1024
MLA_HEADS = 8
MLA_NOPE = 64
MLA_ROPE = 32
MLA_QK = MLA_NOPE + MLA_ROPE
MLA_V = 64
MLA_Q_LORA = 768
MLA_KV_LORA = 256
ROPE_THETA = 10000.0
PAGE_SIZE = 128
S5_WIDTH = 512
S5_GROUP = 16
S5_GROUPS = S5_WIDTH // S5_GROUP
S5_STATE = 64
S5_NSTATE = S5_GROUPS * S5_STATE
HGRN_HEADS = 8
HGRN_DK = 128
HGRN_SUB = 32
HGRN_EXP_CLAMP = 80.0
MAX_SCORE_BOUND = 40.0
MEM_HEADS = 4
MEM_HEAD_DIM = 128
MEM_WIDTH = MEM_HEADS * MEM_HEAD_DIM
D_FF = 4 * D_MODEL
EPS = 1e-6
LOG2E = math.log2(math.e)
HP = MLA_HEADS * LANES

TM_EVEN_PROJ = 512
TQ_ATTN = 512
ATTN_HEAD_GROUP = 4
PAGES_PER_STEP = 32
PAGE_GROUPS = 4
S5_STEPS = 64
S5_STRIP = 512
TM_RESIDUAL = 1024
TM_NORM_MM, TN_NORM_MM = 1024, 2048
TM_MLP, TF_MLP = 1024, 2048
HGRN_CHUNK, HGRN_NCHUNK = 64, 4
TL_MEM = 1024
TM_MEM_KV = 512
SAMPLE_MEM_SEQS = 8
SAMPLE_PAD = 8
NEW_KEYS_PAD = 16


def _cparams(sem):
    return pltpu.CompilerParams(dimension_semantics=sem, vmem_limit_bytes=VMEM_LIMIT_BYTES)


def _rms(x, g):
    return x * lax.rsqrt(jnp.mean(x * x, axis=-1, keepdims=True) + EPS) * g


def _sigmoid(x):
    return 1.0 / (1.0 + jnp.exp(-x))


def _dot(a, b):
    return jnp.dot(a, b, preferred_element_type=F32)


def _dot_nt(a, b):
    return lax.dot_general(a, b, (((1,), (1,)), ((), ())), preferred_element_type=F32)


def _dot_tn(a, b):
    return lax.dot_general(a, b, (((0,), (0,)), ((), ())), preferred_element_type=F32)


def _row_to_col(row, n):
    r = lax.broadcasted_iota(jnp.int32, (n, n), 0)
    c = lax.broadcasted_iota(jnp.int32, (n, n), 1)
    return jnp.sum(jnp.where(r == c, jnp.broadcast_to(row, (n, n)), 0.0), axis=1, keepdims=True)


def _div_pow2(x, d):
    return lax.shift_right_logical(x, int(math.log2(d)))


def _full_spec(shape):
    nd = len(shape)
    return pl.BlockSpec(shape, lambda *_: (0,) * nd)


def row_spec(tm, width):
    return pl.BlockSpec((tm, width), lambda i: (i, 0))


def _even_proj_kernel(h_ref, g_ref, wp_ref, cqn_ref, wuq_ref, qg_ref, ckvn_ref, wkv_ref, kg_ref,
                      qaug_ref, kaug_ref, vaug_ref, cq_ref, s1q_ref, s2q_ref, ck_ref, s1k_ref, s2k_ref,
                      q_out, k_out, v_out, ckv_out, kr_out, u_out, *maybe_qk_out, nrg):
    o1 = MLA_Q_LORA
    o2 = o1 + MLA_KV_LORA
    o3 = o2 + LANES
    half = MLA_ROPE // 2
    inv_qk = 1.0 / MLA_QK
    qg = qg_ref[...]
    kg = kg_ref[...]
    tm = h_ref.shape[0]
    groups = [slice(tm // nrg * t, tm // nrg * (t + 1)) for t in range(nrg)]
    proj = [_dot(_rms(h_ref[r, :], g_ref[...]).astype(BF16), wp_ref[...]) for r in groups]
    qf, kv = [], []
    for r, p in zip(groups, proj):
        u_out[r, :] = p[:, o3:]
        qf.append(_dot(_rms(p[:, :o1], cqn_ref[...]).astype(BF16), wuq_ref[...]))
        ckv = _rms(p[:, o1:o2], ckvn_ref[...])
        ckv_out[r, :] = ckv
        kv.append(_dot(ckv.astype(BF16), wkv_ref[...]))
    for r, p, qfr, kvr in zip(groups, proj, qf, kv):
        kr = p[:, o2:o3]
        krr = (kr * ck_ref[r, :] + pltpu.roll(kr, half, 1) * s1k_ref[r, :]
               + pltpu.roll(kr, LANES - half, 1) * s2k_ref[r, :])
        kr_out[r, :] = krr[:, :MLA_ROPE]
        kr_sh = pltpu.roll(krr, MLA_NOPE, 1)
        cq_t, s1q_t, s2q_t = cq_ref[r, :], s1q_ref[r, :], s2q_ref[r, :]
        for h in range(MLA_HEADS):
            sl = slice(LANES * h, LANES * (h + 1))
            qh = qfr[:, sl]
            qh = qh * lax.rsqrt(jnp.sum(qh * qh, axis=-1, keepdims=True) * inv_qk + EPS) * qg
            qh = (qh * cq_t + pltpu.roll(qh, half, 1) * s1q_t + pltpu.roll(qh, LANES - half, 1) * s2q_t)
            q_out[r, sl] = (qh + qaug_ref[...]).astype(BF16)
            if maybe_qk_out:
                maybe_qk_out[0][r, sl] = (qh * kg).astype(BF16)
            kh = kvr[:, sl] + kr_sh
            kh = kh * lax.rsqrt(jnp.sum(kh * kh, axis=-1, keepdims=True) * inv_qk + EPS) * kg
            k_out[r, sl] = (kh + kaug_ref[...]).astype(BF16)
        v_out[:, r] = (kvr[:, HP:] + vaug_ref[...]).T.astype(BF16)


def _even_proj(h, g, wp, cqn, wuq, qg, ckvn, wkv, kg, aug, tabs, *, tm, u_shape, u_spec, emit_qk):
    m = h.shape[0]
    ltab = tabs[0].shape[0]
    ntab = ltab // tm
    row = lambda i: (i, 0)
    tab_spec = pl.BlockSpec((tm, LANES), lambda i: (i % ntab, 0))
    in_specs = ([pl.BlockSpec((tm, D_MODEL), row), _full_spec(g.shape), _full_spec(wp.shape),
                 _full_spec(cqn.shape), _full_spec(wuq.shape), _full_spec(qg.shape),
                 _full_spec(ckvn.shape), _full_spec(wkv.shape), _full_spec(kg.shape)]
                + [_full_spec(a.shape) for a in aug] + [tab_spec] * 6)
    out_shape = [jax.ShapeDtypeStruct((m, HP), BF16), jax.ShapeDtypeStruct((m, HP), BF16),
                 jax.ShapeDtypeStruct((m // tm, HP, tm), BF16), jax.ShapeDtypeStruct((m, MLA_KV_LORA), F32),
                 jax.ShapeDtypeStruct((m, MLA_ROPE), F32), jax.ShapeDtypeStruct(u_shape, F32)]
    out_specs = [pl.BlockSpec((tm, HP), row), pl.BlockSpec((tm, HP), row),
                 pl.BlockSpec((None, HP, tm), lambda i: (i, 0, 0)),
                 pl.BlockSpec((tm, MLA_KV_LORA), row), pl.BlockSpec((tm, MLA_ROPE), row),
                 u_spec]
    if emit_qk:
        out_shape.append(jax.ShapeDtypeStruct((m, HP), BF16))
        out_specs.append(pl.BlockSpec((tm, HP), row))
    return pl.pallas_call(
        functools.partial(_even_proj_kernel, nrg=2), grid=(m // tm,), in_specs=in_specs, out_specs=out_specs,
        out_shape=out_shape,
        compiler_params=_cparams(("arbitrary",)), name="even_proj",
    )(h, g, wp, cqn, wuq, qg, ckvn, wkv, kg, *aug, *tabs)


def _flash_kernel(bounded_ref, q_ref, k_ref, vt_ref, o_ref, *, tq, tk, hg):
    i = pl.program_id(1)
    nfull = (i * tq) // tk
    key = lax.broadcasted_iota(jnp.int32, (tk, tq), 0) + nfull * tk
    qry = lax.broadcasted_iota(jnp.int32, (tk, tq), 1) + i * tq
    causal = qry >= key

    heads = [slice(LANES * h, LANES * (h + 1)) for h in range(MLA_HEADS)]

    def accumulate(j, acc, masked):
        off = pl.multiple_of(j * tk, tk)
        out = []
        for h0 in range(0, MLA_HEADS, hg):
            grp = range(h0, h0 + hg)
            scores = [_dot_nt(k_ref[pl.ds(off, tk), heads[h]], q_ref[:, heads[h]]) for h in grp]
            if masked:
                scores = [jnp.where(causal, s, -jnp.inf) for s in scores]
            probs = [jnp.exp2(s).astype(BF16) for s in scores]
            out += [acc[h] + _dot(vt_ref[j, heads[h], :], p) for p, h in zip(probs, grp)]
        return tuple(out)

    @pl.when(bounded_ref[0] != 0)
    def _():
        acc = lax.fori_loop(0, nfull, lambda j, c: accumulate(j, c, False),
                            (jnp.zeros((LANES, tq), F32),) * MLA_HEADS)
        acc = accumulate(nfull, acc, True)
        for h, sl in enumerate(heads):
            o_ref[:, sl] = (acc[h] / acc[h][MLA_V:MLA_V + 1, :]).T.astype(BF16)

    def update(j, carry, masked):
        off = pl.multiple_of(j * tk, tk)
        out = []
        for h0 in range(0, MLA_HEADS, hg):
            grp = range(h0, h0 + hg)
            scores = [_dot_nt(k_ref[pl.ds(off, tk), heads[h]], q_ref[:, heads[h]]) for h in grp]
            probs, stats = [], []
            for h, s in zip(grp, scores):
                m, l = carry[3 * h], carry[3 * h + 1]
                if masked:
                    s = jnp.where(causal, s, -jnp.inf)
                m_new = jnp.maximum(m, jnp.max(s, axis=0, keepdims=True))
                alpha = jnp.exp2(m - m_new)
                p = jnp.exp2(s - m_new)
                stats.append((m_new, alpha * l + jnp.sum(p, axis=0, keepdims=True), alpha))
                probs.append(p.astype(BF16))
            pv = [_dot(vt_ref[j, heads[h], :], p) for p, h in zip(probs, grp)]
            for h, (m_new, l_new, alpha), o in zip(grp, stats, pv):
                out += [m_new, l_new, alpha * carry[3 * h + 2] + o]
        return tuple(out)

    @pl.when(bounded_ref[0] == 0)
    def _():
        init = (jnp.full((1, tq), -jnp.inf, F32), jnp.zeros((1, tq), F32), jnp.zeros((LANES, tq), F32))
        carry = lax.fori_loop(0, nfull, lambda j, c: update(j, c, False), init * MLA_HEADS)
        carry = update(nfull, carry, True)
        for h, sl in enumerate(heads):
            o_ref[:, sl] = (carry[3 * h + 2] / carry[3 * h + 1]).T.astype(BF16)


def _flash_attention(bounded, q, k, vt, *, tq, hg=2):
    b, l, _ = q.shape
    tk = vt.shape[2]
    nkb = l // tk
    grid_spec = pltpu.PrefetchScalarGridSpec(
        num_scalar_prefetch=1, grid=(b, l // tq),
        in_specs=[pl.BlockSpec((None, tq, HP), lambda bi, i, f: (bi, i, 0)),
                  pl.BlockSpec((None, l, HP), lambda bi, i, f: (bi, 0, 0)),
                  pl.BlockSpec((nkb, HP, tk), lambda bi, i, f: (bi, 0, 0))],
        out_specs=pl.BlockSpec((None, tq, HP), lambda bi, i, f: (bi, i, 0)))
    return pl.pallas_call(
        functools.partial(_flash_kernel, tq=tq, tk=tk, hg=hg), grid_spec=grid_spec,
        out_shape=jax.ShapeDtypeStruct((b, l, HP), BF16),
        compiler_params=_cparams(("arbitrary", "arbitrary")), name="prompt_attention",
    )(bounded, q, k, vt)


def _paged_kernel(pt_ref, bounded_ref, *refs, npg, ngrp, nsteps, nq):
    lat_refs = refs[:npg]
    krt_refs = refs[npg:2 * npg]
    (qn_ref, rhs2_ref, wuk_ref, cnew_ref, krnew_ref, wuv_ref, bound_ref,
     o_ref, wabs, m_scr, l_scr, a_scr, l_acc, a_acc) = refs[2 * npg:]
    del pt_ref
    s = pl.program_id(1)
    nslots = nsteps * ngrp + 1
    ncol = LANES
    inv_qk = 1.0 / MLA_QK
    bounded = bounded_ref[0] != 0
    last = s == nsteps - 1

    @pl.when(s == 0)
    def _():
        wabs[...] = _dot(wuk_ref[...], qn_ref[...]).astype(BF16)
        l_acc[...] = jnp.zeros(l_acc.shape, F32)
        a_acc[...] = jnp.zeros(a_acc.shape, F32)

    def scores(blocks, mask):
        kn = [_dot(c, wuk_ref[...]) for c, _ in blocks]
        sq = [k * k for k in kn]
        psum = [q[:, 0:LANES] + q[:, LANES:2 * LANES] + q[:, 2 * LANES:3 * LANES] + q[:, 3 * LANES:] for q in sq]
        r2 = [_dot(jnp.concatenate([p.astype(BF16), x], axis=1), rhs2_ref[...]) for p, (_, x) in zip(psum, blocks)]
        scn = [_dot(c, wabs[...]) for c, _ in blocks]
        out = []
        for t in range(len(blocks)):
            sc = (scn[t] + r2[t][:, LANES:]) * lax.rsqrt(r2[t][:, :LANES] * inv_qk + EPS)
            out.append(sc if mask is None else jnp.where(mask, sc, -jnp.inf))
        return out

    def stats(blocks, scs):
        probs, out = [], []
        for sc in scs:
            m = jnp.max(sc, axis=0, keepdims=True)
            p = jnp.exp2(sc - m)
            out.append((m, jnp.sum(p, axis=0, keepdims=True)))
            probs.append(p.astype(BF16))
        acc = [_dot_tn(p, c) for p, (c, _) in zip(probs, blocks)]
        return [(m, l, a) for (m, l), a in zip(out, acc)]

    def accumulate(blocks, scs):
        probs = [jnp.exp2(sc - bound_ref[...]) for sc in scs]
        l_new = l_acc[...]
        for p in probs:
            l_new = l_new + jnp.sum(p, axis=0, keepdims=True)
        l_acc[...] = l_new
        a_new = a_acc[...]
        for p, (c, _) in zip(probs, blocks):
            a_new = a_new + _dot_tn(p.astype(BF16), c)
        a_acc[...] = a_new

    def finish(num):
        full = _dot(num.astype(BF16), wuv_ref[...])
        hrow = lax.broadcasted_iota(jnp.int32, (MLA_HEADS, MLA_HEADS * MLA_V), 0)
        hcol = _div_pow2(lax.broadcasted_iota(jnp.int32, (MLA_HEADS, MLA_HEADS * MLA_V), 1), MLA_V)
        rows = []
        for qi in range(nq):
            blk = full[MLA_HEADS * qi:MLA_HEADS * (qi + 1), :]
            rows.append(jnp.sum(jnp.where(hrow == hcol, blk, 0.0), axis=0, keepdims=True))
        o_ref[...] = jnp.concatenate(rows, axis=0)

    zpad = jnp.zeros((LANES - 2 * MLA_ROPE, PAGE_SIZE), F32)

    def rope_block(g):
        krt = krt_refs[g][...]
        return jnp.concatenate([krt, krt * krt, zpad], axis=0).T.astype(BF16)

    pg = npg // ngrp
    groups = [(jnp.concatenate([lat_refs[g][...].astype(BF16) for g in range(pg * t, pg * (t + 1))], axis=0),
               jnp.concatenate([rope_block(g) for g in range(pg * t, pg * (t + 1))], axis=0))
              for t in range(ngrp)]
    scs = scores(groups, None)

    @pl.when(bounded)
    def _():
        accumulate(groups, scs)

    @pl.when(jnp.logical_not(bounded))
    def _():
        for t, (m, l, a) in enumerate(stats(groups, scs)):
            slot = s * ngrp + t
            m_scr[pl.ds(slot, 1), :] = m
            l_scr[pl.ds(slot, 1), :] = l
            a_scr[slot] = a

    def new_block():
        nnew = cnew_ref.shape[0]
        krn = jnp.concatenate([krnew_ref[...], jnp.zeros((nnew, LANES - MLA_ROPE), F32)], axis=1)
        krn = krn + pltpu.roll(krn * krn, MLA_ROPE, 1)
        key = lax.broadcasted_iota(jnp.int32, (nnew, ncol), 0)
        qry = _div_pow2(lax.broadcasted_iota(jnp.int32, (nnew, ncol), 1), MLA_HEADS)
        blocks = [(cnew_ref[...].astype(BF16), krn.astype(BF16))]
        return blocks, scores(blocks, key <= qry)

    @pl.when(jnp.logical_and(last, bounded))
    def _():
        accumulate(*new_block())
        finish(a_acc[...] * _row_to_col(1.0 / l_acc[...], ncol))

    @pl.when(jnp.logical_and(last, jnp.logical_not(bounded)))
    def _():
        blocks, scs_new = new_block()
        (m2, l2, a2), = stats(blocks, scs_new)
        m_scr[nslots - 1:nslots, :] = m2
        l_scr[nslots - 1:nslots, :] = l2
        a_scr[nslots - 1] = a2
        mall = m_scr[0:nslots, :]
        w = jnp.exp2(mall - jnp.max(mall, axis=0, keepdims=True))
        den = jnp.sum(l_scr[0:nslots, :] * w, axis=0, keepdims=True)
        wn = w / den
        num = jnp.zeros((ncol, MLA_KV_LORA), F32)
        for t in range(nslots):
            num = num + a_scr[t] * _row_to_col(wn[t:t + 1, :], ncol)
        finish(num)


def _paged_attention(page_table, bounded, cache_lat, cache_krt, e, qn, rhs2, wuk, cnew, krnew, wuv, bound, *,
                     npg, ngrp, nq):
    nb, npages = page_table.shape
    nsteps = npages // npg
    nnew = cnew.shape[1]
    nslots = nsteps * ngrp + 1

    def page_spec(shape, g):
        return pl.BlockSpec((None, None) + shape, lambda b, s, pt, f: (pt[b, s * npg + g], e, 0, 0))

    per_b3 = lambda b, s, pt, f: (b, 0, 0)
    const2 = lambda b, s, pt, f: (0, 0)
    in_specs = ([page_spec((PAGE_SIZE, MLA_KV_LORA), g) for g in range(npg)]
                + [page_spec((MLA_ROPE, PAGE_SIZE), g) for g in range(npg)]
                + [pl.BlockSpec((None,) + qn.shape[1:], per_b3), pl.BlockSpec((None,) + rhs2.shape[1:], per_b3),
                   pl.BlockSpec(wuk.shape, const2),
                   pl.BlockSpec((None, nnew, MLA_KV_LORA), per_b3), pl.BlockSpec((None, nnew, MLA_ROPE), per_b3),
                   pl.BlockSpec(wuv.shape, const2), pl.BlockSpec(bound.shape, const2)])
    grid_spec = pltpu.PrefetchScalarGridSpec(
        num_scalar_prefetch=2, grid=(nb, nsteps), in_specs=in_specs,
        out_specs=pl.BlockSpec((None, nq, MLA_HEADS * MLA_V), per_b3),
        scratch_shapes=[pltpu.VMEM((MLA_KV_LORA, LANES), BF16),
                        pltpu.VMEM((nslots, LANES), F32), pltpu.VMEM((nslots, LANES), F32),
                        pltpu.VMEM((nslots, LANES, MLA_KV_LORA), F32),
                        pltpu.VMEM((1, LANES), F32), pltpu.VMEM((LANES, MLA_KV_LORA), F32)])
    return pl.pallas_call(
        functools.partial(_paged_kernel, npg=npg, ngrp=ngrp, nsteps=nsteps, nq=nq),
        grid_spec=grid_spec, out_shape=jax.ShapeDtypeStruct((nb, nq, MLA_HEADS * MLA_V), F32),
        compiler_params=_cparams(("arbitrary", "arbitrary")), name="paged_attention",
    )(page_table, bounded, *([cache_lat] * npg), *([cache_krt] * npg), qn, rhs2, wuk, cnew, krnew, wuv, bound)


def _s5_kernel(u_ref, h0r_ref, h0i_ref, lamr_ref, lami_ref, lstep_ref, brm_ref, bim_ref, crm_ref, cim_ref,
               d_ref, wg_ref, bg_ref, o_ref, hr_out, hi_out, xr_scr, xi_scr, hcr, hci, disc, io_scr, *,
               tt, nb, strip, interleave):
    c = pl.program_id(0)

    @pl.when(c == 0)
    def _():
        lr = jnp.minimum(lamr_ref[...], -1e-4)
        li = lami_ref[...]
        dt = jnp.exp(lstep_ref[...])
        mag = jnp.exp(lr * dt)
        abr = mag * jnp.cos(li * dt)
        abi = mag * jnp.sin(li * dt)
        den = lr * lr + li * li
        disc[0:1, :] = abr
        disc[1:2, :] = abi
        disc[2:3, :] = ((abr - 1.0) * lr + abi * li) / den
        disc[3:4, :] = (abi * lr - (abr - 1.0) * li) / den
        hcr[...] = h0r_ref[...]
        hci[...] = h0i_ref[...]

    nlb = S5_WIDTH // LANES
    if interleave:
        for b in range(nb):
            for j in range(nlb):
                c0 = S5_WIDTH * b + LANES * j
                io_scr[j, pl.ds(b, tt, stride=nb), :] = u_ref[:, c0:c0 + LANES]
        u = jnp.concatenate([io_scr[j] for j in range(nlb)], axis=1)
    else:
        u = u_ref[...]
    ub = u.astype(BF16)
    kc = 2 * LANES
    ks = kc * S5_STATE // S5_GROUP
    for k in range(S5_WIDTH // kc):
        cols = slice(kc * k, kc * (k + 1))
        sts = slice(ks * k, ks * (k + 1))
        pr = _dot(ub[:, cols], brm_ref[cols, sts])
        pi = _dot(ub[:, cols], bim_ref[cols, sts])
        cor = disc[2:3, sts]
        coi = disc[3:4, sts]
        xr_scr[:, sts] = cor * pr - coi * pi
        xi_scr[:, sts] = cor * pi + coi * pr

    for s0 in range(0, S5_NSTATE, strip):
        lanes = slice(s0, s0 + strip)
        ar = jnp.broadcast_to(disc[0:1, lanes], (nb, strip))
        ai = jnp.broadcast_to(disc[1:2, lanes], (nb, strip))

        def step(t, carry, lanes=lanes, ar=ar, ai=ai):
            hr, hi = carry
            rows = pl.ds(pl.multiple_of(t * nb, nb), nb)
            nr = ar * hr - ai * hi + xr_scr[rows, lanes]
            ni = ar * hi + ai * hr + xi_scr[rows, lanes]
            xr_scr[rows, lanes] = nr
            xi_scr[rows, lanes] = ni
            return nr, ni

        hr, hi = lax.fori_loop(0, tt, step, (hcr[:, lanes], hci[:, lanes]))
        hcr[:, lanes] = hr
        hci[:, lanes] = hi

    ys = []
    for k in range(S5_WIDTH // kc):
        cols = slice(kc * k, kc * (k + 1))
        sts = slice(ks * k, ks * (k + 1))
        ys.append(_dot(xr_scr[:, sts].astype(BF16), crm_ref[sts, cols])
                  - _dot(xi_scr[:, sts].astype(BF16), cim_ref[sts, cols]))
    y = jnp.concatenate(ys, axis=1) + d_ref[...] * u
    z = jax.nn.gelu(y)
    gate = _sigmoid(_dot(z.astype(BF16), wg_ref[...]) + bg_ref[...])
    if interleave:
        out = z * gate
        for j in range(nlb):
            io_scr[j] = out[:, LANES * j:LANES * (j + 1)]
        for b in range(nb):
            for j in range(nlb):
                c0 = S5_WIDTH * b + LANES * j
                o_ref[:, c0:c0 + LANES] = io_scr[j, pl.ds(b, tt, stride=nb), :].astype(BF16)
    else:
        o_ref[...] = (z * gate).astype(BF16)

    @pl.when(c == pl.num_programs(0) - 1)
    def _():
        hr_out[...] = hcr[...]
        hi_out[...] = hci[...]


def _s5(u, h0r, h0i, lamr, lami, lstep, brm, bim, crm, cim, d, wg, bg, *, tt, nb, strip, interleave):
    steps = u.shape[0] if interleave else u.shape[0] // nb
    blk = tt * nb
    consts = (h0r, h0i, lamr, lami, lstep, brm, bim, crm, cim, d, wg, bg)
    io_spec = row_spec(tt, nb * S5_WIDTH) if interleave else row_spec(blk, S5_WIDTH)
    return pl.pallas_call(
        functools.partial(_s5_kernel, tt=tt, nb=nb, strip=strip, interleave=interleave),
        grid=(steps // tt,),
        in_specs=[io_spec] + [_full_spec(a.shape) for a in consts],
        out_specs=[io_spec, _full_spec((nb, S5_NSTATE)), _full_spec((nb, S5_NSTATE))],
        out_shape=[jax.ShapeDtypeStruct(u.shape, BF16),
                   jax.ShapeDtypeStruct((nb, S5_NSTATE), F32), jax.ShapeDtypeStruct((nb, S5_NSTATE), F32)],
        scratch_shapes=[pltpu.VMEM((blk, S5_NSTATE), F32), pltpu.VMEM((blk, S5_NSTATE), F32),
                        pltpu.VMEM((nb, S5_NSTATE), F32), pltpu.VMEM((nb, S5_NSTATE), F32),
                        pltpu.VMEM((8, S5_NSTATE), F32), pltpu.VMEM((S5_WIDTH // LANES, blk, LANES), F32)],
        compiler_params=_cparams(("arbitrary",)), name="s5",
    )(u, *consts)


def _mm_res_kernel(*refs, nop):
    res_ref = refs[2 * nop]
    o_ref = refs[2 * nop + 1]
    acc = res_ref[...]
    for t in range(nop):
        acc = acc + _dot(refs[t][...].astype(BF16), refs[nop + t][...])
    o_ref[...] = acc


def _mm_res(ops, ws, res, *, tm, op_specs):
    m, n = res.shape
    row = lambda i: (i, 0)
    in_specs = list(op_specs) + [_full_spec(w.shape) for w in ws] + [pl.BlockSpec((tm, n), row)]
    return pl.pallas_call(
        functools.partial(_mm_res_kernel, nop=len(ops)), grid=(m // tm,), in_specs=in_specs,
        out_specs=pl.BlockSpec((tm, n), row), out_shape=jax.ShapeDtypeStruct((m, n), F32),
        compiler_params=_cparams(("arbitrary",)), name="matmul_residual",
    )(*ops, *ws, res)


def _mem_kv_kernel(x_ref, g_ref, w_ref, kg_ref, k_out, v_out):
    mn = _rms(x_ref[...], g_ref[...]).astype(BF16)
    kv = _dot(mn, w_ref[...])
    kg = kg_ref[...]
    for h in range(MEM_HEADS):
        sl = slice(LANES * h, LANES * (h + 1))
        k_out[:, sl] = _rms(kv[:, sl], kg)
    v_out[...] = kv[:, MEM_WIDTH:]


def _mem_kv(x, g, w, kg, *, tm):
    m = x.shape[0]
    row = lambda i: (i, 0)
    return pl.pallas_call(
        _mem_kv_kernel, grid=(m // tm,),
        in_specs=[pl.BlockSpec((tm, D_MODEL), row), _full_spec(g.shape), _full_spec(w.shape), _full_spec(kg.shape)],
        out_specs=[pl.BlockSpec((tm, MEM_WIDTH), row), pl.BlockSpec((tm, MEM_WIDTH), row)],
        out_shape=[jax.ShapeDtypeStruct((m, MEM_WIDTH), F32), jax.ShapeDtypeStruct((m, MEM_WIDTH), F32)],
        compiler_params=_cparams(("arbitrary",)), name="mem_kv",
    )(x, g, w, kg)


def _mem_attn_kernel(h_ref, g_ref, wq_ref, qg_ref, mk_ref, mv_ref, wo_ref, o_ref, *, nb, tl, paired):
    x = h_ref[...].reshape(nb * tl, D_MODEL)
    hn = _rms(x, g_ref[...]).astype(BF16)
    q = _dot(hn, wq_ref[...])
    qg = qg_ref[...] * (MEM_HEAD_DIM ** -0.5 * LOG2E)
    qn = [_rms(q[:, LANES * h:LANES * (h + 1)], qg).astype(BF16) for h in range(MEM_HEADS)]

    def head_block(ref, b, h):
        if paired:
            return jnp.concatenate([ref[b, :, h, :], ref[b, :, MEM_HEADS + h, :]], axis=0).astype(BF16)
        return ref[b, :, LANES * h:LANES * (h + 1)].astype(BF16)

    pairs = [(b, h) for b in range(nb) for h in range(MEM_HEADS)]
    scores = [_dot_nt(qn[h][tl * b:tl * (b + 1), :], head_block(mk_ref, b, h)) for b, h in pairs]
    probs = [jnp.exp2(s - jnp.max(s, axis=-1, keepdims=True)) for s in scores]
    outs = [_dot(p.astype(BF16), head_block(mv_ref, b, h)) / jnp.sum(p, axis=-1, keepdims=True)
            for p, (b, h) in zip(probs, pairs)]
    rows = [jnp.concatenate(outs[MEM_HEADS * b:MEM_HEADS * (b + 1)], axis=1) for b in range(nb)]
    o = (rows[0] if nb == 1 else jnp.concatenate(rows, axis=0)).astype(BF16)
    o_ref[...] = (x + _dot(o, wo_ref[...])).reshape(nb, tl, D_MODEL)


def _mem_attn(h, g, wq, qg, mk, mv, wo, *, nb, tl, layer=None):
    b, l, _ = h.shape
    blk = lambda bi, i: (bi, i, 0)
    if layer is None:
        mem_spec = pl.BlockSpec((nb,) + mk.shape[1:], lambda bi, i: (bi, 0, 0))
    else:
        mem_spec = pl.BlockSpec((None, nb) + mk.shape[2:], lambda bi, i: (layer, bi, 0, 0, 0))
    return pl.pallas_call(
        functools.partial(_mem_attn_kernel, nb=nb, tl=tl, paired=layer is not None), grid=(b // nb, l // tl),
        in_specs=[pl.BlockSpec((nb, tl, D_MODEL), blk), _full_spec(g.shape), _full_spec(wq.shape),
                  _full_spec(qg.shape), mem_spec, mem_spec, _full_spec(wo.shape)],
        out_specs=pl.BlockSpec((nb, tl, D_MODEL), blk),
        out_shape=jax.ShapeDtypeStruct(h.shape, F32),
        compiler_params=_cparams(("arbitrary", "arbitrary")), name="mem_attention",
    )(h, g, wq, qg, mk, mv, wo)


def _mlp_kernel(h_ref, g_ref, wu_ref, wd_ref, o_ref, xn_scr, acc_scr):
    j = pl.program_id(1)

    @pl.when(j == 0)
    def _():
        xn_scr[...] = _rms(h_ref[...], g_ref[...]).astype(BF16)
        acc_scr[...] = jnp.zeros(acc_scr.shape, F32)

    a = _dot(xn_scr[...], wu_ref[...])
    a = jnp.square(jnp.maximum(a, 0.0)).astype(BF16)
    acc_scr[...] += _dot(a, wd_ref[...])

    @pl.when(j == pl.num_programs(1) - 1)
    def _():
        o_ref[...] = h_ref[...] + acc_scr[...]


def _mlp(h, g, wu, wd, *, tm, tf):
    m = h.shape[0]
    return pl.pallas_call(
        _mlp_kernel, grid=(m // tm, D_FF // tf),
        in_specs=[pl.BlockSpec((tm, D_MODEL), lambda i, j: (i, 0)), _full_spec(g.shape),
                  pl.BlockSpec((D_MODEL, tf), lambda i, j: (0, j)), pl.BlockSpec((tf, D_MODEL), lambda i, j: (j, 0))],
        out_specs=pl.BlockSpec((tm, D_MODEL), lambda i, j: (i, 0)),
        out_shape=jax.ShapeDtypeStruct((m, D_MODEL), F32),
        scratch_shapes=[pltpu.VMEM((tm, D_MODEL), BF16), pltpu.VMEM((tm, D_MODEL), F32)],
        compiler_params=_cparams(("arbitrary", "arbitrary")), name="mlp",
    )(h, g, wu, wd)


def _norm_mm_kernel(h_ref, g_ref, w_ref, o_ref, xn_scr):
    @pl.when(pl.program_id(1) == 0)
    def _():
        xn_scr[...] = _rms(h_ref[...], g_ref[...]).astype(BF16)

    o_ref[...] = _dot(xn_scr[...], w_ref[...])


def _norm_mm(h, g, w, *, tm, tn):
    m = h.shape[0]
    n = w.shape[1]
    return pl.pallas_call(
        _norm_mm_kernel, grid=(m // tm, n // tn),
        in_specs=[pl.BlockSpec((tm, D_MODEL), lambda i, j: (i, 0)), _full_spec(g.shape),
                  pl.BlockSpec((D_MODEL, tn), lambda i, j: (0, j))],
        out_specs=pl.BlockSpec((tm, tn), lambda i, j: (i, j)),
        out_shape=jax.ShapeDtypeStruct((m, n), F32),
        scratch_shapes=[pltpu.VMEM((tm, D_MODEL), BF16)],
        compiler_params=_cparams(("arbitrary", "arbitrary")), name="norm_matmul",
    )(h, g, w)


def _hgrn_kernel(q_ref, f_ref, i_ref, g_ref, lbp_ref, on_ref, s0_ref, o_ref, s_out, s_scr, *,
                 chunk, nchunk, layer, l_valid):
    c = pl.program_id(1)
    tb = chunk * nchunk

    @pl.when(c == 0)
    def _():
        s_scr[...] = s0_ref[...]

    lbp = lbp_ref[...]
    e = jnp.exp(lbp - jnp.max(lbp, axis=0, keepdims=True))
    sm = e / jnp.sum(e, axis=0, keepdims=True)
    lb = jnp.sum(sm[0:layer + 1, :], axis=0, keepdims=True) - sm[0:1, :]

    q = q_ref[...]
    qa = q * _sigmoid(q)
    fg = lb + (1.0 - lb) * _sigmoid(f_ref[...])
    logf = jnp.log(fg)
    kk = 1.0 - fg
    v = i_ref[...]
    if l_valid is not None:
        valid = (lax.broadcasted_iota(jnp.int32, (tb, 1), 0) + c * tb) < l_valid
        logf = jnp.where(valid, logf, 0.0)
        kk = jnp.where(valid, kk, 0.0)
    vb = v.astype(BF16)

    tr = lax.broadcasted_iota(jnp.int32, (tb, tb), 0)
    tc = lax.broadcasted_iota(jnp.int32, (tb, tb), 1)
    same_chunk = _div_pow2(tr, chunk) == _div_pow2(tc, chunk)
    tri = jnp.where(same_chunk, jnp.where(tr >= tc, 1.0, 0.0), 0.0).astype(BF16)
    hi = logf.astype(BF16)
    lo = (logf - hi.astype(F32)).astype(BF16)
    bcum = _dot(tri, hi) + _dot(tri, lo)
    qhat = (qa * jnp.exp(bcum)).astype(BF16)

    nsub = chunk // HGRN_SUB
    khat, dec, qloc, kloc, masks = [], [], [], [], []
    spread = jnp.zeros((1, bcum.shape[1]), F32)
    for ci in range(nchunk):
        c0 = ci * chunk
        blast = bcum[c0 + chunk - 1:c0 + chunk, :]
        khat.append((kk[c0:c0 + chunk, :] * jnp.exp(blast - bcum[c0:c0 + chunk, :])).astype(BF16))
        dec.append(jnp.exp(blast))
        for i in range(nsub):
            r0 = c0 + i * HGRN_SUB
            r1 = r0 + HGRN_SUB
            base = bcum[r0 - 1:r0, :] if i > 0 else jnp.zeros((1, bcum.shape[1]), F32)
            spread = jnp.minimum(spread, bcum[r1 - 1:r1, :] - base)
            qloc.append((qa[r0:r1, :] * jnp.exp(bcum[r0:r1, :] - base)).astype(BF16))
            kloc.append((kk[c0:r1, :] * jnp.exp(jnp.minimum(base - bcum[c0:r1, :], HGRN_EXP_CLAMP))).astype(BF16))
    wild = jnp.min(spread) < -HGRN_EXP_CLAMP
    for i in range(nsub):
        ncols = (i + 1) * HGRN_SUB
        ar = lax.broadcasted_iota(jnp.int32, (HGRN_SUB, ncols), 0) + i * HGRN_SUB
        ac = lax.broadcasted_iota(jnp.int32, (HGRN_SUB, ncols), 1)
        in_block = ac >= i * HGRN_SUB
        masks.append(jnp.logical_and(ar >= ac, jnp.logical_not(jnp.logical_and(wild, in_block))))

    hsl = [slice(HGRN_DK * h, HGRN_DK * (h + 1)) for h in range(HGRN_HEADS)]

    def in_block_exact():
        pos = jnp.bitwise_and(lax.broadcasted_iota(jnp.int32, (tb, 1), 0), HGRN_SUB - 1)
        out = jnp.zeros((tb, bcum.shape[1]), F32)
        for j in range(HGRN_SUB):
            ok = pos >= j
            kj, bj, vj = (kk, bcum, v) if j == 0 else (pltpu.roll(x, j, 0) for x in (kk, bcum, v))
            e = jnp.where(ok, qa * kj * jnp.exp(jnp.where(ok, bcum - bj, 0.0)), 0.0)
            out = out + jnp.concatenate(
                [jnp.sum(e[:, sl], axis=-1, keepdims=True) * vj[:, sl] for sl in hsl], axis=1)
        return out

    blocks = [(ci, i) for ci in range(nchunk) for i in range(nsub)]
    att = [[_dot_nt(qloc[ci * nsub + i][:, sl], kloc[ci * nsub + i][:, sl]) for ci, i in blocks] for sl in hsl]
    att = [[jnp.where(masks[i], a, 0.0).astype(BF16) for a, (ci, i) in zip(row, blocks)] for row in att]
    intra = [[_dot(a, vb[ci * chunk:ci * chunk + (i + 1) * HGRN_SUB, sl]) for a, (ci, i) in zip(row, blocks)]
             for row, sl in zip(att, hsl)]
    kv = [[_dot_tn(khat[ci][:, sl], vb[ci * chunk:(ci + 1) * chunk, sl]) for ci in range(nchunk)] for sl in hsl]
    dcol = [[_row_to_col(dec[ci][:, sl], HGRN_DK) for ci in range(nchunk)] for sl in hsl]
    st = [s_scr[h] for h in range(HGRN_HEADS)]
    inter = [[] for _ in hsl]
    for ci in range(nchunk):
        rows = slice(ci * chunk, (ci + 1) * chunk)
        for h, sl in enumerate(hsl):
            inter[h].append(_dot(qhat[rows, sl], st[h].astype(BF16)))
        for h in range(HGRN_HEADS):
            st[h] = dcol[h][ci] * st[h] + kv[h][ci]
    o_heads = []
    for h in range(HGRN_HEADS):
        s_scr[h] = st[h]
        parts = [inter[h][ci][i * HGRN_SUB:(i + 1) * HGRN_SUB, :] + intra[h][ci * nsub + i] for ci, i in blocks]
        o_heads.append(parts[0] if len(parts) == 1 else jnp.concatenate(parts, axis=0))

    o = jnp.concatenate(o_heads, axis=1)
    o = lax.cond(wild, lambda: o + in_block_exact(), lambda: o)
    g = g_ref[...]
    o_ref[...] = (_rms(o, on_ref[...]) * (g * _sigmoid(g))).astype(BF16)

    @pl.when(c == pl.num_programs(1) - 1)
    def _():
        s_out[...] = s_scr[...]


def _hgrn(proj, lbp, on, s0, *, chunk, nchunk, layer, l_valid):
    b, l, _ = proj.shape
    w = D_MODEL
    tb = chunk * nchunk

    def col(k):
        return pl.BlockSpec((None, tb, w), lambda bi, c: (bi, c, k))

    st_spec = pl.BlockSpec((None, HGRN_HEADS, HGRN_DK, HGRN_DK), lambda bi, c: (bi, 0, 0, 0))
    return pl.pallas_call(
        functools.partial(_hgrn_kernel, chunk=chunk, nchunk=nchunk, layer=layer, l_valid=l_valid),
        grid=(b, l // tb),
        in_specs=[col(0), col(1), col(2), col(3), _full_spec(lbp.shape), _full_spec(on.shape), st_spec],
        out_specs=[pl.BlockSpec((None, tb, w), lambda bi, c: (bi, c, 0)), st_spec],
        out_shape=[jax.ShapeDtypeStruct((b, l, w), BF16), jax.ShapeDtypeStruct(s0.shape, F32)],
        scratch_shapes=[pltpu.VMEM((HGRN_HEADS, HGRN_DK, HGRN_DK), F32)],
        compiler_params=_cparams(("arbitrary", "arbitrary")), name="hgrn",
    )(proj, proj, proj, proj, lbp, on, s0)


def _pad_last(x, n):
    return jnp.pad(x, [(0, 0)] * (x.ndim - 1) + [(0, n - x.shape[-1])])


def _head_pad(w, per):
    k = w.shape[0]
    return _pad_last(w.reshape(k, -1, per), LANES).reshape(k, -1)


def _rope_tables(pos):
    half = MLA_ROPE // 2
    inv = ROPE_THETA ** (-jnp.arange(half, dtype=F32) / half)
    ang = pos.astype(F32)[:, None] * inv[None, :]
    cos, sin = jnp.cos(ang), jnp.sin(ang)
    n = pos.shape[0]
    z = lambda w: jnp.zeros((n, w), F32)
    scale = MLA_QK ** -0.5 * LOG2E
    cq = scale * jnp.concatenate([jnp.ones((n, MLA_NOPE), F32), cos, cos, z(LANES - MLA_QK)], axis=1)
    s1q = scale * jnp.concatenate([z(MLA_NOPE + half), sin, z(LANES - MLA_QK)], axis=1)
    s2q = scale * jnp.concatenate([z(MLA_NOPE), -sin, z(half + LANES - MLA_QK)], axis=1)
    ck = jnp.concatenate([cos, cos, z(LANES - MLA_ROPE)], axis=1)
    s1k = jnp.concatenate([z(half), sin, z(LANES - MLA_ROPE)], axis=1)
    s2k = jnp.concatenate([-sin, z(LANES - half)], axis=1)
    return (cq, s1q, s2q, ck, s1k, s2k)


def _block_diag(x):
    g, a, b = x.shape
    eye = jnp.eye(g, dtype=x.dtype)
    return (x[:, :, None, :] * eye[:, None, :, None]).reshape(g * a, g * b)


def kernel(x_prompt, x_sample, cache_mla_latent, cache_mla_krope, state_s5_re, state_s5_im, state_hgrn, cache_mem_k, cache_mem_v, page_table, mem_prompt, norm_mix, norm_mem, norm_memsrc, norm_mlp, w_mem_q, w_mem_k, w_mem_v, w_mem_o, mem_q_gain, mem_k_gain, w_mlp_up, w_mlp_down, w_in_even, mla_cq_norm, mla_ckv_norm, w_mla_uq, w_mla_ukv, mla_qn_nope, mla_qn_rope, mla_kn_nope, mla_kn_rope, s5_lambda_re, s5_lambda_im, s5_log_step, s5_b_re, s5_b_im, s5_c_re, s5_c_im, s5_d, s5_w_glu, s5_b_glu, w_out_even, w_in_odd, hgrn_lower_bounds, hgrn_out_norm, w_out_odd):
    bsz, seq, _ = x_prompt.shape
    dbs, dseq, _ = x_sample.shape
    depth = norm_mix.shape[0]
    past_len = page_table.shape[1] * PAGE_SIZE
    ns = SAMPLE_PAD
    mem_len = mem_prompt.shape[1]
    row2 = lambda a: a.reshape(1, -1).astype(F32)

    hp = x_prompt.reshape(bsz * seq, D_MODEL)
    hs = jnp.pad(x_sample, ((0, 0), (0, ns - dseq), (0, 0))).reshape(dbs * ns, D_MODEL)

    tabs_p = _rope_tables(jnp.arange(seq, dtype=jnp.int32))
    pos_s = past_len + jnp.arange(ns, dtype=jnp.int32)
    tabs_s = tuple(jnp.tile(t, (dbs, 1)) for t in _rope_tables(pos_s))

    outs_p = {k: [] for k in ("lat", "kr", "s5r", "s5i", "hg", "mk", "mv")}
    outs_s = {k: [] for k in ("lat", "kr", "s5r", "s5i", "hg")}

    tm_p = TM_EVEN_PROJ
    nl_p = seq // tm_p
    tm_r = TM_RESIDUAL
    nl_r = seq // tm_r

    for l in range(depth):
        if l % 2 == 0:
            e = l // 2
            w_in = w_in_even[e]
            o1 = MLA_Q_LORA + MLA_KV_LORA
            wp = jnp.concatenate([w_in[:, :o1], _pad_last(w_in[:, o1:o1 + MLA_ROPE], LANES),
                                  w_in[:, o1 + MLA_ROPE:]], axis=1).astype(BF16)
            wuq = _head_pad(w_mla_uq[e], MLA_QK).astype(BF16)
            ukv = w_mla_ukv[e].reshape(MLA_KV_LORA, MLA_HEADS, MLA_NOPE + MLA_V)
            wuk_c = ukv[:, :, :MLA_NOPE].reshape(MLA_KV_LORA, -1)
            wuv_c = ukv[:, :, MLA_NOPE:].reshape(MLA_KV_LORA, -1)
            wkv = jnp.concatenate([_head_pad(wuk_c, MLA_NOPE), _head_pad(wuv_c, MLA_V)], axis=1).astype(BF16)
            qg = _pad_last(jnp.concatenate([mla_qn_nope[e], mla_qn_rope[e], mla_qn_rope[e]])[None, :], LANES)
            kg = _pad_last(jnp.concatenate([mla_kn_nope[e], mla_kn_rope[e], mla_kn_rope[e]])[None, :], LANES)
            cqn = row2(mla_cq_norm[e])
            ckvn = row2(mla_ckv_norm[e])
            g_mix = row2(norm_mix[l])
            score_bound = MLA_QK ** 0.5 * LOG2E * jnp.max(jnp.abs(qg)) * jnp.max(jnp.abs(kg))
            bounded = (score_bound <= MAX_SCORE_BOUND).astype(jnp.int32).reshape(1)
            lane = jnp.arange(LANES)
            aug = (jnp.where(lane == MLA_QK, 1.0, 0.0).astype(F32)[None, :],
                   jnp.where(lane == MLA_QK, -score_bound, 0.0).astype(F32)[None, :],
                   jnp.tile(jnp.where(lane == MLA_V, 1.0, 0.0).astype(F32), MLA_HEADS)[None, :])

            brm = _block_diag(jnp.swapaxes(s5_b_re[e], 1, 2)).astype(BF16)
            bim = _block_diag(jnp.swapaxes(s5_b_im[e], 1, 2)).astype(BF16)
            crm = _block_diag(jnp.swapaxes(s5_c_re[e], 1, 2)).astype(BF16)
            cim = _block_diag(jnp.swapaxes(s5_c_im[e], 1, 2)).astype(BF16)
            lamr = row2(s5_lambda_re[e])
            lami = row2(s5_lambda_im[e])
            lstep = row2(jnp.repeat(s5_log_step[e], S5_STATE))
            s5_consts = (lamr, lami, lstep, brm, bim, crm, cim, row2(s5_d[e]), s5_w_glu[e].astype(BF16),
                         row2(s5_b_glu[e]))
            w_out = w_out_even[e]
            wo_att_c = w_out[:MLA_HEADS * MLA_V].astype(BF16)
            wo_att_p = _pad_last(w_out[:MLA_HEADS * MLA_V].reshape(MLA_HEADS, MLA_V, D_MODEL).swapaxes(1, 2),
                                 LANES).swapaxes(1, 2).reshape(HP, D_MODEL).astype(BF16)
            wo_s5 = w_out[MLA_HEADS * MLA_V:].astype(BF16)

            q, k, v, ckv, kr, u = _even_proj(
                hp, g_mix, wp, cqn, wuq, qg, ckvn, wkv, kg, aug, tabs_p, tm=tm_p,
                u_shape=(seq, bsz * S5_WIDTH), emit_qk=False,
                u_spec=pl.BlockSpec((tm_p, S5_WIDTH), lambda i: (i % nl_p, i // nl_p)))
            o_att = _flash_attention(bounded, q.reshape(bsz, seq, HP), k.reshape(bsz, seq, HP), v,
                                     tq=TQ_ATTN, hg=ATTN_HEAD_GROUP)
            z0 = jnp.zeros((bsz, S5_NSTATE), F32)
            o_s5, hr, hi = _s5(u, z0, z0, *s5_consts, tt=S5_STEPS, nb=bsz, strip=S5_STRIP, interleave=True)
            hp = _mm_res([o_att.reshape(bsz * seq, HP), o_s5], [wo_att_p, wo_s5], hp,
                         tm=tm_r, op_specs=[row_spec(tm_r, HP),
                                            pl.BlockSpec((tm_r, S5_WIDTH), lambda i: (i % nl_r, i // nl_r))])
            outs_p["lat"].append(ckv.reshape(bsz, seq, MLA_KV_LORA))
            outs_p["kr"].append(kr.reshape(bsz, seq, MLA_ROPE))
            outs_p["s5r"].append(hr.reshape(bsz, S5_GROUPS, S5_STATE))
            outs_p["s5i"].append(hi.reshape(bsz, S5_GROUPS, S5_STATE))

            m_s = dbs * ns
            q, k, v, ckv, kr, u, qk = _even_proj(
                hs, g_mix, wp, cqn, wuq, qg, ckvn, wkv, kg, aug, tabs_s, tm=tm_p,
                u_shape=(m_s, S5_WIDTH), u_spec=row_spec(tm_p, S5_WIDTH), emit_qk=True)
            del q, k, v
            ckv3 = ckv.reshape(dbs, ns, MLA_KV_LORA)
            kr3 = kr.reshape(dbs, ns, MLA_ROPE)
            qk4 = qk.reshape(dbs, ns, MLA_HEADS, LANES)
            eye_h = jnp.eye(MLA_HEADS, dtype=BF16)
            ncols = ns * MLA_HEADS
            sub = LANES // MLA_HEADS
            nblk = MLA_NOPE // sub
            qn = (jnp.transpose(qk4[..., :MLA_NOPE], (0, 2, 3, 1))[..., None]
                  * eye_h[None, :, None, None, :])
            qn = qn.reshape(dbs, MLA_HEADS, nblk, sub, ncols).swapaxes(1, 2).reshape(dbs, MLA_HEADS * MLA_NOPE, ncols)
            qn = _pad_last(qn, LANES)
            wuk_p = (wuk_c.reshape(MLA_KV_LORA, MLA_HEADS, nblk, sub).swapaxes(1, 2)
                     .reshape(MLA_KV_LORA, MLA_HEADS * MLA_NOPE).astype(BF16))
            qr = jnp.transpose(qk4[..., MLA_NOPE:MLA_QK], (0, 3, 1, 2)).reshape(dbs, MLA_ROPE, ncols)
            qr = _pad_last(qr, LANES)
            colmask = (jnp.arange(LANES) < ncols)
            e16 = ((jnp.arange(LANES)[:, None] // sub == (jnp.arange(LANES)[None, :] % MLA_HEADS))
                   & colmask[None, :]).astype(BF16)
            onr = jnp.broadcast_to(colmask[None, :], (MLA_ROPE, LANES)).astype(BF16)
            zb = lambda r: jnp.zeros((dbs, r, LANES), BF16)
            bc = lambda x: jnp.broadcast_to(x[None], (dbs,) + x.shape)
            rhs2 = jnp.concatenate([
                jnp.concatenate([bc(e16), zb(LANES)], axis=2),
                jnp.concatenate([zb(MLA_ROPE), qr], axis=2),
                jnp.concatenate([bc(onr), zb(MLA_ROPE)], axis=2),
                jnp.zeros((dbs, LANES - 2 * MLA_ROPE, 2 * LANES), BF16)], axis=1)
            nnew = NEW_KEYS_PAD
            cnew = jnp.pad(ckv3, ((0, 0), (0, nnew - ns), (0, 0)))
            krnew = jnp.pad(kr3, ((0, 0), (0, nnew - ns), (0, 0)))
            o_att_s = _paged_attention(page_table, bounded, cache_mla_latent, jnp.swapaxes(cache_mla_krope, 2, 3), e,
                                       qn, rhs2, wuk_p, cnew, krnew, wuv_c.astype(BF16),
                                       jnp.full((1, LANES), score_bound, F32),
                                       npg=PAGES_PER_STEP, ngrp=PAGE_GROUPS, nq=ns)
            u_tb = jnp.transpose(u.reshape(dbs, ns, S5_WIDTH)[:, :dseq], (1, 0, 2)).reshape(dseq * dbs, S5_WIDTH)
            o_s5, hr, hi = _s5(u_tb, state_s5_re[e].reshape(dbs, S5_NSTATE), state_s5_im[e].reshape(dbs, S5_NSTATE),
                               *s5_consts, tt=dseq, nb=dbs, strip=S5_STRIP, interleave=False)
            o_s5 = jnp.transpose(o_s5.reshape(dseq, dbs, S5_WIDTH), (1, 0, 2))
            o_s5 = jnp.pad(o_s5, ((0, 0), (0, ns - dseq), (0, 0))).reshape(m_s, S5_WIDTH)
            hs = _mm_res([o_att_s.reshape(m_s, MLA_HEADS * MLA_V), o_s5], [wo_att_c, wo_s5], hs, tm=m_s,
                         op_specs=[row_spec(m_s, MLA_HEADS * MLA_V), row_spec(m_s, S5_WIDTH)])
            outs_s["lat"].append(ckv3[:, :dseq])
            outs_s["kr"].append(kr3[:, :dseq])
            outs_s["s5r"].append(hr.reshape(dbs, S5_GROUPS, S5_STATE))
            outs_s["s5i"].append(hi.reshape(dbs, S5_GROUPS, S5_STATE))
        else:
            o = l // 2
            g_mix = row2(norm_mix[l])
            w_in = w_in_odd[o].astype(BF16)
            w_out = w_out_odd[o].astype(BF16)
            on = row2(hgrn_out_norm[o])
            lbp = hgrn_lower_bounds.astype(F32)

            proj = _norm_mm(hp, g_mix, w_in, tm=TM_NORM_MM, tn=TN_NORM_MM)
            s_zero = jnp.zeros((bsz, HGRN_HEADS, HGRN_DK, HGRN_DK), F32)
            og, st = _hgrn(proj.reshape(bsz, seq, 4 * D_MODEL), lbp, on, s_zero, chunk=HGRN_CHUNK, nchunk=HGRN_NCHUNK, layer=l,
                           l_valid=None)
            hp = _mm_res([og.reshape(bsz * seq, D_MODEL)], [w_out], hp, tm=tm_r, op_specs=[row_spec(tm_r, D_MODEL)])
            outs_p["hg"].append(st)

            m_s = dbs * ns
            proj = _norm_mm(hs, g_mix, w_in, tm=m_s, tn=TN_NORM_MM)
            lpad = HGRN_SUB
            proj = jnp.pad(proj.reshape(dbs, ns, 4 * D_MODEL), ((0, 0), (0, lpad - ns), (0, 0)))
            og, st = _hgrn(proj, lbp, on, state_hgrn[o], chunk=lpad, nchunk=1, layer=l, l_valid=dseq)
            hs = _mm_res([og[:, :ns].reshape(m_s, D_MODEL)], [w_out], hs, tm=m_s, op_specs=[row_spec(m_s, D_MODEL)])
            outs_s["hg"].append(st)

        g_mem = row2(norm_mem[l])
        wq = w_mem_q[l].astype(BF16)
        wo = w_mem_o[l].astype(BF16)
        mqg = row2(mem_q_gain[l])
        wkv_m = jnp.concatenate([w_mem_k[l], w_mem_v[l]], axis=1).astype(BF16)
        mk, mv = _mem_kv(mem_prompt.reshape(bsz * mem_len, D_MODEL), row2(norm_memsrc[l]), wkv_m,
                         row2(mem_k_gain[l]), tm=TM_MEM_KV)
        mk = mk.reshape(bsz, mem_len, MEM_WIDTH)
        mv = mv.reshape(bsz, mem_len, MEM_WIDTH)
        outs_p["mk"].append(mk.reshape(bsz, mem_len, MEM_HEADS, MEM_HEAD_DIM))
        outs_p["mv"].append(mv.reshape(bsz, mem_len, MEM_HEADS, MEM_HEAD_DIM))
        hp = _mem_attn(hp.reshape(bsz, seq, D_MODEL), g_mem, wq, mqg, mk, mv, wo,
                       nb=1, tl=TL_MEM).reshape(bsz * seq, D_MODEL)
        pair_shape = (depth, dbs, mem_len // 2, 2 * MEM_HEADS, MEM_HEAD_DIM)
        hs = _mem_attn(hs.reshape(dbs, ns, D_MODEL), g_mem, wq, mqg, cache_mem_k.reshape(pair_shape),
                       cache_mem_v.reshape(pair_shape), wo, nb=SAMPLE_MEM_SEQS, tl=ns, layer=l).reshape(dbs * ns, D_MODEL)

        g_mlp = row2(norm_mlp[l])
        wu = w_mlp_up[l].astype(BF16)
        wd = w_mlp_down[l].astype(BF16)
        hp = _mlp(hp, g_mlp, wu, wd, tm=TM_MLP, tf=TF_MLP)
        hs = _mlp(hs, g_mlp, wu, wd, tm=dbs * ns, tf=TF_MLP)

    y_p = hp.reshape(bsz, seq, D_MODEL)
    y_s = hs.reshape(dbs, ns, D_MODEL)[:, :dseq]
    return (y_p, y_s,
            jnp.stack(outs_p["lat"], axis=1), jnp.stack(outs_p["kr"], axis=1),
            jnp.stack(outs_p["s5r"]), jnp.stack(outs_p["s5i"]), jnp.stack(outs_p["hg"]),
            jnp.stack(outs_p["mk"]), jnp.stack(outs_p["mv"]),
            jnp.stack(outs_s["lat"], axis=1), jnp.stack(outs_s["kr"], axis=1),
            jnp.stack(outs_s["s5r"]), jnp.stack(outs_s["s5i"]), jnp.stack(outs_s["hg"]))
```

```python
import functools
import math

import jax
import jax.numpy as jnp
from jax import lax
from jax.experimental import pallas as pl
from jax.experimental.pallas import tpu as pltpu

F32 = jnp.float32
BF16 = jnp.bfloat16

LANES = 128
VMEM_LIMIT_BYTES = 56 * 1024 * 1024

D_MODEL = 1024
MLA_HEADS = 8
MLA_NOPE = 64
MLA_ROPE = 32
MLA_QK = MLA_NOPE + MLA_ROPE
MLA_V = 64
MLA_Q_LORA = 768
MLA_KV_LORA = 256
ROPE_THETA = 10000.0
PAGE_SIZE = 128
S5_WIDTH = 512
S5_GROUP = 16
S5_GROUPS = S5_WIDTH // S5_GROUP
S5_STATE = 64
S5_NSTATE = S5_GROUPS * S5_STATE
HGRN_HEADS = 8
HGRN_DK = 128
HGRN_SUB = 32
HGRN_EXP_CLAMP = 80.0
MAX_SCORE_BOUND = 40.0
MEM_HEADS = 4
MEM_HEAD_DIM = 128
MEM_WIDTH = MEM_HEADS * MEM_HEAD_DIM
D_FF = 4 * D_MODEL
EPS = 1e-6
LOG2E = math.log2(math.e)
HP = MLA_HEADS * LANES

TM_EVEN_PROJ = 512
TQ_ATTN = 512
ATTN_HEAD_GROUP = 4
PAGES_PER_STEP = 64
PAGE_GROUPS = 8
S5_STEPS = 64
S5_STRIP = 512
TM_RESIDUAL = 1024
TM_NORM_MM, TN_NORM_MM = 1024, 2048
TM_MLP, TF_MLP = 1024, 2048
HGRN_CHUNK, HGRN_NCHUNK = 64, 4
TL_MEM = 1024
TM_MEM_KV = 512
SAMPLE_MEM_SEQS = 8
SAMPLE_PAD = 8
NEW_KEYS_PAD = 16


def _cparams(sem):
    return pltpu.CompilerParams(dimension_semantics=sem, vmem_limit_bytes=VMEM_LIMIT_BYTES)


def _rms(x, g):
    return x * lax.rsqrt(jnp.mean(x * x, axis=-1, keepdims=True) + EPS) * g


def _sigmoid(x):
    return 1.0 / (1.0 + jnp.exp(-x))


def _dot(a, b):
    return jnp.dot(a, b, preferred_element_type=F32)


def _dot_nt(a, b):
    return lax.dot_general(a, b, (((1,), (1,)), ((), ())), preferred_element_type=F32)


def _dot_tn(a, b):
    return lax.dot_general(a, b, (((0,), (0,)), ((), ())), preferred_element_type=F32)


def _row_to_col(row, n):
    r = lax.broadcasted_iota(jnp.int32, (n, n), 0)
    c = lax.broadcasted_iota(jnp.int32, (n, n), 1)
    return jnp.sum(jnp.where(r == c, jnp.broadcast_to(row, (n, n)), 0.0), axis=1, keepdims=True)


def _div_pow2(x, d):
    return lax.shift_right_logical(x, int(math.log2(d)))


def _full_spec(shape):
    nd = len(shape)
    return pl.BlockSpec(shape, lambda *_: (0,) * nd)


def row_spec(tm, width):
    return pl.BlockSpec((tm, width), lambda i: (i, 0))


def _even_proj_kernel(h_ref, g_ref, wp_ref, cqn_ref, wuq_ref, qg_ref, ckvn_ref, wkv_ref, kg_ref,
                      qaug_ref, kaug_ref, vaug_ref, cq_ref, s1q_ref, s2q_ref, ck_ref, s1k_ref, s2k_ref,
                      q_out, k_out, v_out, ckv_out, kr_out, u_out, *maybe_qk_out, nrg):
    o1 = MLA_Q_LORA
    o2 = o1 + MLA_KV_LORA
    o3 = o2 + LANES
    half = MLA_ROPE // 2
    inv_qk = 1.0 / MLA_QK
    qg = qg_ref[...]
    kg = kg_ref[...]
    tm = h_ref.shape[0]
    groups = [slice(tm // nrg * t, tm // nrg * (t + 1)) for t in range(nrg)]
    proj = [_dot(_rms(h_ref[r, :], g_ref[...]).astype(BF16), wp_ref[...]) for r in groups]
    qf, kv = [], []
    for r, p in zip(groups, proj):
        u_out[r, :] = p[:, o3:]
        qf.append(_dot(_rms(p[:, :o1], cqn_ref[...]).astype(BF16), wuq_ref[...]))
        ckv = _rms(p[:, o1:o2], ckvn_ref[...])
        ckv_out[r, :] = ckv
        kv.append(_dot(ckv.astype(BF16), wkv_ref[...]))
    for r, p, qfr, kvr in zip(groups, proj, qf, kv):
        kr = p[:, o2:o3]
        krr = (kr * ck_ref[r, :] + pltpu.roll(kr, half, 1) * s1k_ref[r, :]
               + pltpu.roll(kr, LANES - half, 1) * s2k_ref[r, :])
        kr_out[r, :] = krr[:, :MLA_ROPE]
        kr_sh = pltpu.roll(krr, MLA_NOPE, 1)
        cq_t, s1q_t, s2q_t = cq_ref[r, :], s1q_ref[r, :], s2q_ref[r, :]
        for h in range(MLA_HEADS):
            sl = slice(LANES * h, LANES * (h + 1))
            qh = qfr[:, sl]
            qh = qh * lax.rsqrt(jnp.sum(qh * qh, axis=-1, keepdims=True) * inv_qk + EPS) * qg
            qh = (qh * cq_t + pltpu.roll(qh, half, 1) * s1q_t + pltpu.roll(qh, LANES - half, 1) * s2q_t)
            q_out[r, sl] = (qh + qaug_ref[...]).astype(BF16)
            if maybe_qk_out:
                maybe_qk_out[0][r, sl] = (qh * kg).astype(BF16)
            kh = kvr[:, sl] + kr_sh
            kh = kh * lax.rsqrt(jnp.sum(kh * kh, axis=-1, keepdims=True) * inv_qk + EPS) * kg
            k_out[r, sl] = (kh + kaug_ref[...]).astype(BF16)
        v_out[:, r] = (kvr[:, HP:] + vaug_ref[...]).T.astype(BF16)


def _even_proj(h, g, wp, cqn, wuq, qg, ckvn, wkv, kg, aug, tabs, *, tm, u_shape, u_spec, emit_qk):
    m = h.shape[0]
    ltab = tabs[0].shape[0]
    ntab = ltab // tm
    row = lambda i: (i, 0)
    tab_spec = pl.BlockSpec((tm, LANES), lambda i: (i % ntab, 0))
    in_specs = ([pl.BlockSpec((tm, D_MODEL), row), _full_spec(g.shape), _full_spec(wp.shape),
                 _full_spec(cqn.shape), _full_spec(wuq.shape), _full_spec(qg.shape),
                 _full_spec(ckvn.shape), _full_spec(wkv.shape), _full_spec(kg.shape)]
                + [_full_spec(a.shape) for a in aug] + [tab_spec] * 6)
    out_shape = [jax.ShapeDtypeStruct((m, HP), BF16), jax.ShapeDtypeStruct((m, HP), BF16),
                 jax.ShapeDtypeStruct((m // tm, HP, tm), BF16), jax.ShapeDtypeStruct((m, MLA_KV_LORA), F32),
                 jax.ShapeDtypeStruct((m, MLA_ROPE), F32), jax.ShapeDtypeStruct(u_shape, F32)]
    out_specs = [pl.BlockSpec((tm, HP), row), pl.BlockSpec((tm, HP), row),
                 pl.BlockSpec((None, HP, tm), lambda i: (i, 0, 0)),
                 pl.BlockSpec((tm, MLA_KV_LORA), row), pl.BlockSpec((tm, MLA_ROPE), row),
                 u_spec]
    if emit_qk:
        out_shape.append(jax.ShapeDtypeStruct((m, HP), BF16))
        out_specs.append(pl.BlockSpec((tm, HP), row))
    return pl.pallas_call(
        functools.partial(_even_proj_kernel, nrg=2), grid=(m // tm,), in_specs=in_specs, out_specs=out_specs,
        out_shape=out_shape,
        compiler_params=_cparams(("arbitrary",)), name="even_proj",
    )(h, g, wp, cqn, wuq, qg, ckvn, wkv, kg, *aug, *tabs)


def _flash_kernel(bounded_ref, q_ref, k_ref, vt_ref, o_ref, *, tq, tk, hg):
    i = pl.program_id(1)
    nfull = (i * tq) // tk
    key = lax.broadcasted_iota(jnp.int32, (tk, tq), 0) + nfull * tk
    qry = lax.broadcasted_iota(jnp.int32, (tk, tq), 1) + i * tq
    causal = qry >= key

    heads = [slice(LANES * h, LANES * (h + 1)) for h in range(MLA_HEADS)]

    def accumulate(j, acc, masked):
        off = pl.multiple_of(j * tk, tk)
        out = []
        for h0 in range(0, MLA_HEADS, hg):
            grp = range(h0, h0 + hg)
            scores = [_dot_nt(k_ref[pl.ds(off, tk), heads[h]], q_ref[:, heads[h]]) for h in grp]
            if masked:
                scores = [jnp.where(causal, s, -jnp.inf) for s in scores]
            probs = [jnp.exp2(s).astype(BF16) for s in scores]
            out += [acc[h] + _dot(vt_ref[j, heads[h], :], p) for p, h in zip(probs, grp)]
        return tuple(out)

    @pl.when(bounded_ref[0] != 0)
    def _():
        acc = lax.fori_loop(0, nfull, lambda j, c: accumulate(j, c, False),
                            (jnp.zeros((LANES, tq), F32),) * MLA_HEADS)
        acc = accumulate(nfull, acc, True)
        for h, sl in enumerate(heads):
            o_ref[:, sl] = (acc[h] / acc[h][MLA_V:MLA_V + 1, :]).T.astype(BF16)

    def update(j, carry, masked):
        off = pl.multiple_of(j * tk, tk)
        out = []
        for h0 in range(0, MLA_HEADS, hg):
            grp = range(h0, h0 + hg)
            scores = [_dot_nt(k_ref[pl.ds(off, tk), heads[h]], q_ref[:, heads[h]]) for h in grp]
            probs, stats = [], []
            for h, s in zip(grp, scores):
                m, l = carry[3 * h], carry[3 * h + 1]
                if masked:
                    s = jnp.where(causal, s, -jnp.inf)
                m_new = jnp.maximum(m, jnp.max(s, axis=0, keepdims=True))
                alpha = jnp.exp2(m - m_new)
                p = jnp.exp2(s - m_new)
                stats.append((m_new, alpha * l + jnp.sum(p, axis=0, keepdims=True), alpha))
                probs.append(p.astype(BF16))
            pv = [_dot(vt_ref[j, heads[h], :], p) for p, h in zip(probs, grp)]
            for h, (m_new, l_new, alpha), o in zip(grp, stats, pv):
                out += [m_new, l_new, alpha * carry[3 * h + 2] + o]
        return tuple(out)

    @pl.when(bounded_ref[0] == 0)
    def _():
        init = (jnp.full((1, tq), -jnp.inf, F32), jnp.zeros((1, tq), F32), jnp.zeros((LANES, tq), F32))
        carry = lax.fori_loop(0, nfull, lambda j, c: update(j, c, False), init * MLA_HEADS)
        carry = update(nfull, carry, True)
        for h, sl in enumerate(heads):
            o_ref[:, sl] = (carry[3 * h + 2] / carry[3 * h + 1]).T.astype(BF16)


def _flash_attention(bounded, q, k, vt, *, tq, hg=2):
    b, l, _ = q.shape
    tk = vt.shape[2]
    nkb = l // tk
    grid_spec = pltpu.PrefetchScalarGridSpec(
        num_scalar_prefetch=1, grid=(b, l // tq),
        in_specs=[pl.BlockSpec((None, tq, HP), lambda bi, i, f: (bi, i, 0)),
                  pl.BlockSpec((None, l, HP), lambda bi, i, f: (bi, 0, 0)),
                  pl.BlockSpec((nkb, HP, tk), lambda bi, i, f: (bi, 0, 0))],
        out_specs=pl.BlockSpec((None, tq, HP), lambda bi, i, f: (bi, i, 0)))
    return pl.pallas_call(
        functools.partial(_flash_kernel, tq=tq, tk=tk, hg=hg), grid_spec=grid_spec,
        out_shape=jax.ShapeDtypeStruct((b, l, HP), BF16),
        compiler_params=_cparams(("arbitrary", "arbitrary")), name="prompt_attention",
    )(bounded, q, k, vt)


def _paged_kernel(pt_ref, bounded_ref, *refs, npg, ngrp, nsteps, nq):
    lat_refs = refs[:npg]
    krt_refs = refs[npg:2 * npg]
    (qn_ref, rhs2_ref, wuk_ref, cnew_ref, krnew_ref, wuv_ref, bound_ref,
     o_ref, wabs, m_scr, l_scr, a_scr, l_acc, a_acc) = refs[2 * npg:]
    del pt_ref
    s = pl.program_id(1)
    nslots = nsteps * ngrp + 1
    ncol = LANES
    inv_qk = 1.0 / MLA_QK
    bounded = bounded_ref[0] != 0
    last = s == nsteps - 1

    @pl.when(s == 0)
    def _():
        wabs[...] = _dot(wuk_ref[...], qn_ref[...]).astype(BF16)
        l_acc[...] = jnp.zeros(l_acc.shape, F32)
        a_acc[...] = jnp.zeros(a_acc.shape, F32)

    def scores(blocks, mask):
        kn = [_dot(c, wuk_ref[...]) for c, _ in blocks]
        sq = [k * k for k in kn]
        psum = [q[:, 0:LANES] + q[:, LANES:2 * LANES] + q[:, 2 * LANES:3 * LANES] + q[:, 3 * LANES:] for q in sq]
        r2 = [_dot(jnp.concatenate([p.astype(BF16), x], axis=1), rhs2_ref[...]) for p, (_, x) in zip(psum, blocks)]
        scn = [_dot(c, wabs[...]) for c, _ in blocks]
        out = []
        for t in range(len(blocks)):
            sc = (scn[t] + r2[t][:, LANES:]) * lax.rsqrt(r2[t][:, :LANES] * inv_qk + EPS)
            out.append(sc if mask is None else jnp.where(mask, sc, -jnp.inf))
        return out

    def stats(blocks, scs):
        probs, out = [], []
        for sc in scs:
            m = jnp.max(sc, axis=0, keepdims=True)
            p = jnp.exp2(sc - m)
            out.append((m, jnp.sum(p, axis=0, keepdims=True)))
            probs.append(p.astype(BF16))
        acc = [_dot_tn(p, c) for p, (c, _) in zip(probs, blocks)]
        return [(m, l, a) for (m, l), a in zip(out, acc)]

    def accumulate(blocks, scs):
        probs = [jnp.exp2(sc - bound_ref[...]) for sc in scs]
        l_new = l_acc[...]
        for p in probs:
            l_new = l_new + jnp.sum(p, axis=0, keepdims=True)
        l_acc[...] = l_new
        a_new = a_acc[...]
        for p, (c, _) in zip(probs, blocks):
            a_new = a_new + _dot_tn(p.astype(BF16), c)
        a_acc[...] = a_new

    def finish(num):
        full = _dot(num.astype(BF16), wuv_ref[...])
        hrow = lax.broadcasted_iota(jnp.int32, (MLA_HEADS, MLA_HEADS * MLA_V), 0)
        hcol = _div_pow2(lax.broadcasted_iota(jnp.int32, (MLA_HEADS, MLA_HEADS * MLA_V), 1), MLA_V)
        rows = []
        for qi in range(nq):
            blk = full[MLA_HEADS * qi:MLA_HEADS * (qi + 1), :]
            rows.append(jnp.sum(jnp.where(hrow == hcol, blk, 0.0), axis=0, keepdims=True))
        o_ref[...] = jnp.concatenate(rows, axis=0)

    zpad = jnp.zeros((LANES - 2 * MLA_ROPE, PAGE_SIZE), F32)

    def rope_block(g):
        krt = krt_refs[g][...]
        return jnp.concatenate([krt, krt * krt, zpad], axis=0).T.astype(BF16)

    pg = npg // ngrp
    groups = [(jnp.concatenate([lat_refs[g][...].astype(BF16) for g in range(pg * t, pg * (t + 1))], axis=0),
               jnp.concatenate([rope_block(g) for g in range(pg * t, pg * (t + 1))], axis=0))
              for t in range(ngrp)]
    scs = scores(groups, None)

    @pl.when(bounded)
    def _():
        accumulate(groups, scs)

    @pl.when(jnp.logical_not(bounded))
    def _():
        for t, (m, l, a) in enumerate(stats(groups, scs)):
            slot = s * ngrp + t
            m_scr[pl.ds(slot, 1), :] = m
            l_scr[pl.ds(slot, 1), :] = l
            a_scr[slot] = a

    def new_block():
        nnew = cnew_ref.shape[0]
        krn = jnp.concatenate([krnew_ref[...], jnp.zeros((nnew, LANES - MLA_ROPE), F32)], axis=1)
        krn = krn + pltpu.roll(krn * krn, MLA_ROPE, 1)
        key = lax.broadcasted_iota(jnp.int32, (nnew, ncol), 0)
        qry = _div_pow2(lax.broadcasted_iota(jnp.int32, (nnew, ncol), 1), MLA_HEADS)
        blocks = [(cnew_ref[...].astype(BF16), krn.astype(BF16))]
        return blocks, scores(blocks, key <= qry)

    @pl.when(jnp.logical_and(last, bounded))
    def _():
        accumulate(*new_block())
        finish(a_acc[...] * _row_to_col(1.0 / l_acc[...], ncol))

    @pl.when(jnp.logical_and(last, jnp.logical_not(bounded)))
    def _():
        blocks, scs_new = new_block()
        (m2, l2, a2), = stats(blocks, scs_new)
        m_scr[nslots - 1:nslots, :] = m2
        l_scr[nslots - 1:nslots, :] = l2
        a_scr[nslots - 1] = a2
        mall = m_scr[0:nslots, :]
        w = jnp.exp2(mall - jnp.max(mall, axis=0, keepdims=True))
        den = jnp.sum(l_scr[0:nslots, :] * w, axis=0, keepdims=True)
        wn = w / den
        num = jnp.zeros((ncol, MLA_KV_LORA), F32)
        for t in range(nslots):
            num = num + a_scr[t] * _row_to_col(wn[t:t + 1, :], ncol)
        finish(num)


def _paged_attention(page_table, bounded, cache_lat, cache_krt, e, qn, rhs2, wuk, cnew, krnew, wuv, bound, *,
                     npg, ngrp, nq):
    nb, npages = page_table.shape
    nsteps = npages // npg
    nnew = cnew.shape[1]
    nslots = nsteps * ngrp + 1

    def page_spec(shape, g):
        return pl.BlockSpec((None, None) + shape, lambda b, s, pt, f: (pt[b, s * npg + g], e, 0, 0))

    per_b3 = lambda b, s, pt, f: (b, 0, 0)
    const2 = lambda b, s, pt, f: (0, 0)
    in_specs = ([page_spec((PAGE_SIZE, MLA_KV_LORA), g) for g in range(npg)]
                + [page_spec((MLA_ROPE, PAGE_SIZE), g) for g in range(npg)]
                + [pl.BlockSpec((None,) + qn.shape[1:], per_b3), pl.BlockSpec((None,) + rhs2.shape[1:], per_b3),
                   pl.BlockSpec(wuk.shape, const2),
                   pl.BlockSpec((None, nnew, MLA_KV_LORA), per_b3), pl.BlockSpec((None, nnew, MLA_ROPE), per_b3),
                   pl.BlockSpec(wuv.shape, const2), pl.BlockSpec(bound.shape, const2)])
    grid_spec = pltpu.PrefetchScalarGridSpec(
        num_scalar_prefetch=2, grid=(nb, nsteps), in_specs=in_specs,
        out_specs=pl.BlockSpec((None, nq, MLA_HEADS * MLA_V), per_b3),
        scratch_shapes=[pltpu.VMEM((MLA_KV_LORA, LANES), BF16),
                        pltpu.VMEM((nslots, LANES), F32), pltpu.VMEM((nslots, LANES), F32),
                        pltpu.VMEM((nslots, LANES, MLA_KV_LORA), F32),
                        pltpu.VMEM((1, LANES), F32), pltpu.VMEM((LANES, MLA_KV_LORA), F32)])
    return pl.pallas_call(
        functools.partial(_paged_kernel, npg=npg, ngrp=ngrp, nsteps=nsteps, nq=nq),
        grid_spec=grid_spec, out_shape=jax.ShapeDtypeStruct((nb, nq, MLA_HEADS * MLA_V), F32),
        compiler_params=_cparams(("arbitrary", "arbitrary")), name="paged_attention",
    )(page_table, bounded, *([cache_lat] * npg), *([cache_krt] * npg), qn, rhs2, wuk, cnew, krnew, wuv, bound)


def _s5_kernel(u_ref, h0r_ref, h0i_ref, lamr_ref, lami_ref, lstep_ref, brm_ref, bim_ref, crm_ref, cim_ref,
               d_ref, wg_ref, bg_ref, o_ref, hr_out, hi_out, xr_scr, xi_scr, hcr, hci, disc, io_scr, *,
               tt, nb, strip, interleave):
    c = pl.program_id(0)

    @pl.when(c == 0)
    def _():
        lr = jnp.minimum(lamr_ref[...], -1e-4)
        li = lami_ref[...]
        dt = jnp.exp(lstep_ref[...])
        mag = jnp.exp(lr * dt)
        abr = mag * jnp.cos(li * dt)
        abi = mag * jnp.sin(li * dt)
        den = lr * lr + li * li
        disc[0:1, :] = abr
        disc[1:2, :] = abi
        disc[2:3, :] = ((abr - 1.0) * lr + abi * li) / den
        disc[3:4, :] = (abi * lr - (abr - 1.0) * li) / den
        hcr[...] = h0r_ref[...]
        hci[...] = h0i_ref[...]

    nlb = S5_WIDTH // LANES
    if interleave:
        for b in range(nb):
            for j in range(nlb):
                c0 = S5_WIDTH * b + LANES * j
                io_scr[j, pl.ds(b, tt, stride=nb), :] = u_ref[:, c0:c0 + LANES]
        u = jnp.concatenate([io_scr[j] for j in range(nlb)], axis=1)
    else:
        u = u_ref[...]
    ub = u.astype(BF16)
    kc = 2 * LANES
    ks = kc * S5_STATE // S5_GROUP
    for k in range(S5_WIDTH // kc):
        cols = slice(kc * k, kc * (k + 1))
        sts = slice(ks * k, ks * (k + 1))
        pr = _dot(ub[:, cols], brm_ref[cols, sts])
        pi = _dot(ub[:, cols], bim_ref[cols, sts])
        cor = disc[2:3, sts]
        coi = disc[3:4, sts]
        xr_scr[:, sts] = cor * pr - coi * pi
        xi_scr[:, sts] = cor * pi + coi * pr

    for s0 in range(0, S5_NSTATE, strip):
        lanes = slice(s0, s0 + strip)
        ar = jnp.broadcast_to(disc[0:1, lanes], (nb, strip))
        ai = jnp.broadcast_to(disc[1:2, lanes], (nb, strip))

        def step(t, carry, lanes=lanes, ar=ar, ai=ai):
            hr, hi = carry
            rows = pl.ds(pl.multiple_of(t * nb, nb), nb)
            nr = ar * hr - ai * hi + xr_scr[rows, lanes]
            ni = ar * hi + ai * hr + xi_scr[rows, lanes]
            xr_scr[rows, lanes] = nr
            xi_scr[rows, lanes] = ni
            return nr, ni

        hr, hi = lax.fori_loop(0, tt, step, (hcr[:, lanes], hci[:, lanes]))
        hcr[:, lanes] = hr
        hci[:, lanes] = hi

    ys = []
    for k in range(S5_WIDTH // kc):
        cols = slice(kc * k, kc * (k + 1))
        sts = slice(ks * k, ks * (k + 1))
        ys.append(_dot(xr_scr[:, sts].astype(BF16), crm_ref[sts, cols])
                  - _dot(xi_scr[:, sts].astype(BF16), cim_ref[sts, cols]))
    y = jnp.concatenate(ys, axis=1) + d_ref[...] * u
    z = jax.nn.gelu(y)
    gate = _sigmoid(_dot(z.astype(BF16), wg_ref[...]) + bg_ref[...])
    if interleave:
        out = z * gate
        for j in range(nlb):
            io_scr[j] = out[:, LANES * j:LANES * (j + 1)]
        for b in range(nb):
            for j in range(nlb):
                c0 = S5_WIDTH * b + LANES * j
                o_ref[:, c0:c0 + LANES] = io_scr[j, pl.ds(b, tt, stride=nb), :].astype(BF16)
    else:
        o_ref[...] = (z * gate).astype(BF16)

    @pl.when(c == pl.num_programs(0) - 1)
    def _():
        hr_out[...] = hcr[...]
        hi_out[...] = hci[...]


def _s5(u, h0r, h0i, lamr, lami, lstep, brm, bim, crm, cim, d, wg, bg, *, tt, nb, strip, interleave):
    steps = u.shape[0] if interleave else u.shape[0] // nb
    blk = tt * nb
    consts = (h0r, h0i, lamr, lami, lstep, brm, bim, crm, cim, d, wg, bg)
    io_spec = row_spec(tt, nb * S5_WIDTH) if interleave else row_spec(blk, S5_WIDTH)
    return pl.pallas_call(
        functools.partial(_s5_kernel, tt=tt, nb=nb, strip=strip, interleave=interleave),
        grid=(steps // tt,),
        in_specs=[io_spec] + [_full_spec(a.shape) for a in consts],
        out_specs=[io_spec, _full_spec((nb, S5_NSTATE)), _full_spec((nb, S5_NSTATE))],
        out_shape=[jax.ShapeDtypeStruct(u.shape, BF16),
                   jax.ShapeDtypeStruct((nb, S5_NSTATE), F32), jax.ShapeDtypeStruct((nb, S5_NSTATE), F32)],
        scratch_shapes=[pltpu.VMEM((blk, S5_NSTATE), F32), pltpu.VMEM((blk, S5_NSTATE), F32),
                        pltpu.VMEM((nb, S5_NSTATE), F32), pltpu.VMEM((nb, S5_NSTATE), F32),
                        pltpu.VMEM((8, S5_NSTATE), F32), pltpu.VMEM((S5_WIDTH // LANES, blk, LANES), F32)],
        compiler_params=_cparams(("arbitrary",)), name="s5",
    )(u, *consts)


def _mm_res_kernel(*refs, nop):
    res_ref = refs[2 * nop]
    o_ref = refs[2 * nop + 1]
    acc = res_ref[...]
    for t in range(nop):
        acc = acc + _dot(refs[t][...].astype(BF16), refs[nop + t][...])
    o_ref[...] = acc


def _mm_res(ops, ws, res, *, tm, op_specs):
    m, n = res.shape
    row = lambda i: (i, 0)
    in_specs = list(op_specs) + [_full_spec(w.shape) for w in ws] + [pl.BlockSpec((tm, n), row)]
    return pl.pallas_call(
        functools.partial(_mm_res_kernel, nop=len(ops)), grid=(m // tm,), in_specs=in_specs,
        out_specs=pl.BlockSpec((tm, n), row), out_shape=jax.ShapeDtypeStruct((m, n), F32),
        compiler_params=_cparams(("arbitrary",)), name="matmul_residual",
    )(*ops, *ws, res)


def _mem_kv_kernel(x_ref, g_ref, w_ref, kg_ref, k_out, v_out):
    mn = _rms(x_ref[...], g_ref[...]).astype(BF16)
    kv = _dot(mn, w_ref[...])
    kg = kg_ref[...]
    for h in range(MEM_HEADS):
        sl = slice(LANES * h, LANES * (h + 1))
        k_out[:, sl] = _rms(kv[:, sl], kg)
    v_out[...] = kv[:, MEM_WIDTH:]


def _mem_kv(x, g, w, kg, *, tm):
    m = x.shape[0]
    row = lambda i: (i, 0)
    return pl.pallas_call(
        _mem_kv_kernel, grid=(m // tm,),
        in_specs=[pl.BlockSpec((tm, D_MODEL), row), _full_spec(g.shape), _full_spec(w.shape), _full_spec(kg.shape)],
        out_specs=[pl.BlockSpec((tm, MEM_WIDTH), row), pl.BlockSpec((tm, MEM_WIDTH), row)],
        out_shape=[jax.ShapeDtypeStruct((m, MEM_WIDTH), F32), jax.ShapeDtypeStruct((m, MEM_WIDTH), F32)],
        compiler_params=_cparams(("arbitrary",)), name="mem_kv",
    )(x, g, w, kg)


def _mem_attn_kernel(h_ref, g_ref, wq_ref, qg_ref, mk_ref, mv_ref, wo_ref, o_ref, *, nb, tl, paired):
    x = h_ref[...].reshape(nb * tl, D_MODEL)
    hn = _rms(x, g_ref[...]).astype(BF16)
    q = _dot(hn, wq_ref[...])
    qg = qg_ref[...] * (MEM_HEAD_DIM ** -0.5 * LOG2E)
    qn = [_rms(q[:, LANES * h:LANES * (h + 1)], qg).astype(BF16) for h in range(MEM_HEADS)]

    def head_block(ref, b, h):
        if paired:
            return jnp.concatenate([ref[b, :, h, :], ref[b, :, MEM_HEADS + h, :]], axis=0).astype(BF16)
        return ref[b, :, LANES * h:LANES * (h + 1)].astype(BF16)

    pairs = [(b, h) for b in range(nb) for h in range(MEM_HEADS)]
    scores = [_dot_nt(qn[h][tl * b:tl * (b + 1), :], head_block(mk_ref, b, h)) for b, h in pairs]
    probs = [jnp.exp2(s - jnp.max(s, axis=-1, keepdims=True)) for s in scores]
    outs = [_dot(p.astype(BF16), head_block(mv_ref, b, h)) / jnp.sum(p, axis=-1, keepdims=True)
            for p, (b, h) in zip(probs, pairs)]
    rows = [jnp.concatenate(outs[MEM_HEADS * b:MEM_HEADS * (b + 1)], axis=1) for b in range(nb)]
    o = (rows[0] if nb == 1 else jnp.concatenate(rows, axis=0)).astype(BF16)
    o_ref[...] = (x + _dot(o, wo_ref[...])).reshape(nb, tl, D_MODEL)


def _mem_attn(h, g, wq, qg, mk, mv, wo, *, nb, tl, layer=None):
    b, l, _ = h.shape
    blk = lambda bi, i: (bi, i, 0)
    if layer is None:
        mem_spec = pl.BlockSpec((nb,) + mk.shape[1:], lambda bi, i: (bi, 0, 0))
    else:
        mem_spec = pl.BlockSpec((None, nb) + mk.shape[2:], lambda bi, i: (layer, bi, 0, 0, 0))
    return pl.pallas_call(
        functools.partial(_mem_attn_kernel, nb=nb, tl=tl, paired=layer is not None), grid=(b // nb, l // tl),
        in_specs=[pl.BlockSpec((nb, tl, D_MODEL), blk), _full_spec(g.shape), _full_spec(wq.shape),
                  _full_spec(qg.shape), mem_spec, mem_spec, _full_spec(wo.shape)],
        out_specs=pl.BlockSpec((nb, tl, D_MODEL), blk),
        out_shape=jax.ShapeDtypeStruct(h.shape, F32),
        compiler_params=_cparams(("arbitrary", "arbitrary")), name="mem_attention",
    )(h, g, wq, qg, mk, mv, wo)


def _mlp_kernel(h_ref, g_ref, wu_ref, wd_ref, o_ref, xn_scr, acc_scr):
    j = pl.program_id(1)

    @pl.when(j == 0)
    def _():
        xn_scr[...] = _rms(h_ref[...], g_ref[...]).astype(BF16)
        acc_scr[...] = jnp.zeros(acc_scr.shape, F32)

    a = _dot(xn_scr[...], wu_ref[...])
    a = jnp.square(jnp.maximum(a, 0.0)).astype(BF16)
    acc_scr[...] += _dot(a, wd_ref[...])

    @pl.when(j == pl.num_programs(1) - 1)
    def _():
        o_ref[...] = h_ref[...] + acc_scr[...]


def _mlp(h, g, wu, wd, *, tm, tf):
    m = h.shape[0]
    return pl.pallas_call(
        _mlp_kernel, grid=(m // tm, D_FF // tf),
        in_specs=[pl.BlockSpec((tm, D_MODEL), lambda i, j: (i, 0)), _full_spec(g.shape),
                  pl.BlockSpec((D_MODEL, tf), lambda i, j: (0, j)), pl.BlockSpec((tf, D_MODEL), lambda i, j: (j, 0))],
        out_specs=pl.BlockSpec((tm, D_MODEL), lambda i, j: (i, 0)),
        out_shape=jax.ShapeDtypeStruct((m, D_MODEL), F32),
        scratch_shapes=[pltpu.VMEM((tm, D_MODEL), BF16), pltpu.VMEM((tm, D_MODEL), F32)],
        compiler_params=_cparams(("arbitrary", "arbitrary")), name="mlp",
    )(h, g, wu, wd)


def _norm_mm_kernel(h_ref, g_ref, w_ref, o_ref, xn_scr):
    @pl.when(pl.program_id(1) == 0)
    def _():
        xn_scr[...] = _rms(h_ref[...], g_ref[...]).astype(BF16)

    o_ref[...] = _dot(xn_scr[...], w_ref[...])


def _norm_mm(h, g, w, *, tm, tn):
    m = h.shape[0]
    n = w.shape[1]
    return pl.pallas_call(
        _norm_mm_kernel, grid=(m // tm, n // tn),
        in_specs=[pl.BlockSpec((tm, D_MODEL), lambda i, j: (i, 0)), _full_spec(g.shape),
                  pl.BlockSpec((D_MODEL, tn), lambda i, j: (0, j))],
        out_specs=pl.BlockSpec((tm, tn), lambda i, j: (i, j)),
        out_shape=jax.ShapeDtypeStruct((m, n), F32),
        scratch_shapes=[pltpu.VMEM((tm, D_MODEL), BF16)],
        compiler_params=_cparams(("arbitrary", "arbitrary")), name="norm_matmul",
    )(h, g, w)


def _hgrn_kernel(q_ref, f_ref, i_ref, g_ref, lbp_ref, on_ref, s0_ref, o_ref, s_out, s_scr, *,
                 chunk, nchunk, layer, l_valid):
    c = pl.program_id(1)
    tb = chunk * nchunk

    @pl.when(c == 0)
    def _():
        s_scr[...] = s0_ref[...]

    lbp = lbp_ref[...]
    e = jnp.exp(lbp - jnp.max(lbp, axis=0, keepdims=True))
    sm = e / jnp.sum(e, axis=0, keepdims=True)
    lb = jnp.sum(sm[0:layer + 1, :], axis=0, keepdims=True) - sm[0:1, :]

    q = q_ref[...]
    qa = q * _sigmoid(q)
    fg = lb + (1.0 - lb) * _sigmoid(f_ref[...])
    logf = jnp.log(fg)
    kk = 1.0 - fg
    v = i_ref[...]
    if l_valid is not None:
        valid = (lax.broadcasted_iota(jnp.int32, (tb, 1), 0) + c * tb) < l_valid
        logf = jnp.where(valid, logf, 0.0)
        kk = jnp.where(valid, kk, 0.0)
    vb = v.astype(BF16)

    tr = lax.broadcasted_iota(jnp.int32, (tb, tb), 0)
    tc = lax.broadcasted_iota(jnp.int32, (tb, tb), 1)
    same_chunk = _div_pow2(tr, chunk) == _div_pow2(tc, chunk)
    tri = jnp.where(same_chunk, jnp.where(tr >= tc, 1.0, 0.0), 0.0).astype(BF16)
    hi = logf.astype(BF16)
    lo = (logf - hi.astype(F32)).astype(BF16)
    bcum = _dot(tri, hi) + _dot(tri, lo)
    qhat = (qa * jnp.exp(bcum)).astype(BF16)

    nsub = chunk // HGRN_SUB
    khat, dec, qloc, kloc, masks = [], [], [], [], []
    spread = jnp.zeros((1, bcum.shape[1]), F32)
    for ci in range(nchunk):
        c0 = ci * chunk
        blast = bcum[c0 + chunk - 1:c0 + chunk, :]
        khat.append((kk[c0:c0 + chunk, :] * jnp.exp(blast - bcum[c0:c0 + chunk, :])).astype(BF16))
        dec.append(jnp.exp(blast))
        for i in range(nsub):
            r0 = c0 + i * HGRN_SUB
            r1 = r0 + HGRN_SUB
            base = bcum[r0 - 1:r0, :] if i > 0 else jnp.zeros((1, bcum.shape[1]), F32)
            spread = jnp.minimum(spread, bcum[r1 - 1:r1, :] - base)
            qloc.append((qa[r0:r1, :] * jnp.exp(bcum[r0:r1, :] - base)).astype(BF16))
            kloc.append((kk[c0:r1, :] * jnp.exp(jnp.minimum(base - bcum[c0:r1, :], HGRN_EXP_CLAMP))).astype(BF16))
    wild = jnp.min(spread) < -HGRN_EXP_CLAMP
    for i in range(nsub):
        ncols = (i + 1) * HGRN_SUB
        ar = lax.broadcasted_iota(jnp.int32, (HGRN_SUB, ncols), 0) + i * HGRN_SUB
        ac = lax.broadcasted_iota(jnp.int32, (HGRN_SUB, ncols), 1)
        in_block = ac >= i * HGRN_SUB
        masks.append(jnp.logical_and(ar >= ac, jnp.logical_not(jnp.logical_and(wild, in_block))))

    hsl = [slice(HGRN_DK * h, HGRN_DK * (h + 1)) for h in range(HGRN_HEADS)]

    def in_block_exact():
        pos = jnp.bitwise_and(lax.broadcasted_iota(jnp.int32, (tb, 1), 0), HGRN_SUB - 1)
        out = jnp.zeros((tb, bcum.shape[1]), F32)
        for j in range(HGRN_SUB):
            ok = pos >= j
            kj, bj, vj = (kk, bcum, v) if j == 0 else (pltpu.roll(x, j, 0) for x in (kk, bcum, v))
            e = jnp.where(ok, qa * kj * jnp.exp(jnp.where(ok, bcum - bj, 0.0)), 0.0)
            out = out + jnp.concatenate(
                [jnp.sum(e[:, sl], axis=-1, keepdims=True) * vj[:, sl] for sl in hsl], axis=1)
        return out

    blocks = [(ci, i) for ci in range(nchunk) for i in range(nsub)]
    att = [[_dot_nt(qloc[ci * nsub + i][:, sl], kloc[ci * nsub + i][:, sl]) for ci, i in blocks] for sl in hsl]
    att = [[jnp.where(masks[i], a, 0.0).astype(BF16) for a, (ci, i) in zip(row, blocks)] for row in att]
    intra = [[_dot(a, vb[ci * chunk:ci * chunk + (i + 1) * HGRN_SUB, sl]) for a, (ci, i) in zip(row, blocks)]
             for row, sl in zip(att, hsl)]
    kv = [[_dot_tn(khat[ci][:, sl], vb[ci * chunk:(ci + 1) * chunk, sl]) for ci in range(nchunk)] for sl in hsl]
    dcol = [[_row_to_col(dec[ci][:, sl], HGRN_DK) for ci in range(nchunk)] for sl in hsl]
    st = [s_scr[h] for h in range(HGRN_HEADS)]
    inter = [[] for _ in hsl]
    for ci in range(nchunk):
        rows = slice(ci * chunk, (ci + 1) * chunk)
        for h, sl in enumerate(hsl):
            inter[h].append(_dot(qhat[rows, sl], st[h].astype(BF16)))
        for h in range(HGRN_HEADS):
            st[h] = dcol[h][ci] * st[h] + kv[h][ci]
    o_heads = []
    for h in range(HGRN_HEADS):
        s_scr[h] = st[h]
        parts = [inter[h][ci][i * HGRN_SUB:(i + 1) * HGRN_SUB, :] + intra[h][ci * nsub + i] for ci, i in blocks]
        o_heads.append(parts[0] if len(parts) == 1 else jnp.concatenate(parts, axis=0))

    o = jnp.concatenate(o_heads, axis=1)
    o = lax.cond(wild, lambda: o + in_block_exact(), lambda: o)
    g = g_ref[...]
    o_ref[...] = (_rms(o, on_ref[...]) * (g * _sigmoid(g))).astype(BF16)

    @pl.when(c == pl.num_programs(1) - 1)
    def _():
        s_out[...] = s_scr[...]


def _hgrn(proj, lbp, on, s0, *, chunk, nchunk, layer, l_valid):
    b, l, _ = proj.shape
    w = D_MODEL
    tb = chunk * nchunk

    def col(k):
        return pl.BlockSpec((None, tb, w), lambda bi, c: (bi, c, k))

    st_spec = pl.BlockSpec((None, HGRN_HEADS, HGRN_DK, HGRN_DK), lambda bi, c: (bi, 0, 0, 0))
    return pl.pallas_call(
        functools.partial(_hgrn_kernel, chunk=chunk, nchunk=nchunk, layer=layer, l_valid=l_valid),
        grid=(b, l // tb),
        in_specs=[col(0), col(1), col(2), col(3), _full_spec(lbp.shape), _full_spec(on.shape), st_spec],
        out_specs=[pl.BlockSpec((None, tb, w), lambda bi, c: (bi, c, 0)), st_spec],
        out_shape=[jax.ShapeDtypeStruct((b, l, w), BF16), jax.ShapeDtypeStruct(s0.shape, F32)],
        scratch_shapes=[pltpu.VMEM((HGRN_HEADS, HGRN_DK, HGRN_DK), F32)],
        compiler_params=_cparams(("arbitrary", "arbitrary")), name="hgrn",
    )(proj, proj, proj, proj, lbp, on, s0)


def _pad_last(x, n):
    return jnp.pad(x, [(0, 0)] * (x.ndim - 1) + [(0, n - x.shape[-1])])


def _head_pad(w, per):
    k = w.shape[0]
    return _pad_last(w.reshape(k, -1, per), LANES).reshape(k, -1)


def _rope_tables(pos):
    half = MLA_ROPE // 2
    inv = ROPE_THETA ** (-jnp.arange(half, dtype=F32) / half)
    ang = pos.astype(F32)[:, None] * inv[None, :]
    cos, sin = jnp.cos(ang), jnp.sin(ang)
    n = pos.shape[0]
    z = lambda w: jnp.zeros((n, w), F32)
    scale = MLA_QK ** -0.5 * LOG2E
    cq = scale * jnp.concatenate([jnp.ones((n, MLA_NOPE), F32), cos, cos, z(LANES - MLA_QK)], axis=1)
    s1q = scale * jnp.concatenate([z(MLA_NOPE + half), sin, z(LANES - MLA_QK)], axis=1)
    s2q = scale * jnp.concatenate([z(MLA_NOPE), -sin, z(half + LANES - MLA_QK)], axis=1)
    ck = jnp.concatenate([cos, cos, z(LANES - MLA_ROPE)], axis=1)
    s1k = jnp.concatenate([z(half), sin, z(LANES - MLA_ROPE)], axis=1)
    s2k = jnp.concatenate([-sin, z(LANES - half)], axis=1)
    return (cq, s1q, s2q, ck, s1k, s2k)


def _block_diag(x):
    g, a, b = x.shape
    eye = jnp.eye(g, dtype=x.dtype)
    return (x[:, :, None, :] * eye[:, None, :, None]).reshape(g * a, g * b)


def kernel(x_prompt, x_sample, cache_mla_latent, cache_mla_krope, state_s5_re, state_s5_im, state_hgrn, cache_mem_k, cache_mem_v, page_table, mem_prompt, norm_mix, norm_mem, norm_memsrc, norm_mlp, w_mem_q, w_mem_k, w_mem_v, w_mem_o, mem_q_gain, mem_k_gain, w_mlp_up, w_mlp_down, w_in_even, mla_cq_norm, mla_ckv_norm, w_mla_uq, w_mla_ukv, mla_qn_nope, mla_qn_rope, mla_kn_nope, mla_kn_rope, s5_lambda_re, s5_lambda_im, s5_log_step, s5_b_re, s5_b_im, s5_c_re, s5_c_im, s5_d, s5_w_glu, s5_b_glu, w_out_even, w_in_odd, hgrn_lower_bounds, hgrn_out_norm, w_out_odd):
    bsz, seq, _ = x_prompt.shape
    dbs, dseq, _ = x_sample.shape
    depth = norm_mix.shape[0]
    past_len = page_table.shape[1] * PAGE_SIZE
    ns = SAMPLE_PAD
    mem_len = mem_prompt.shape[1]
    row2 = lambda a: a.reshape(1, -1).astype(F32)

    hp = x_prompt.reshape(bsz * seq, D_MODEL)
    hs = jnp.pad(x_sample, ((0, 0), (0, ns - dseq), (0, 0))).reshape(dbs * ns, D_MODEL)

    tabs_p = _rope_tables(jnp.arange(seq, dtype=jnp.int32))
    pos_s = past_len + jnp.arange(ns, dtype=jnp.int32)
    tabs_s = tuple(jnp.tile(t, (dbs, 1)) for t in _rope_tables(pos_s))

    outs_p = {k: [] for k in ("lat", "kr", "s5r", "s5i", "hg", "mk", "mv")}
    outs_s = {k: [] for k in ("lat", "kr", "s5r", "s5i", "hg")}

    tm_p = TM_EVEN_PROJ
    nl_p = seq // tm_p
    tm_r = TM_RESIDUAL
    nl_r = seq // tm_r

    for l in range(depth):
        if l % 2 == 0:
            e = l // 2
            w_in = w_in_even[e]
            o1 = MLA_Q_LORA + MLA_KV_LORA
            wp = jnp.concatenate([w_in[:, :o1], _pad_last(w_in[:, o1:o1 + MLA_ROPE], LANES),
                                  w_in[:, o1 + MLA_ROPE:]], axis=1).astype(BF16)
            wuq = _head_pad(w_mla_uq[e], MLA_QK).astype(BF16)
            ukv = w_mla_ukv[e].reshape(MLA_KV_LORA, MLA_HEADS, MLA_NOPE + MLA_V)
            wuk_c = ukv[:, :, :MLA_NOPE].reshape(MLA_KV_LORA, -1)
            wuv_c = ukv[:, :, MLA_NOPE:].reshape(MLA_KV_LORA, -1)
            wkv = jnp.concatenate([_head_pad(wuk_c, MLA_NOPE), _head_pad(wuv_c, MLA_V)], axis=1).astype(BF16)
            qg = _pad_last(jnp.concatenate([mla_qn_nope[e], mla_qn_rope[e], mla_qn_rope[e]])[None, :], LANES)
            kg = _pad_last(jnp.concatenate([mla_kn_nope[e], mla_kn_rope[e], mla_kn_rope[e]])[None, :], LANES)
            cqn = row2(mla_cq_norm[e])
            ckvn = row2(mla_ckv_norm[e])
            g_mix = row2(norm_mix[l])
            score_bound = MLA_QK ** 0.5 * LOG2E * jnp.max(jnp.abs(qg)) * jnp.max(jnp.abs(kg))
            bounded = (score_bound <= MAX_SCORE_BOUND).astype(jnp.int32).reshape(1)
            lane = jnp.arange(LANES)
            aug = (jnp.where(lane == MLA_QK, 1.0, 0.0).astype(F32)[None, :],
                   jnp.where(lane == MLA_QK, -score_bound, 0.0).astype(F32)[None, :],
                   jnp.tile(jnp.where(lane == MLA_V, 1.0, 0.0).astype(F32), MLA_HEADS)[None, :])

            brm = _block_diag(jnp.swapaxes(s5_b_re[e], 1, 2)).astype(BF16)
            bim = _block_diag(jnp.swapaxes(s5_b_im[e], 1, 2)).astype(BF16)
            crm = _block_diag(jnp.swapaxes(s5_c_re[e], 1, 2)).astype(BF16)
            cim = _block_diag(jnp.swapaxes(s5_c_im[e], 1, 2)).astype(BF16)
            lamr = row2(s5_lambda_re[e])
            lami = row2(s5_lambda_im[e])
            lstep = row2(jnp.repeat(s5_log_step[e], S5_STATE))
            s5_consts = (lamr, lami, lstep, brm, bim, crm, cim, row2(s5_d[e]), s5_w_glu[e].astype(BF16),
                         row2(s5_b_glu[e]))
            w_out = w_out_even[e]
            wo_att_c = w_out[:MLA_HEADS * MLA_V].astype(BF16)
            wo_att_p = _pad_last(w_out[:MLA_HEADS * MLA_V].reshape(MLA_HEADS, MLA_V, D_MODEL).swapaxes(1, 2),
                                 LANES).swapaxes(1, 2).reshape(HP, D_MODEL).astype(BF16)
            wo_s5 = w_out[MLA_HEADS * MLA_V:].astype(BF16)

            q, k, v, ckv, kr, u = _even_proj(
                hp, g_mix, wp, cqn, wuq, qg, ckvn, wkv, kg, aug, tabs_p, tm=tm_p,
                u_shape=(seq, bsz * S5_WIDTH), emit_qk=False,
                u_spec=pl.BlockSpec((tm_p, S5_WIDTH), lambda i: (i % nl_p, i // nl_p)))
            o_att = _flash_attention(bounded, q.reshape(bsz, seq, HP), k.reshape(bsz, seq, HP), v,
                                     tq=TQ_ATTN, hg=ATTN_HEAD_GROUP)
            z0 = jnp.zeros((bsz, S5_NSTATE), F32)
            o_s5, hr, hi = _s5(u, z0, z0, *s5_consts, tt=S5_STEPS, nb=bsz, strip=S5_STRIP, interleave=True)
            hp = _mm_res([o_att.reshape(bsz * seq, HP), o_s5], [wo_att_p, wo_s5], hp,
                         tm=tm_r, op_specs=[row_spec(tm_r, HP),
                                            pl.BlockSpec((tm_r, S5_WIDTH), lambda i: (i % nl_r, i // nl_r))])
            outs_p["lat"].append(ckv.reshape(bsz, seq, MLA_KV_LORA))
            outs_p["kr"].append(kr.reshape(bsz, seq, MLA_ROPE))
            outs_p["s5r"].append(hr.reshape(bsz, S5_GROUPS, S5_STATE))
            outs_p["s5i"].append(hi.reshape(bsz, S5_GROUPS, S5_STATE))

            m_s = dbs * ns
            q, k, v, ckv, kr, u, qk = _even_proj(
                hs, g_mix, wp, cqn, wuq, qg, ckvn, wkv, kg, aug, tabs_s, tm=tm_p,
                u_shape=(m_s, S5_WIDTH), u_spec=row_spec(tm_p, S5_WIDTH), emit_qk=True)
            del q, k, v
            ckv3 = ckv.reshape(dbs, ns, MLA_KV_LORA)
            kr3 = kr.reshape(dbs, ns, MLA_ROPE)
            qk4 = qk.reshape(dbs, ns, MLA_HEADS, LANES)
            eye_h = jnp.eye(MLA_HEADS, dtype=BF16)
            ncols = ns * MLA_HEADS
            sub = LANES // MLA_HEADS
            nblk = MLA_NOPE // sub
            qn = (jnp.transpose(qk4[..., :MLA_NOPE], (0, 2, 3, 1))[..., None]
                  * eye_h[None, :, None, None, :])
            qn = qn.reshape(dbs, MLA_HEADS, nblk, sub, ncols).swapaxes(1, 2).reshape(dbs, MLA_HEADS * MLA_NOPE, ncols)
            qn = _pad_last(qn, LANES)
            wuk_p = (wuk_c.reshape(MLA_KV_LORA, MLA_HEADS, nblk, sub).swapaxes(1, 2)
                     .reshape(MLA_KV_LORA, MLA_HEADS * MLA_NOPE).astype(BF16))
            qr = jnp.transpose(qk4[..., MLA_NOPE:MLA_QK], (0, 3, 1, 2)).reshape(dbs, MLA_ROPE, ncols)
            qr = _pad_last(qr, LANES)
            colmask = (jnp.arange(LANES) < ncols)
            e16 = ((jnp.arange(LANES)[:, None] // sub == (jnp.arange(LANES)[None, :] % MLA_HEADS))
                   & colmask[None, :]).astype(BF16)
            onr = jnp.broadcast_to(colmask[None, :], (MLA_ROPE, LANES)).astype(BF16)
            zb = lambda r: jnp.zeros((dbs, r, LANES), BF16)
            bc = lambda x: jnp.broadcast_to(x[None], (dbs,) + x.shape)
            rhs2 = jnp.concatenate([
                jnp.concatenate([bc(e16), zb(LANES)], axis=2),
                jnp.concatenate([zb(MLA_ROPE), qr], axis=2),
                jnp.concatenate([bc(onr), zb(MLA_ROPE)], axis=2),
                jnp.zeros((dbs, LANES - 2 * MLA_ROPE, 2 * LANES), BF16)], axis=1)
            nnew = NEW_KEYS_PAD
            cnew = jnp.pad(ckv3, ((0, 0), (0, nnew - ns), (0, 0)))
            krnew = jnp.pad(kr3, ((0, 0), (0, nnew - ns), (0, 0)))
            o_att_s = _paged_attention(page_table, bounded, cache_mla_latent, jnp.swapaxes(cache_mla_krope, 2, 3), e,
                                       qn, rhs2, wuk_p, cnew, krnew, wuv_c.astype(BF16),
                                       jnp.full((1, LANES), score_bound, F32),
                                       npg=PAGES_PER_STEP, ngrp=PAGE_GROUPS, nq=ns)
            u_tb = jnp.transpose(u.reshape(dbs, ns, S5_WIDTH)[:, :dseq], (1, 0, 2)).reshape(dseq * dbs, S5_WIDTH)
            o_s5, hr, hi = _s5(u_tb, state_s5_re[e].reshape(dbs, S5_NSTATE), state_s5_im[e].reshape(dbs, S5_NSTATE),
                               *s5_consts, tt=dseq, nb=dbs, strip=S5_STRIP, interleave=False)
            o_s5 = jnp.transpose(o_s5.reshape(dseq, dbs, S5_WIDTH), (1, 0, 2))
            o_s5 = jnp.pad(o_s5, ((0, 0), (0, ns - dseq), (0, 0))).reshape(m_s, S5_WIDTH)
            hs = _mm_res([o_att_s.reshape(m_s, MLA_HEADS * MLA_V), o_s5], [wo_att_c, wo_s5], hs, tm=m_s,
                         op_specs=[row_spec(m_s, MLA_HEADS * MLA_V), row_spec(m_s, S5_WIDTH)])
            outs_s["lat"].append(ckv3[:, :dseq])
            outs_s["kr"].append(kr3[:, :dseq])
            outs_s["s5r"].append(hr.reshape(dbs, S5_GROUPS, S5_STATE))
            outs_s["s5i"].append(hi.reshape(dbs, S5_GROUPS, S5_STATE))
        else:
            o = l // 2
            g_mix = row2(norm_mix[l])
            w_in = w_in_odd[o].astype(BF16)
            w_out = w_out_odd[o].astype(BF16)
            on = row2(hgrn_out_norm[o])
            lbp = hgrn_lower_bounds.astype(F32)

            proj = _norm_mm(hp, g_mix, w_in, tm=TM_NORM_MM, tn=TN_NORM_MM)
            s_zero = jnp.zeros((bsz, HGRN_HEADS, HGRN_DK, HGRN_DK), F32)
            og, st = _hgrn(proj.reshape(bsz, seq, 4 * D_MODEL), lbp, on, s_zero, chunk=HGRN_CHUNK, nchunk=HGRN_NCHUNK, layer=l,
                           l_valid=None)
            hp = _mm_res([og.reshape(bsz * seq, D_MODEL)], [w_out], hp, tm=tm_r, op_specs=[row_spec(tm_r, D_MODEL)])
            outs_p["hg"].append(st)

            m_s = dbs * ns
            proj = _norm_mm(hs, g_mix, w_in, tm=m_s, tn=TN_NORM_MM)
            lpad = HGRN_SUB
            proj = jnp.pad(proj.reshape(dbs, ns, 4 * D_MODEL), ((0, 0), (0, lpad - ns), (0, 0)))
            og, st = _hgrn(proj, lbp, on, state_hgrn[o], chunk=lpad, nchunk=1, layer=l, l_valid=dseq)
            hs = _mm_res([og[:, :ns].reshape(m_s, D_MODEL)], [w_out], hs, tm=m_s, op_specs=[row_spec(m_s, D_MODEL)])
            outs_s["hg"].append(st)

        g_mem = row2(norm_mem[l])
        wq = w_mem_q[l].astype(BF16)
        wo = w_mem_o[l].astype(BF16)
        mqg = row2(mem_q_gain[l])
        wkv_m = jnp.concatenate([w_mem_k[l], w_mem_v[l]], axis=1).astype(BF16)
        mk, mv = _mem_kv(mem_prompt.reshape(bsz * mem_len, D_MODEL), row2(norm_memsrc[l]), wkv_m,
                         row2(mem_k_gain[l]), tm=TM_MEM_KV)
        mk = mk.reshape(bsz, mem_len, MEM_WIDTH)
        mv = mv.reshape(bsz, mem_len, MEM_WIDTH)
        outs_p["mk"].append(mk.reshape(bsz, mem_len, MEM_HEADS, MEM_HEAD_DIM))
        outs_p["mv"].append(mv.reshape(bsz, mem_len, MEM_HEADS, MEM_HEAD_DIM))
        hp = _mem_attn(hp.reshape(bsz, seq, D_MODEL), g_mem, wq, mqg, mk, mv, wo,
                       nb=1, tl=TL_MEM).reshape(bsz * seq, D_MODEL)
        pair_shape = (depth, dbs, mem_len // 2, 2 * MEM_HEADS, MEM_HEAD_DIM)
        hs = _mem_attn(hs.reshape(dbs, ns, D_MODEL), g_mem, wq, mqg, cache_mem_k.reshape(pair_shape),
                       cache_mem_v.reshape(pair_shape), wo, nb=SAMPLE_MEM_SEQS, tl=ns, layer=l).reshape(dbs * ns, D_MODEL)

        g_mlp = row2(norm_mlp[l])
        wu = w_mlp_up[l].astype(BF16)
        wd = w_mlp_down[l].astype(BF16)
        hp = _mlp(hp, g_mlp, wu, wd, tm=TM_MLP, tf=TF_MLP)
        hs = _mlp(hs, g_mlp, wu, wd, tm=dbs * ns, tf=TF_MLP)

    y_p = hp.reshape(bsz, seq, D_MODEL)
    y_s = hs.reshape(dbs, ns, D_MODEL)[:, :dseq]
    return (y_p, y_s,
            jnp.stack(outs_p["lat"], axis=1), jnp.stack(outs_p["kr"], axis=1),
            jnp.stack(outs_p["s5r"]), jnp.stack(outs_p["s5i"]), jnp.stack(outs_p["hg"]),
            jnp.stack(outs_p["mk"]), jnp.stack(outs_p["mv"]),
            jnp.stack(outs_s["lat"], axis=1), jnp.stack(outs_s["kr"], axis=1),
            jnp.stack(outs_s["s5r"]), jnp.stack(outs_s["s5i"]), jnp.stack(outs_s["hg"]))
```

```python
import functools
import math

import jax
import jax.numpy as jnp
from jax import lax
from jax.experimental import pallas as pl
from jax.experimental.pallas import tpu as pltpu

F32 = jnp.float32
BF16 = jnp.bfloat16

LANES = 128
VMEM_LIMIT_BYTES = 56 * 1024 * 1024

D_MODEL = 1024
MLA_HEADS = 8
MLA_NOPE = 64
MLA_ROPE = 32
MLA_QK = MLA_NOPE + MLA_ROPE
MLA_V = 64
MLA_Q_LORA = 768
MLA_KV_LORA = 256
ROPE_THETA = 10000.0
PAGE_SIZE = 128
S5_WIDTH = 512
S5_GROUP = 16
S5_GROUPS = S5_WIDTH // S5_GROUP
S5_STATE = 64
S5_NSTATE = S5_GROUPS * S5_STATE
HGRN_HEADS = 8
HGRN_DK = 128
HGRN_SUB = 32
HGRN_EXP_CLAMP = 80.0
MAX_SCORE_BOUND = 40.0
MEM_HEADS = 4
MEM_HEAD_DIM = 128
MEM_WIDTH = MEM_HEADS * MEM_HEAD_DIM
D_FF = 4 * D_MODEL
EPS = 1e-6
LOG2E = math.log2(math.e)
HP = MLA_HEADS * LANES

TM_EVEN_PROJ = 512
TQ_ATTN = 512
ATTN_HEAD_GROUP = 4
PAGES_PER_STEP = 64
PAGE_GROUPS = 8
S5_STEPS = 64
S5_STRIP = 512
TM_RESIDUAL = 1024
TM_NORM_MM, TN_NORM_MM = 512, 4096
TM_MLP, TF_MLP = 512, 4096
HGRN_CHUNK, HGRN_NCHUNK = 64, 4
TL_MEM = 1024
TM_MEM_KV = 512
SAMPLE_MEM_SEQS = 8
SAMPLE_PAD = 8
NEW_KEYS_PAD = 16


def _cparams(sem):
    return pltpu.CompilerParams(dimension_semantics=sem, vmem_limit_bytes=VMEM_LIMIT_BYTES)


def _rms(x, g):
    return x * lax.rsqrt(jnp.mean(x * x, axis=-1, keepdims=True) + EPS) * g


def _sigmoid(x):
    return 1.0 / (1.0 + jnp.exp(-x))


def _dot(a, b):
    return jnp.dot(a, b, preferred_element_type=F32)


def _dot_nt(a, b):
    return lax.dot_general(a, b, (((1,), (1,)), ((), ())), preferred_element_type=F32)


def _dot_tn(a, b):
    return lax.dot_general(a, b, (((0,), (0,)), ((), ())), preferred_element_type=F32)


def _row_to_col(row, n):
    r = lax.broadcasted_iota(jnp.int32, (n, n), 0)
    c = lax.broadcasted_iota(jnp.int32, (n, n), 1)
    return jnp.sum(jnp.where(r == c, jnp.broadcast_to(row, (n, n)), 0.0), axis=1, keepdims=True)


def _div_pow2(x, d):
    return lax.shift_right_logical(x, int(math.log2(d)))


def _full_spec(shape):
    nd = len(shape)
    return pl.BlockSpec(shape, lambda *_: (0,) * nd)


def row_spec(tm, width):
    return pl.BlockSpec((tm, width), lambda i: (i, 0))


def _even_proj_kernel(h_ref, g_ref, wp_ref, cqn_ref, wuq_ref, qg_ref, ckvn_ref, wkv_ref, kg_ref,
                      qaug_ref, kaug_ref, vaug_ref, cq_ref, s1q_ref, s2q_ref, ck_ref, s1k_ref, s2k_ref,
                      q_out, k_out, v_out, ckv_out, kr_out, u_out, *maybe_qk_out, nrg):
    o1 = MLA_Q_LORA
    o2 = o1 + MLA_KV_LORA
    o3 = o2 + LANES
    half = MLA_ROPE // 2
    inv_qk = 1.0 / MLA_QK
    qg = qg_ref[...]
    kg = kg_ref[...]
    tm = h_ref.shape[0]
    groups = [slice(tm // nrg * t, tm // nrg * (t + 1)) for t in range(nrg)]
    proj = [_dot(_rms(h_ref[r, :], g_ref[...]).astype(BF16), wp_ref[...]) for r in groups]
    qf, kv = [], []
    for r, p in zip(groups, proj):
        u_out[r, :] = p[:, o3:]
        qf.append(_dot(_rms(p[:, :o1], cqn_ref[...]).astype(BF16), wuq_ref[...]))
        ckv = _rms(p[:, o1:o2], ckvn_ref[...])
        ckv_out[r, :] = ckv
        kv.append(_dot(ckv.astype(BF16), wkv_ref[...]))
    for r, p, qfr, kvr in zip(groups, proj, qf, kv):
        kr = p[:, o2:o3]
        krr = (kr * ck_ref[r, :] + pltpu.roll(kr, half, 1) * s1k_ref[r, :]
               + pltpu.roll(kr, LANES - half, 1) * s2k_ref[r, :])
        kr_out[r, :] = krr[:, :MLA_ROPE]
        kr_sh = pltpu.roll(krr, MLA_NOPE, 1)
        cq_t, s1q_t, s2q_t = cq_ref[r, :], s1q_ref[r, :], s2q_ref[r, :]
        for h in range(MLA_HEADS):
            sl = slice(LANES * h, LANES * (h + 1))
            qh = qfr[:, sl]
            qh = qh * lax.rsqrt(jnp.sum(qh * qh, axis=-1, keepdims=True) * inv_qk + EPS) * qg
            qh = (qh * cq_t + pltpu.roll(qh, half, 1) * s1q_t + pltpu.roll(qh, LANES - half, 1) * s2q_t)
            q_out[r, sl] = (qh + qaug_ref[...]).astype(BF16)
            if maybe_qk_out:
                maybe_qk_out[0][r, sl] = (qh * kg).astype(BF16)
            kh = kvr[:, sl] + kr_sh
            kh = kh * lax.rsqrt(jnp.sum(kh * kh, axis=-1, keepdims=True) * inv_qk + EPS) * kg
            k_out[r, sl] = (kh + kaug_ref[...]).astype(BF16)
        v_out[:, r] = (kvr[:, HP:] + vaug_ref[...]).T.astype(BF16)


def _even_proj(h, g, wp, cqn, wuq, qg, ckvn, wkv, kg, aug, tabs, *, tm, u_shape, u_spec, emit_qk):
    m = h.shape[0]
    ltab = tabs[0].shape[0]
    ntab = ltab // tm
    row = lambda i: (i, 0)
    tab_spec = pl.BlockSpec((tm, LANES), lambda i: (i % ntab, 0))
    in_specs = ([pl.BlockSpec((tm, D_MODEL), row), _full_spec(g.shape), _full_spec(wp.shape),
                 _full_spec(cqn.shape), _full_spec(wuq.shape), _full_spec(qg.shape),
                 _full_spec(ckvn.shape), _full_spec(wkv.shape), _full_spec(kg.shape)]
                + [_full_spec(a.shape) for a in aug] + [tab_spec] * 6)
    out_shape = [jax.ShapeDtypeStruct((m, HP), BF16), jax.ShapeDtypeStruct((m, HP), BF16),
                 jax.ShapeDtypeStruct((m // tm, HP, tm), BF16), jax.ShapeDtypeStruct((m, MLA_KV_LORA), F32),
                 jax.ShapeDtypeStruct((m, MLA_ROPE), F32), jax.ShapeDtypeStruct(u_shape, F32)]
    out_specs = [pl.BlockSpec((tm, HP), row), pl.BlockSpec((tm, HP), row),
                 pl.BlockSpec((None, HP, tm), lambda i: (i, 0, 0)),
                 pl.BlockSpec((tm, MLA_KV_LORA), row), pl.BlockSpec((tm, MLA_ROPE), row),
                 u_spec]
    if emit_qk:
        out_shape.append(jax.ShapeDtypeStruct((m, HP), BF16))
        out_specs.append(pl.BlockSpec((tm, HP), row))
    return pl.pallas_call(
        functools.partial(_even_proj_kernel, nrg=2), grid=(m // tm,), in_specs=in_specs, out_specs=out_specs,
        out_shape=out_shape,
        compiler_params=_cparams(("arbitrary",)), name="even_proj",
    )(h, g, wp, cqn, wuq, qg, ckvn, wkv, kg, *aug, *tabs)


def _flash_kernel(bounded_ref, q_ref, k_ref, vt_ref, o_ref, *, tq, tk, hg):
    i = pl.program_id(1)
    nfull = (i * tq) // tk
    key = lax.broadcasted_iota(jnp.int32, (tk, tq), 0) + nfull * tk
    qry = lax.broadcasted_iota(jnp.int32, (tk, tq), 1) + i * tq
    causal = qry >= key

    heads = [slice(LANES * h, LANES * (h + 1)) for h in range(MLA_HEADS)]

    def accumulate(j, acc, masked):
        off = pl.multiple_of(j * tk, tk)
        out = []
        for h0 in range(0, MLA_HEADS, hg):
            grp = range(h0, h0 + hg)
            scores = [_dot_nt(k_ref[pl.ds(off, tk), heads[h]], q_ref[:, heads[h]]) for h in grp]
            if masked:
                scores = [jnp.where(causal, s, -jnp.inf) for s in scores]
            probs = [jnp.exp2(s).astype(BF16) for s in scores]
            out += [acc[h] + _dot(vt_ref[j, heads[h], :], p) for p, h in zip(probs, grp)]
        return tuple(out)

    @pl.when(bounded_ref[0] != 0)
    def _():
        acc = lax.fori_loop(0, nfull, lambda j, c: accumulate(j, c, False),
                            (jnp.zeros((LANES, tq), F32),) * MLA_HEADS)
        acc = accumulate(nfull, acc, True)
        for h, sl in enumerate(heads):
            o_ref[:, sl] = (acc[h] / acc[h][MLA_V:MLA_V + 1, :]).T.astype(BF16)

    def update(j, carry, masked):
        off = pl.multiple_of(j * tk, tk)
        out = []
        for h0 in range(0, MLA_HEADS, hg):
            grp = range(h0, h0 + hg)
            scores = [_dot_nt(k_ref[pl.ds(off, tk), heads[h]], q_ref[:, heads[h]]) for h in grp]
            probs, stats = [], []
            for h, s in zip(grp, scores):
                m, l = carry[3 * h], carry[3 * h + 1]
                if masked:
                    s = jnp.where(causal, s, -jnp.inf)
                m_new = jnp.maximum(m, jnp.max(s, axis=0, keepdims=True))
                alpha = jnp.exp2(m - m_new)
                p = jnp.exp2(s - m_new)
                stats.append((m_new, alpha * l + jnp.sum(p, axis=0, keepdims=True), alpha))
                probs.append(p.astype(BF16))
            pv = [_dot(vt_ref[j, heads[h], :], p) for p, h in zip(probs, grp)]
            for h, (m_new, l_new, alpha), o in zip(grp, stats, pv):
                out += [m_new, l_new, alpha * carry[3 * h + 2] + o]
        return tuple(out)

    @pl.when(bounded_ref[0] == 0)
    def _():
        init = (jnp.full((1, tq), -jnp.inf, F32), jnp.zeros((1, tq), F32), jnp.zeros((LANES, tq), F32))
        carry = lax.fori_loop(0, nfull, lambda j, c: update(j, c, False), init * MLA_HEADS)
        carry = update(nfull, carry, True)
        for h, sl in enumerate(heads):
            o_ref[:, sl] = (carry[3 * h + 2] / carry[3 * h + 1]).T.astype(BF16)


def _flash_attention(bounded, q, k, vt, *, tq, hg=2):
    b, l, _ = q.shape
    tk = vt.shape[2]
    nkb = l // tk
    grid_spec = pltpu.PrefetchScalarGridSpec(
        num_scalar_prefetch=1, grid=(b, l // tq),
        in_specs=[pl.BlockSpec((None, tq, HP), lambda bi, i, f: (bi, i, 0)),
                  pl.BlockSpec((None, l, HP), lambda bi, i, f: (bi, 0, 0)),
                  pl.BlockSpec((nkb, HP, tk), lambda bi, i, f: (bi, 0, 0))],
        out_specs=pl.BlockSpec((None, tq, HP), lambda bi, i, f: (bi, i, 0)))
    return pl.pallas_call(
        functools.partial(_flash_kernel, tq=tq, tk=tk, hg=hg), grid_spec=grid_spec,
        out_shape=jax.ShapeDtypeStruct((b, l, HP), BF16),
        compiler_params=_cparams(("arbitrary", "arbitrary")), name="prompt_attention",
    )(bounded, q, k, vt)


def _paged_kernel(pt_ref, bounded_ref, *refs, npg, ngrp, nsteps, nq):
    lat_refs = refs[:npg]
    krt_refs = refs[npg:2 * npg]
    (qn_ref, rhs2_ref, wuk_ref, cnew_ref, krnew_ref, wuv_ref, bound_ref,
     o_ref, wabs, m_scr, l_scr, a_scr, l_acc, a_acc) = refs[2 * npg:]
    del pt_ref
    s = pl.program_id(1)
    nslots = nsteps * ngrp + 1
    ncol = LANES
    inv_qk = 1.0 / MLA_QK
    bounded = bounded_ref[0] != 0
    last = s == nsteps - 1

    @pl.when(s == 0)
    def _():
        wabs[...] = _dot(wuk_ref[...], qn_ref[...]).astype(BF16)
        l_acc[...] = jnp.zeros(l_acc.shape, F32)
        a_acc[...] = jnp.zeros(a_acc.shape, F32)

    def scores(blocks, mask):
        kn = [_dot(c, wuk_ref[...]) for c, _ in blocks]
        sq = [k * k for k in kn]
        psum = [q[:, 0:LANES] + q[:, LANES:2 * LANES] + q[:, 2 * LANES:3 * LANES] + q[:, 3 * LANES:] for q in sq]
        r2 = [_dot(jnp.concatenate([p.astype(BF16), x], axis=1), rhs2_ref[...]) for p, (_, x) in zip(psum, blocks)]
        scn = [_dot(c, wabs[...]) for c, _ in blocks]
        out = []
        for t in range(len(blocks)):
            sc = (scn[t] + r2[t][:, LANES:]) * lax.rsqrt(r2[t][:, :LANES] * inv_qk + EPS)
            out.append(sc if mask is None else jnp.where(mask, sc, -jnp.inf))
        return out

    def stats(blocks, scs):
        probs, out = [], []
        for sc in scs:
            m = jnp.max(sc, axis=0, keepdims=True)
            p = jnp.exp2(sc - m)
            out.append((m, jnp.sum(p, axis=0, keepdims=True)))
            probs.append(p.astype(BF16))
        acc = [_dot_tn(p, c) for p, (c, _) in zip(probs, blocks)]
        return [(m, l, a) for (m, l), a in zip(out, acc)]

    def accumulate(blocks, scs):
        probs = [jnp.exp2(sc - bound_ref[...]) for sc in scs]
        l_new = l_acc[...]
        for p in probs:
            l_new = l_new + jnp.sum(p, axis=0, keepdims=True)
        l_acc[...] = l_new
        a_new = a_acc[...]
        for p, (c, _) in zip(probs, blocks):
            a_new = a_new + _dot_tn(p.astype(BF16), c)
        a_acc[...] = a_new

    def finish(num):
        full = _dot(num.astype(BF16), wuv_ref[...])
        hrow = lax.broadcasted_iota(jnp.int32, (MLA_HEADS, MLA_HEADS * MLA_V), 0)
        hcol = _div_pow2(lax.broadcasted_iota(jnp.int32, (MLA_HEADS, MLA_HEADS * MLA_V), 1), MLA_V)
        rows = []
        for qi in range(nq):
            blk = full[MLA_HEADS * qi:MLA_HEADS * (qi + 1), :]
            rows.append(jnp.sum(jnp.where(hrow == hcol, blk, 0.0), axis=0, keepdims=True))
        o_ref[...] = jnp.concatenate(rows, axis=0)

    zpad = jnp.zeros((LANES - 2 * MLA_ROPE, PAGE_SIZE), F32)

    def rope_block(g):
        krt = krt_refs[g][...]
        return jnp.concatenate([krt, krt * krt, zpad], axis=0).T.astype(BF16)

    pg = npg // ngrp
    groups = [(jnp.concatenate([lat_refs[g][...].astype(BF16) for g in range(pg * t, pg * (t + 1))], axis=0),
               jnp.concatenate([rope_block(g) for g in range(pg * t, pg * (t + 1))], axis=0))
              for t in range(ngrp)]
    scs = scores(groups, None)

    @pl.when(bounded)
    def _():
        accumulate(groups, scs)

    @pl.when(jnp.logical_not(bounded))
    def _():
        for t, (m, l, a) in enumerate(stats(groups, scs)):
            slot = s * ngrp + t
            m_scr[pl.ds(slot, 1), :] = m
            l_scr[pl.ds(slot, 1), :] = l
            a_scr[slot] = a

    def new_block():
        nnew = cnew_ref.shape[0]
        krn = jnp.concatenate([krnew_ref[...], jnp.zeros((nnew, LANES - MLA_ROPE), F32)], axis=1)
        krn = krn + pltpu.roll(krn * krn, MLA_ROPE, 1)
        key = lax.broadcasted_iota(jnp.int32, (nnew, ncol), 0)
        qry = _div_pow2(lax.broadcasted_iota(jnp.int32, (nnew, ncol), 1), MLA_HEADS)
        blocks = [(cnew_ref[...].astype(BF16), krn.astype(BF16))]
        return blocks, scores(blocks, key <= qry)

    @pl.when(jnp.logical_and(last, bounded))
    def _():
        accumulate(*new_block())
        finish(a_acc[...] * _row_to_col(1.0 / l_acc[...], ncol))

    @pl.when(jnp.logical_and(last, jnp.logical_not(bounded)))
    def _():
        blocks, scs_new = new_block()
        (m2, l2, a2), = stats(blocks, scs_new)
        m_scr[nslots - 1:nslots, :] = m2
        l_scr[nslots - 1:nslots, :] = l2
        a_scr[nslots - 1] = a2
        mall = m_scr[0:nslots, :]
        w = jnp.exp2(mall - jnp.max(mall, axis=0, keepdims=True))
        den = jnp.sum(l_scr[0:nslots, :] * w, axis=0, keepdims=True)
        wn = w / den
        num = jnp.zeros((ncol, MLA_KV_LORA), F32)
        for t in range(nslots):
            num = num + a_scr[t] * _row_to_col(wn[t:t + 1, :], ncol)
        finish(num)


def _paged_attention(page_table, bounded, cache_lat, cache_krt, e, qn, rhs2, wuk, cnew, krnew, wuv, bound, *,
                     npg, ngrp, nq):
    nb, npages = page_table.shape
    nsteps = npages // npg
    nnew = cnew.shape[1]
    nslots = nsteps * ngrp + 1

    def page_spec(shape, g):
        return pl.BlockSpec((None, None) + shape, lambda b, s, pt, f: (pt[b, s * npg + g], e, 0, 0))

    per_b3 = lambda b, s, pt, f: (b, 0, 0)
    const2 = lambda b, s, pt, f: (0, 0)
    in_specs = ([page_spec((PAGE_SIZE, MLA_KV_LORA), g) for g in range(npg)]
                + [page_spec((MLA_ROPE, PAGE_SIZE), g) for g in range(npg)]
                + [pl.BlockSpec((None,) + qn.shape[1:], per_b3), pl.BlockSpec((None,) + rhs2.shape[1:], per_b3),
                   pl.BlockSpec(wuk.shape, const2),
                   pl.BlockSpec((None, nnew, MLA_KV_LORA), per_b3), pl.BlockSpec((None, nnew, MLA_ROPE), per_b3),
                   pl.BlockSpec(wuv.shape, const2), pl.BlockSpec(bound.shape, const2)])
    grid_spec = pltpu.PrefetchScalarGridSpec(
        num_scalar_prefetch=2, grid=(nb, nsteps), in_specs=in_specs,
        out_specs=pl.BlockSpec((None, nq, MLA_HEADS * MLA_V), per_b3),
        scratch_shapes=[pltpu.VMEM((MLA_KV_LORA, LANES), BF16),
                        pltpu.VMEM((nslots, LANES), F32), pltpu.VMEM((nslots, LANES), F32),
                        pltpu.VMEM((nslots, LANES, MLA_KV_LORA), F32),
                        pltpu.VMEM((1, LANES), F32), pltpu.VMEM((LANES, MLA_KV_LORA), F32)])
    return pl.pallas_call(
        functools.partial(_paged_kernel, npg=npg, ngrp=ngrp, nsteps=nsteps, nq=nq),
        grid_spec=grid_spec, out_shape=jax.ShapeDtypeStruct((nb, nq, MLA_HEADS * MLA_V), F32),
        compiler_params=_cparams(("arbitrary", "arbitrary")), name="paged_attention",
    )(page_table, bounded, *([cache_lat] * npg), *([cache_krt] * npg), qn, rhs2, wuk, cnew, krnew, wuv, bound)


def _s5_kernel(u_ref, h0r_ref, h0i_ref, lamr_ref, lami_ref, lstep_ref, brm_ref, bim_ref, crm_ref, cim_ref,
               d_ref, wg_ref, bg_ref, o_ref, hr_out, hi_out, xr_scr, xi_scr, hcr, hci, disc, io_scr, *,
               tt, nb, strip, interleave):
    c = pl.program_id(0)

    @pl.when(c == 0)
    def _():
        lr = jnp.minimum(lamr_ref[...], -1e-4)
        li = lami_ref[...]
        dt = jnp.exp(lstep_ref[...])
        mag = jnp.exp(lr * dt)
        abr = mag * jnp.cos(li * dt)
        abi = mag * jnp.sin(li * dt)
        den = lr * lr + li * li
        disc[0:1, :] = abr
        disc[1:2, :] = abi
        disc[2:3, :] = ((abr - 1.0) * lr + abi * li) / den
        disc[3:4, :] = (abi * lr - (abr - 1.0) * li) / den
        hcr[...] = h0r_ref[...]
        hci[...] = h0i_ref[...]

    nlb = S5_WIDTH // LANES
    if interleave:
        for b in range(nb):
            for j in range(nlb):
                c0 = S5_WIDTH * b + LANES * j
                io_scr[j, pl.ds(b, tt, stride=nb), :] = u_ref[:, c0:c0 + LANES]
        u = jnp.concatenate([io_scr[j] for j in range(nlb)], axis=1)
    else:
        u = u_ref[...]
    ub = u.astype(BF16)
    kc = 2 * LANES
    ks = kc * S5_STATE // S5_GROUP
    for k in range(S5_WIDTH // kc):
        cols = slice(kc * k, kc * (k + 1))
        sts = slice(ks * k, ks * (k + 1))
        pr = _dot(ub[:, cols], brm_ref[cols, sts])
        pi = _dot(ub[:, cols], bim_ref[cols, sts])
        cor = disc[2:3, sts]
        coi = disc[3:4, sts]
        xr_scr[:, sts] = cor * pr - coi * pi
        xi_scr[:, sts] = cor * pi + coi * pr

    for s0 in range(0, S5_NSTATE, strip):
        lanes = slice(s0, s0 + strip)
        ar = jnp.broadcast_to(disc[0:1, lanes], (nb, strip))
        ai = jnp.broadcast_to(disc[1:2, lanes], (nb, strip))

        def step(t, carry, lanes=lanes, ar=ar, ai=ai):
            hr, hi = carry
            rows = pl.ds(pl.multiple_of(t * nb, nb), nb)
            nr = ar * hr - ai * hi + xr_scr[rows, lanes]
            ni = ar * hi + ai * hr + xi_scr[rows, lanes]
            xr_scr[rows, lanes] = nr
            xi_scr[rows, lanes] = ni
            return nr, ni

        hr, hi = lax.fori_loop(0, tt, step, (hcr[:, lanes], hci[:, lanes]))
        hcr[:, lanes] = hr
        hci[:, lanes] = hi

    ys = []
    for k in range(S5_WIDTH // kc):
        cols = slice(kc * k, kc * (k + 1))
        sts = slice(ks * k, ks * (k + 1))
        ys.append(_dot(xr_scr[:, sts].astype(BF16), crm_ref[sts, cols])
                  - _dot(xi_scr[:, sts].astype(BF16), cim_ref[sts, cols]))
    y = jnp.concatenate(ys, axis=1) + d_ref[...] * u
    z = jax.nn.gelu(y)
    gate = _sigmoid(_dot(z.astype(BF16), wg_ref[...]) + bg_ref[...])
    if interleave:
        out = z * gate
        for j in range(nlb):
            io_scr[j] = out[:, LANES * j:LANES * (j + 1)]
        for b in range(nb):
            for j in range(nlb):
                c0 = S5_WIDTH * b + LANES * j
                o_ref[:, c0:c0 + LANES] = io_scr[j, pl.ds(b, tt, stride=nb), :].astype(BF16)
    else:
        o_ref[...] = (z * gate).astype(BF16)

    @pl.when(c == pl.num_programs(0) - 1)
    def _():
        hr_out[...] = hcr[...]
        hi_out[...] = hci[...]


def _s5(u, h0r, h0i, lamr, lami, lstep, brm, bim, crm, cim, d, wg, bg, *, tt, nb, strip, interleave):
    steps = u.shape[0] if interleave else u.shape[0] // nb
    blk = tt * nb
    consts = (h0r, h0i, lamr, lami, lstep, brm, bim, crm, cim, d, wg, bg)
    io_spec = row_spec(tt, nb * S5_WIDTH) if interleave else row_spec(blk, S5_WIDTH)
    return pl.pallas_call(
        functools.partial(_s5_kernel, tt=tt, nb=nb, strip=strip, interleave=interleave),
        grid=(steps // tt,),
        in_specs=[io_spec] + [_full_spec(a.shape) for a in consts],
        out_specs=[io_spec, _full_spec((nb, S5_NSTATE)), _full_spec((nb, S5_NSTATE))],
        out_shape=[jax.ShapeDtypeStruct(u.shape, BF16),
                   jax.ShapeDtypeStruct((nb, S5_NSTATE), F32), jax.ShapeDtypeStruct((nb, S5_NSTATE), F32)],
        scratch_shapes=[pltpu.VMEM((blk, S5_NSTATE), F32), pltpu.VMEM((blk, S5_NSTATE), F32),
                        pltpu.VMEM((nb, S5_NSTATE), F32), pltpu.VMEM((nb, S5_NSTATE), F32),
                        pltpu.VMEM((8, S5_NSTATE), F32), pltpu.VMEM((S5_WIDTH // LANES, blk, LANES), F32)],
        compiler_params=_cparams(("arbitrary",)), name="s5",
    )(u, *consts)


def _mm_res_kernel(*refs, nop):
    res_ref = refs[2 * nop]
    o_ref = refs[2 * nop + 1]
    acc = res_ref[...]
    for t in range(nop):
        acc = acc + _dot(refs[t][...].astype(BF16), refs[nop + t][...])
    o_ref[...] = acc


def _mm_res(ops, ws, res, *, tm, op_specs):
    m, n = res.shape
    row = lambda i: (i, 0)
    in_specs = list(op_specs) + [_full_spec(w.shape) for w in ws] + [pl.BlockSpec((tm, n), row)]
    return pl.pallas_call(
        functools.partial(_mm_res_kernel, nop=len(ops)), grid=(m // tm,), in_specs=in_specs,
        out_specs=pl.BlockSpec((tm, n), row), out_shape=jax.ShapeDtypeStruct((m, n), F32),
        compiler_params=_cparams(("arbitrary",)), name="matmul_residual",
    )(*ops, *ws, res)


def _mem_kv_kernel(x_ref, g_ref, w_ref, kg_ref, k_out, v_out):
    mn = _rms(x_ref[...], g_ref[...]).astype(BF16)
    kv = _dot(mn, w_ref[...])
    kg = kg_ref[...]
    for h in range(MEM_HEADS):
        sl = slice(LANES * h, LANES * (h + 1))
        k_out[:, sl] = _rms(kv[:, sl], kg)
    v_out[...] = kv[:, MEM_WIDTH:]


def _mem_kv(x, g, w, kg, *, tm):
    m = x.shape[0]
    row = lambda i: (i, 0)
    return pl.pallas_call(
        _mem_kv_kernel, grid=(m // tm,),
        in_specs=[pl.BlockSpec((tm, D_MODEL), row), _full_spec(g.shape), _full_spec(w.shape), _full_spec(kg.shape)],
        out_specs=[pl.BlockSpec((tm, MEM_WIDTH), row), pl.BlockSpec((tm, MEM_WIDTH), row)],
        out_shape=[jax.ShapeDtypeStruct((m, MEM_WIDTH), F32), jax.ShapeDtypeStruct((m, MEM_WIDTH), F32)],
        compiler_params=_cparams(("arbitrary",)), name="mem_kv",
    )(x, g, w, kg)


def _mem_attn_kernel(h_ref, g_ref, wq_ref, qg_ref, mk_ref, mv_ref, wo_ref, o_ref, *, nb, tl, paired):
    x = h_ref[...].reshape(nb * tl, D_MODEL)
    hn = _rms(x, g_ref[...]).astype(BF16)
    q = _dot(hn, wq_ref[...])
    qg = qg_ref[...] * (MEM_HEAD_DIM ** -0.5 * LOG2E)
    qn = [_rms(q[:, LANES * h:LANES * (h + 1)], qg).astype(BF16) for h in range(MEM_HEADS)]

    def head_block(ref, b, h):
        if paired:
            return jnp.concatenate([ref[b, :, h, :], ref[b, :, MEM_HEADS + h, :]], axis=0).astype(BF16)
        return ref[b, :, LANES * h:LANES * (h + 1)].astype(BF16)

    pairs = [(b, h) for b in range(nb) for h in range(MEM_HEADS)]
    scores = [_dot_nt(qn[h][tl * b:tl * (b + 1), :], head_block(mk_ref, b, h)) for b, h in pairs]
    probs = [jnp.exp2(s - jnp.max(s, axis=-1, keepdims=True)) for s in scores]
    outs = [_dot(p.astype(BF16), head_block(mv_ref, b, h)) / jnp.sum(p, axis=-1, keepdims=True)
            for p, (b, h) in zip(probs, pairs)]
    rows = [jnp.concatenate(outs[MEM_HEADS * b:MEM_HEADS * (b + 1)], axis=1) for b in range(nb)]
    o = (rows[0] if nb == 1 else jnp.concatenate(rows, axis=0)).astype(BF16)
    o_ref[...] = (x + _dot(o, wo_ref[...])).reshape(nb, tl, D_MODEL)


def _mem_attn(h, g, wq, qg, mk, mv, wo, *, nb, tl, layer=None):
    b, l, _ = h.shape
    blk = lambda bi, i: (bi, i, 0)
    if layer is None:
        mem_spec = pl.BlockSpec((nb,) + mk.shape[1:], lambda bi, i: (bi, 0, 0))
    else:
        mem_spec = pl.BlockSpec((None, nb) + mk.shape[2:], lambda bi, i: (layer, bi, 0, 0, 0))
    return pl.pallas_call(
        functools.partial(_mem_attn_kernel, nb=nb, tl=tl, paired=layer is not None), grid=(b // nb, l // tl),
        in_specs=[pl.BlockSpec((nb, tl, D_MODEL), blk), _full_spec(g.shape), _full_spec(wq.shape),
                  _full_spec(qg.shape), mem_spec, mem_spec, _full_spec(wo.shape)],
        out_specs=pl.BlockSpec((nb, tl, D_MODEL), blk),
        out_shape=jax.ShapeDtypeStruct(h.shape, F32),
        compiler_params=_cparams(("arbitrary", "arbitrary")), name="mem_attention",
    )(h, g, wq, qg, mk, mv, wo)


def _mlp_kernel(h_ref, g_ref, wu_ref, wd_ref, o_ref, xn_scr, acc_scr):
    j = pl.program_id(1)

    @pl.when(j == 0)
    def _():
        xn_scr[...] = _rms(h_ref[...], g_ref[...]).astype(BF16)
        acc_scr[...] = jnp.zeros(acc_scr.shape, F32)

    a = _dot(xn_scr[...], wu_ref[...])
    a = jnp.square(jnp.maximum(a, 0.0)).astype(BF16)
    acc_scr[...] += _dot(a, wd_ref[...])

    @pl.when(j == pl.num_programs(1) - 1)
    def _():
        o_ref[...] = h_ref[...] + acc_scr[...]


def _mlp(h, g, wu, wd, *, tm, tf):
    m = h.shape[0]
    return pl.pallas_call(
        _mlp_kernel, grid=(m // tm, D_FF // tf),
        in_specs=[pl.BlockSpec((tm, D_MODEL), lambda i, j: (i, 0)), _full_spec(g.shape),
                  pl.BlockSpec((D_MODEL, tf), lambda i, j: (0, j)), pl.BlockSpec((tf, D_MODEL), lambda i, j: (j, 0))],
        out_specs=pl.BlockSpec((tm, D_MODEL), lambda i, j: (i, 0)),
        out_shape=jax.ShapeDtypeStruct((m, D_MODEL), F32),
        scratch_shapes=[pltpu.VMEM((tm, D_MODEL), BF16), pltpu.VMEM((tm, D_MODEL), F32)],
        compiler_params=_cparams(("arbitrary", "arbitrary")), name="mlp",
    )(h, g, wu, wd)


def _norm_mm_kernel(h_ref, g_ref, w_ref, o_ref, xn_scr):
    @pl.when(pl.program_id(1) == 0)
    def _():
        xn_scr[...] = _rms(h_ref[...], g_ref[...]).astype(BF16)

    o_ref[...] = _dot(xn_scr[...], w_ref[...])


def _norm_mm(h, g, w, *, tm, tn):
    m = h.shape[0]
    n = w.shape[1]
    return pl.pallas_call(
        _norm_mm_kernel, grid=(m // tm, n // tn),
        in_specs=[pl.BlockSpec((tm, D_MODEL), lambda i, j: (i, 0)), _full_spec(g.shape),
                  pl.BlockSpec((D_MODEL, tn), lambda i, j: (0, j))],
        out_specs=pl.BlockSpec((tm, tn), lambda i, j: (i, j)),
        out_shape=jax.ShapeDtypeStruct((m, n), F32),
        scratch_shapes=[pltpu.VMEM((tm, D_MODEL), BF16)],
        compiler_params=_cparams(("arbitrary", "arbitrary")), name="norm_matmul",
    )(h, g, w)


def _hgrn_kernel(q_ref, f_ref, i_ref, g_ref, lbp_ref, on_ref, s0_ref, o_ref, s_out, s_scr, *,
                 chunk, nchunk, layer, l_valid):
    c = pl.program_id(1)
    tb = chunk * nchunk

    @pl.when(c == 0)
    def _():
        s_scr[...] = s0_ref[...]

    lbp = lbp_ref[...]
    e = jnp.exp(lbp - jnp.max(lbp, axis=0, keepdims=True))
    sm = e / jnp.sum(e, axis=0, keepdims=True)
    lb = jnp.sum(sm[0:layer + 1, :], axis=0, keepdims=True) - sm[0:1, :]

    q = q_ref[...]
    qa = q * _sigmoid(q)
    fg = lb + (1.0 - lb) * _sigmoid(f_ref[...])
    logf = jnp.log(fg)
    kk = 1.0 - fg
    v = i_ref[...]
    if l_valid is not None:
        valid = (lax.broadcasted_iota(jnp.int32, (tb, 1), 0) + c * tb) < l_valid
        logf = jnp.where(valid, logf, 0.0)
        kk = jnp.where(valid, kk, 0.0)
    vb = v.astype(BF16)

    tr = lax.broadcasted_iota(jnp.int32, (tb, tb), 0)
    tc = lax.broadcasted_iota(jnp.int32, (tb, tb), 1)
    same_chunk = _div_pow2(tr, chunk) == _div_pow2(tc, chunk)
    tri = jnp.where(same_chunk, jnp.where(tr >= tc, 1.0, 0.0), 0.0).astype(BF16)
    hi = logf.astype(BF16)
    lo = (logf - hi.astype(F32)).astype(BF16)
    bcum = _dot(tri, hi) + _dot(tri, lo)
    qhat = (qa * jnp.exp(bcum)).astype(BF16)

    nsub = chunk // HGRN_SUB
    khat, dec, qloc, kloc, masks = [], [], [], [], []
    spread = jnp.zeros((1, bcum.shape[1]), F32)
    for ci in range(nchunk):
        c0 = ci * chunk
        blast = bcum[c0 + chunk - 1:c0 + chunk, :]
        khat.append((kk[c0:c0 + chunk, :] * jnp.exp(blast - bcum[c0:c0 + chunk, :])).astype(BF16))
        dec.append(jnp.exp(blast))
        for i in range(nsub):
            r0 = c0 + i * HGRN_SUB
            r1 = r0 + HGRN_SUB
            base = bcum[r0 - 1:r0, :] if i > 0 else jnp.zeros((1, bcum.shape[1]), F32)
            spread = jnp.minimum(spread, bcum[r1 - 1:r1, :] - base)
            qloc.append((qa[r0:r1, :] * jnp.exp(bcum[r0:r1, :] - base)).astype(BF16))
            kloc.append((kk[c0:r1, :] * jnp.exp(jnp.minimum(base - bcum[c0:r1, :], HGRN_EXP_CLAMP))).astype(BF16))
    wild = jnp.min(spread) < -HGRN_EXP_CLAMP
    for i in range(nsub):
        ncols = (i + 1) * HGRN_SUB
        ar = lax.broadcasted_iota(jnp.int32, (HGRN_SUB, ncols), 0) + i * HGRN_SUB
        ac = lax.broadcasted_iota(jnp.int32, (HGRN_SUB, ncols), 1)
        in_block = ac >= i * HGRN_SUB
        masks.append(jnp.logical_and(ar >= ac, jnp.logical_not(jnp.logical_and(wild, in_block))))

    hsl = [slice(HGRN_DK * h, HGRN_DK * (h + 1)) for h in range(HGRN_HEADS)]

    def in_block_exact():
        pos = jnp.bitwise_and(lax.broadcasted_iota(jnp.int32, (tb, 1), 0), HGRN_SUB - 1)
        out = jnp.zeros((tb, bcum.shape[1]), F32)
        for j in range(HGRN_SUB):
            ok = pos >= j
            kj, bj, vj = (kk, bcum, v) if j == 0 else (pltpu.roll(x, j, 0) for x in (kk, bcum, v))
            e = jnp.where(ok, qa * kj * jnp.exp(jnp.where(ok, bcum - bj, 0.0)), 0.0)
            out = out + jnp.concatenate(
                [jnp.sum(e[:, sl], axis=-1, keepdims=True) * vj[:, sl] for sl in hsl], axis=1)
        return out

    blocks = [(ci, i) for ci in range(nchunk) for i in range(nsub)]
    att = [[_dot_nt(qloc[ci * nsub + i][:, sl], kloc[ci * nsub + i][:, sl]) for ci, i in blocks] for sl in hsl]
    att = [[jnp.where(masks[i], a, 0.0).astype(BF16) for a, (ci, i) in zip(row, blocks)] for row in att]
    intra = [[_dot(a, vb[ci * chunk:ci * chunk + (i + 1) * HGRN_SUB, sl]) for a, (ci, i) in zip(row, blocks)]
             for row, sl in zip(att, hsl)]
    kv = [[_dot_tn(khat[ci][:, sl], vb[ci * chunk:(ci + 1) * chunk, sl]) for ci in range(nchunk)] for sl in hsl]
    dcol = [[_row_to_col(dec[ci][:, sl], HGRN_DK) for ci in range(nchunk)] for sl in hsl]
    st = [s_scr[h] for h in range(HGRN_HEADS)]
    inter = [[] for _ in hsl]
    for ci in range(nchunk):
        rows = slice(ci * chunk, (ci + 1) * chunk)
        for h, sl in enumerate(hsl):
            inter[h].append(_dot(qhat[rows, sl], st[h].astype(BF16)))
        for h in range(HGRN_HEADS):
            st[h] = dcol[h][ci] * st[h] + kv[h][ci]
    o_heads = []
    for h in range(HGRN_HEADS):
        s_scr[h] = st[h]
        parts = [inter[h][ci][i * HGRN_SUB:(i + 1) * HGRN_SUB, :] + intra[h][ci * nsub + i] for ci, i in blocks]
        o_heads.append(parts[0] if len(parts) == 1 else jnp.concatenate(parts, axis=0))

    o = jnp.concatenate(o_heads, axis=1)
    o = lax.cond(wild, lambda: o + in_block_exact(), lambda: o)
    g = g_ref[...]
    o_ref[...] = (_rms(o, on_ref[...]) * (g * _sigmoid(g))).astype(BF16)

    @pl.when(c == pl.num_programs(1) - 1)
    def _():
        s_out[...] = s_scr[...]


def _hgrn(proj, lbp, on, s0, *, chunk, nchunk, layer, l_valid):
    b, l, _ = proj.shape
    w = D_MODEL
    tb = chunk * nchunk

    def col(k):
        return pl.BlockSpec((None, tb, w), lambda bi, c: (bi, c, k))

    st_spec = pl.BlockSpec((None, HGRN_HEADS, HGRN_DK, HGRN_DK), lambda bi, c: (bi, 0, 0, 0))
    return pl.pallas_call(
        functools.partial(_hgrn_kernel, chunk=chunk, nchunk=nchunk, layer=layer, l_valid=l_valid),
        grid=(b, l // tb),
        in_specs=[col(0), col(1), col(2), col(3), _full_spec(lbp.shape), _full_spec(on.shape), st_spec],
        out_specs=[pl.BlockSpec((None, tb, w), lambda bi, c: (bi, c, 0)), st_spec],
        out_shape=[jax.ShapeDtypeStruct((b, l, w), BF16), jax.ShapeDtypeStruct(s0.shape, F32)],
        scratch_shapes=[pltpu.VMEM((HGRN_HEADS, HGRN_DK, HGRN_DK), F32)],
        compiler_params=_cparams(("arbitrary", "arbitrary")), name="hgrn",
    )(proj, proj, proj, proj, lbp, on, s0)


def _pad_last(x, n):
    return jnp.pad(x, [(0, 0)] * (x.ndim - 1) + [(0, n - x.shape[-1])])


def _head_pad(w, per):
    k = w.shape[0]
    return _pad_last(w.reshape(k, -1, per), LANES).reshape(k, -1)


def _rope_tables(pos):
    half = MLA_ROPE // 2
    inv = ROPE_THETA ** (-jnp.arange(half, dtype=F32) / half)
    ang = pos.astype(F32)[:, None] * inv[None, :]
    cos, sin = jnp.cos(ang), jnp.sin(ang)
    n = pos.shape[0]
    z = lambda w: jnp.zeros((n, w), F32)
    scale = MLA_QK ** -0.5 * LOG2E
    cq = scale * jnp.concatenate([jnp.ones((n, MLA_NOPE), F32), cos, cos, z(LANES - MLA_QK)], axis=1)
    s1q = scale * jnp.concatenate([z(MLA_NOPE + half), sin, z(LANES - MLA_QK)], axis=1)
    s2q = scale * jnp.concatenate([z(MLA_NOPE), -sin, z(half + LANES - MLA_QK)], axis=1)
    ck = jnp.concatenate([cos, cos, z(LANES - MLA_ROPE)], axis=1)
    s1k = jnp.concatenate([z(half), sin, z(LANES - MLA_ROPE)], axis=1)
    s2k = jnp.concatenate([-sin, z(LANES - half)], axis=1)
    return (cq, s1q, s2q, ck, s1k, s2k)


def _block_diag(x):
    g, a, b = x.shape
    eye = jnp.eye(g, dtype=x.dtype)
    return (x[:, :, None, :] * eye[:, None, :, None]).reshape(g * a, g * b)


def kernel(x_prompt, x_sample, cache_mla_latent, cache_mla_krope, state_s5_re, state_s5_im, state_hgrn, cache_mem_k, cache_mem_v, page_table, mem_prompt, norm_mix, norm_mem, norm_memsrc, norm_mlp, w_mem_q, w_mem_k, w_mem_v, w_mem_o, mem_q_gain, mem_k_gain, w_mlp_up, w_mlp_down, w_in_even, mla_cq_norm, mla_ckv_norm, w_mla_uq, w_mla_ukv, mla_qn_nope, mla_qn_rope, mla_kn_nope, mla_kn_rope, s5_lambda_re, s5_lambda_im, s5_log_step, s5_b_re, s5_b_im, s5_c_re, s5_c_im, s5_d, s5_w_glu, s5_b_glu, w_out_even, w_in_odd, hgrn_lower_bounds, hgrn_out_norm, w_out_odd):
    bsz, seq, _ = x_prompt.shape
    dbs, dseq, _ = x_sample.shape
    depth = norm_mix.shape[0]
    past_len = page_table.shape[1] * PAGE_SIZE
    ns = SAMPLE_PAD
    mem_len = mem_prompt.shape[1]
    row2 = lambda a: a.reshape(1, -1).astype(F32)

    hp = x_prompt.reshape(bsz * seq, D_MODEL)
    hs = jnp.pad(x_sample, ((0, 0), (0, ns - dseq), (0, 0))).reshape(dbs * ns, D_MODEL)

    tabs_p = _rope_tables(jnp.arange(seq, dtype=jnp.int32))
    pos_s = past_len + jnp.arange(ns, dtype=jnp.int32)
    tabs_s = tuple(jnp.tile(t, (dbs, 1)) for t in _rope_tables(pos_s))

    outs_p = {k: [] for k in ("lat", "kr", "s5r", "s5i", "hg", "mk", "mv")}
    outs_s = {k: [] for k in ("lat", "kr", "s5r", "s5i", "hg")}

    tm_p = TM_EVEN_PROJ
    nl_p = seq // tm_p
    tm_r = TM_RESIDUAL
    nl_r = seq // tm_r

    for l in range(depth):
        if l % 2 == 0:
            e = l // 2
            w_in = w_in_even[e]
            o1 = MLA_Q_LORA + MLA_KV_LORA
            wp = jnp.concatenate([w_in[:, :o1], _pad_last(w_in[:, o1:o1 + MLA_ROPE], LANES),
                                  w_in[:, o1 + MLA_ROPE:]], axis=1).astype(BF16)
            wuq = _head_pad(w_mla_uq[e], MLA_QK).astype(BF16)
            ukv = w_mla_ukv[e].reshape(MLA_KV_LORA, MLA_HEADS, MLA_NOPE + MLA_V)
            wuk_c = ukv[:, :, :MLA_NOPE].reshape(MLA_KV_LORA, -1)
            wuv_c = ukv[:, :, MLA_NOPE:].reshape(MLA_KV_LORA, -1)
            wkv = jnp.concatenate([_head_pad(wuk_c, MLA_NOPE), _head_pad(wuv_c, MLA_V)], axis=1).astype(BF16)
            qg = _pad_last(jnp.concatenate([mla_qn_nope[e], mla_qn_rope[e], mla_qn_rope[e]])[None, :], LANES)
            kg = _pad_last(jnp.concatenate([mla_kn_nope[e], mla_kn_rope[e], mla_kn_rope[e]])[None, :], LANES)
            cqn = row2(mla_cq_norm[e])
            ckvn = row2(mla_ckv_norm[e])
            g_mix = row2(norm_mix[l])
            score_bound = MLA_QK ** 0.5 * LOG2E * jnp.max(jnp.abs(qg)) * jnp.max(jnp.abs(kg))
            bounded = (score_bound <= MAX_SCORE_BOUND).astype(jnp.int32).reshape(1)
            lane = jnp.arange(LANES)
            aug = (jnp.where(lane == MLA_QK, 1.0, 0.0).astype(F32)[None, :],
                   jnp.where(lane == MLA_QK, -score_bound, 0.0).astype(F32)[None, :],
                   jnp.tile(jnp.where(lane == MLA_V, 1.0, 0.0).astype(F32), MLA_HEADS)[None, :])

            brm = _block_diag(jnp.swapaxes(s5_b_re[e], 1, 2)).astype(BF16)
            bim = _block_diag(jnp.swapaxes(s5_b_im[e], 1, 2)).astype(BF16)
            crm = _block_diag(jnp.swapaxes(s5_c_re[e], 1, 2)).astype(BF16)
            cim = _block_diag(jnp.swapaxes(s5_c_im[e], 1, 2)).astype(BF16)
            lamr = row2(s5_lambda_re[e])
            lami = row2(s5_lambda_im[e])
            lstep = row2(jnp.repeat(s5_log_step[e], S5_STATE))
            s5_consts = (lamr, lami, lstep, brm, bim, crm, cim, row2(s5_d[e]), s5_w_glu[e].astype(BF16),
                         row2(s5_b_glu[e]))
            w_out = w_out_even[e]
            wo_att_c = w_out[:MLA_HEADS * MLA_V].astype(BF16)
            wo_att_p = _pad_last(w_out[:MLA_HEADS * MLA_V].reshape(MLA_HEADS, MLA_V, D_MODEL).swapaxes(1, 2),
                                 LANES).swapaxes(1, 2).reshape(HP, D_MODEL).astype(BF16)
            wo_s5 = w_out[MLA_HEADS * MLA_V:].astype(BF16)

            q, k, v, ckv, kr, u = _even_proj(
                hp, g_mix, wp, cqn, wuq, qg, ckvn, wkv, kg, aug, tabs_p, tm=tm_p,
                u_shape=(seq, bsz * S5_WIDTH), emit_qk=False,
                u_spec=pl.BlockSpec((tm_p, S5_WIDTH), lambda i: (i % nl_p, i // nl_p)))
            o_att = _flash_attention(bounded, q.reshape(bsz, seq, HP), k.reshape(bsz, seq, HP), v,
                                     tq=TQ_ATTN, hg=ATTN_HEAD_GROUP)
            z0 = jnp.zeros((bsz, S5_NSTATE), F32)
            o_s5, hr, hi = _s5(u, z0, z0, *s5_consts, tt=S5_STEPS, nb=bsz, strip=S5_STRIP, interleave=True)
            hp = _mm_res([o_att.reshape(bsz * seq, HP), o_s5], [wo_att_p, wo_s5], hp,
                         tm=tm_r, op_specs=[row_spec(tm_r, HP),
                                            pl.BlockSpec((tm_r, S5_WIDTH), lambda i: (i % nl_r, i // nl_r))])
            outs_p["lat"].append(ckv.reshape(bsz, seq, MLA_KV_LORA))
            outs_p["kr"].append(kr.reshape(bsz, seq, MLA_ROPE))
            outs_p["s5r"].append(hr.reshape(bsz, S5_GROUPS, S5_STATE))
            outs_p["s5i"].append(hi.reshape(bsz, S5_GROUPS, S5_STATE))

            m_s = dbs * ns
            q, k, v, ckv, kr, u, qk = _even_proj(
                hs, g_mix, wp, cqn, wuq, qg, ckvn, wkv, kg, aug, tabs_s, tm=tm_p,
                u_shape=(m_s, S5_WIDTH), u_spec=row_spec(tm_p, S5_WIDTH), emit_qk=True)
            del q, k, v
            ckv3 = ckv.reshape(dbs, ns, MLA_KV_LORA)
            kr3 = kr.reshape(dbs, ns, MLA_ROPE)
            qk4 = qk.reshape(dbs, ns, MLA_HEADS, LANES)
            eye_h = jnp.eye(MLA_HEADS, dtype=BF16)
            ncols = ns * MLA_HEADS
            sub = LANES // MLA_HEADS
            nblk = MLA_NOPE // sub
            qn = (jnp.transpose(qk4[..., :MLA_NOPE], (0, 2, 3, 1))[..., None]
                  * eye_h[None, :, None, None, :])
            qn = qn.reshape(dbs, MLA_HEADS, nblk, sub, ncols).swapaxes(1, 2).reshape(dbs, MLA_HEADS * MLA_NOPE, ncols)
            qn = _pad_last(qn, LANES)
            wuk_p = (wuk_c.reshape(MLA_KV_LORA, MLA_HEADS, nblk, sub).swapaxes(1, 2)
                     .reshape(MLA_KV_LORA, MLA_HEADS * MLA_NOPE).astype(BF16))
            qr = jnp.transpose(qk4[..., MLA_NOPE:MLA_QK], (0, 3, 1, 2)).reshape(dbs, MLA_ROPE, ncols)
            qr = _pad_last(qr, LANES)
            colmask = (jnp.arange(LANES) < ncols)
            e16 = ((jnp.arange(LANES)[:, None] // sub == (jnp.arange(LANES)[None, :] % MLA_HEADS))
                   & colmask[None, :]).astype(BF16)
            onr = jnp.broadcast_to(colmask[None, :], (MLA_ROPE, LANES)).astype(BF16)
            zb = lambda r: jnp.zeros((dbs, r, LANES), BF16)
            bc = lambda x: jnp.broadcast_to(x[None], (dbs,) + x.shape)
            rhs2 = jnp.concatenate([
                jnp.concatenate([bc(e16), zb(LANES)], axis=2),
                jnp.concatenate([zb(MLA_ROPE), qr], axis=2),
                jnp.concatenate([bc(onr), zb(MLA_ROPE)], axis=2),
                jnp.zeros((dbs, LANES - 2 * MLA_ROPE, 2 * LANES), BF16)], axis=1)
            nnew = NEW_KEYS_PAD
            cnew = jnp.pad(ckv3, ((0, 0), (0, nnew - ns), (0, 0)))
            krnew = jnp.pad(kr3, ((0, 0), (0, nnew - ns), (0, 0)))
            o_att_s = _paged_attention(page_table, bounded, cache_mla_latent, jnp.swapaxes(cache_mla_krope, 2, 3), e,
                                       qn, rhs2, wuk_p, cnew, krnew, wuv_c.astype(BF16),
                                       jnp.full((1, LANES), score_bound, F32),
                                       npg=PAGES_PER_STEP, ngrp=PAGE_GROUPS, nq=ns)
            u_tb = jnp.transpose(u.reshape(dbs, ns, S5_WIDTH)[:, :dseq], (1, 0, 2)).reshape(dseq * dbs, S5_WIDTH)
            o_s5, hr, hi = _s5(u_tb, state_s5_re[e].reshape(dbs, S5_NSTATE), state_s5_im[e].reshape(dbs, S5_NSTATE),
                               *s5_consts, tt=dseq, nb=dbs, strip=S5_STRIP, interleave=False)
            o_s5 = jnp.transpose(o_s5.reshape(dseq, dbs, S5_WIDTH), (1, 0, 2))
            o_s5 = jnp.pad(o_s5, ((0, 0), (0, ns - dseq), (0, 0))).reshape(m_s, S5_WIDTH)
            hs = _mm_res([o_att_s.reshape(m_s, MLA_HEADS * MLA_V), o_s5], [wo_att_c, wo_s5], hs, tm=m_s,
                         op_specs=[row_spec(m_s, MLA_HEADS * MLA_V), row_spec(m_s, S5_WIDTH)])
            outs_s["lat"].append(ckv3[:, :dseq])
            outs_s["kr"].append(kr3[:, :dseq])
            outs_s["s5r"].append(hr.reshape(dbs, S5_GROUPS, S5_STATE))
            outs_s["s5i"].append(hi.reshape(dbs, S5_GROUPS, S5_STATE))
        else:
            o = l // 2
            g_mix = row2(norm_mix[l])
            w_in = w_in_odd[o].astype(BF16)
            w_out = w_out_odd[o].astype(BF16)
            on = row2(hgrn_out_norm[o])
            lbp = hgrn_lower_bounds.astype(F32)

            proj = _norm_mm(hp, g_mix, w_in, tm=TM_NORM_MM, tn=TN_NORM_MM)
            s_zero = jnp.zeros((bsz, HGRN_HEADS, HGRN_DK, HGRN_DK), F32)
            og, st = _hgrn(proj.reshape(bsz, seq, 4 * D_MODEL), lbp, on, s_zero, chunk=HGRN_CHUNK, nchunk=HGRN_NCHUNK, layer=l,
                           l_valid=None)
            hp = _mm_res([og.reshape(bsz * seq, D_MODEL)], [w_out], hp, tm=tm_r, op_specs=[row_spec(tm_r, D_MODEL)])
            outs_p["hg"].append(st)

            m_s = dbs * ns
            proj = _norm_mm(hs, g_mix, w_in, tm=m_s, tn=TN_NORM_MM)
            lpad = HGRN_SUB
            proj = jnp.pad(proj.reshape(dbs, ns, 4 * D_MODEL), ((0, 0), (0, lpad - ns), (0, 0)))
            og, st = _hgrn(proj, lbp, on, state_hgrn[o], chunk=lpad, nchunk=1, layer=l, l_valid=dseq)
            hs = _mm_res([og[:, :ns].reshape(m_s, D_MODEL)], [w_out], hs, tm=m_s, op_specs=[row_spec(m_s, D_MODEL)])
            outs_s["hg"].append(st)

        g_mem = row2(norm_mem[l])
        wq = w_mem_q[l].astype(BF16)
        wo = w_mem_o[l].astype(BF16)
        mqg = row2(mem_q_gain[l])
        wkv_m = jnp.concatenate([w_mem_k[l], w_mem_v[l]], axis=1).astype(BF16)
        mk, mv = _mem_kv(mem_prompt.reshape(bsz * mem_len, D_MODEL), row2(norm_memsrc[l]), wkv_m,
                         row2(mem_k_gain[l]), tm=TM_MEM_KV)
        mk = mk.reshape(bsz, mem_len, MEM_WIDTH)
        mv = mv.reshape(bsz, mem_len, MEM_WIDTH)
        outs_p["mk"].append(mk.reshape(bsz, mem_len, MEM_HEADS, MEM_HEAD_DIM))
        outs_p["mv"].append(mv.reshape(bsz, mem_len, MEM_HEADS, MEM_HEAD_DIM))
        hp = _mem_attn(hp.reshape(bsz, seq, D_MODEL), g_mem, wq, mqg, mk, mv, wo,
                       nb=1, tl=TL_MEM).reshape(bsz * seq, D_MODEL)
        pair_shape = (depth, dbs, mem_len // 2, 2 * MEM_HEADS, MEM_HEAD_DIM)
        hs = _mem_attn(hs.reshape(dbs, ns, D_MODEL), g_mem, wq, mqg, cache_mem_k.reshape(pair_shape),
                       cache_mem_v.reshape(pair_shape), wo, nb=SAMPLE_MEM_SEQS, tl=ns, layer=l).reshape(dbs * ns, D_MODEL)

        g_mlp = row2(norm_mlp[l])
        wu = w_mlp_up[l].astype(BF16)
        wd = w_mlp_down[l].astype(BF16)
        hp = _mlp(hp, g_mlp, wu, wd, tm=TM_MLP, tf=TF_MLP)
        hs = _mlp(hs, g_mlp, wu, wd, tm=dbs * ns, tf=TF_MLP)

    y_p = hp.reshape(bsz, seq, D_MODEL)
    y_s = hs.reshape(dbs, ns, D_MODEL)[:, :dseq]
    return (y_p, y_s,
            jnp.stack(outs_p["lat"], axis=1), jnp.stack(outs_p["kr"], axis=1),
            jnp.stack(outs_p["s5r"]), jnp.stack(outs_p["s5i"]), jnp.stack(outs_p["hg"]),
            jnp.stack(outs_p["mk"]), jnp.stack(outs_p["mv"]),
            jnp.stack(outs_s["lat"], axis=1), jnp.stack(outs_s["kr"], axis=1),
            jnp.stack(outs_s["s5r"]), jnp.stack(outs_s["s5i"]), jnp.stack(outs_s["hg"]))
```
